```python
import math
import jax
import jax.numpy as jnp
from jax import lax
import numpy as np

D_MODEL = 1024
BATCH = 8
SEQ = 2048
DEPTH = 4
DEC_BATCH = 32
DEC_SEQ = 4
PAST_LEN = 16384
PAGE_SIZE = 128

N_MIXERS = 3
LAYER_MIXER = tuple(i % N_MIXERS for i in range(DEPTH))
N_GDN = LAYER_MIXER.count(0)
N_MLA = LAYER_MIXER.count(1)
N_MLSTM = LAYER_MIXER.count(2)

GDN_HEADS = D_MODEL // 128
GDN_DK = 128
GDN_DV = 128
GDN_CONV = 4
GDN_CHUNK = 64
GDN_QKV = GDN_HEADS * (2 * GDN_DK + GDN_DV)
GDN_Z = GDN_HEADS * GDN_DV

MLA_HEADS = D_MODEL // 128
MLA_NOPE = 128
MLA_ROPE = 64
MLA_V = 128
MLA_Q_RANK = 3 * D_MODEL // 8
MLA_KV_RANK = D_MODEL // 4
MLA_SCALE = (MLA_NOPE + MLA_ROPE) ** -0.5
ROPE_THETA = 10000.0
Q_BLOCK = 128

ML_HEADS = 4
ML_DK = D_MODEL // (2 * ML_HEADS)
ML_DV = D_MODEL // ML_HEADS
ML_CHUNK = 64
ML_QK = ML_HEADS * ML_DK
ML_V = ML_HEADS * ML_DV

D_FF = ((8 * D_MODEL // 3 + 127) // 128) * 128
EPS = 1e-6

kernel_name = 'hybrid_gdn_mla_mlstm_macaron_step'


def _rmsnorm(x, g):
    xf = x.astype(jnp.float32)
    y = xf * lax.rsqrt(jnp.mean(xf * xf, axis=-1, keepdims=True) + EPS)
    return (y * g.astype(jnp.float32)).astype(x.dtype)


def _l2norm(x):
    return x * lax.rsqrt(jnp.sum(x * x, axis=-1, keepdims=True) + EPS)


def _swiglu(x, w_up, w_down):
    g, u = jnp.split(x @ w_up, 2, axis=-1)
    return (jax.nn.silu(g) * u) @ w_down


def _rope(x, pos):
    half = x.shape[-1] // 2
    inv = ROPE_THETA ** (-jnp.arange(half, dtype=jnp.float32) / half)
    ang = pos.astype(jnp.float32)[:, None] * inv
    ang = ang.reshape((ang.shape[0],) + (1,) * (x.ndim - 3) + (half,))
    cos, sin = jnp.cos(ang), jnp.sin(ang)
    xf = x.astype(jnp.float32)
    x1, x2 = xf[..., :half], xf[..., half:]
    return jnp.concatenate([x1 * cos - x2 * sin, x2 * cos + x1 * sin], axis=-1).astype(x.dtype)


def _to_chunks(t, L):
    B, T, H = t.shape[:3]
    t = t.reshape((B, T // L, L, H) + t.shape[3:])
    return jnp.moveaxis(t, (1, 3), (0, 2))


def _from_chunks(t):
    N, B, H, L = t.shape[:4]
    t = jnp.moveaxis(t, (0, 2), (1, 3))
    return t.reshape((B, N * L, H) + t.shape[4:])


def _causal_conv(xp, buf, w):
    T = xp.shape[1]
    full = jnp.concatenate([buf.astype(xp.dtype), xp], axis=1)
    y = sum(full[:, j:j + T] * w[j] for j in range(w.shape[0]))
    return y, full[:, T:]


def _gated_delta(q, k, v, g, beta, S0):
    B, T, H, _ = q.shape
    DV = v.shape[-1]
    L = math.gcd(T, GDN_CHUNK)
    qc, kc, vc = _to_chunks(q, L), _to_chunks(k, L), _to_chunks(v, L)
    gc = jnp.cumsum(_to_chunks(g, L), axis=-1)
    bc = _to_chunks(beta, L)
    incl = jnp.tril(jnp.ones((L, L), bool))
    strict = jnp.tril(jnp.ones((L, L), bool), -1)
    diff = gc[..., :, None] - gc[..., None, :]
    decay_incl = jnp.exp(jnp.where(incl, diff, -jnp.inf))
    decay_strict = jnp.where(strict, decay_incl, 0.0)
    kb = kc * bc[..., None]
    a_mat = jnp.eye(L, dtype=jnp.float32) + jnp.einsum('nbhid,nbhjd->nbhij', kb, kc) * decay_strict
    rhs = jnp.concatenate([vc * bc[..., None], kb * jnp.exp(gc)[..., None]], axis=-1)
    sol = lax.linalg.triangular_solve(a_mat, rhs, left_side=True, lower=True, unit_diagonal=True)
    u, w = sol[..., :DV], sol[..., DV:]
    attn = jnp.einsum('nbhid,nbhjd->nbhij', qc, kc) * decay_incl
    qg = qc * jnp.exp(gc)[..., None]
    kdec = kc * jnp.exp(gc[..., -1:] - gc)[..., None]
    glast = jnp.exp(gc[..., -1])

    def step(S, xs):
        u_n, w_n, qg_n, attn_n, kdec_n, gl_n = xs
        v_new = u_n - jnp.einsum('bhld,bhde->bhle', w_n, S)
        o = jnp.einsum('bhld,bhde->bhle', qg_n, S) + jnp.einsum('bhij,bhje->bhie', attn_n, v_new)
        S = S * gl_n[..., None, None] + jnp.einsum('bhld,bhle->bhde', kdec_n, v_new)
        return S, o

    S, o = lax.scan(step, S0, (u, w, qg, attn, kdec, glast))
    return _from_chunks(o), S


def _gdn_mixer(x, S0, conv_buf, w_in, conv_w, a_log, dt_bias, norm_w, w_out):
    B, T, _ = x.shape
    f32 = jnp.float32
    proj = x @ w_in
    qkv_pre, z, b, a = jnp.split(proj, [GDN_QKV, GDN_QKV + GDN_Z, GDN_QKV + GDN_Z + GDN_HEADS], axis=-1)
    qkv, new_buf = _causal_conv(qkv_pre, conv_buf, conv_w)
    qkv = jax.nn.silu(qkv.astype(f32))
    q, k, v = jnp.split(qkv, [GDN_HEADS * GDN_DK, 2 * GDN_HEADS * GDN_DK], axis=-1)
    q = _l2norm(q.reshape(B, T, GDN_HEADS, GDN_DK)) * GDN_DK ** -0.5
    k = _l2norm(k.reshape(B, T, GDN_HEADS, GDN_DK))
    v = v.reshape(B, T, GDN_HEADS, GDN_DV)
    beta = jax.nn.sigmoid(b.astype(f32))
    g = -jnp.exp(a_log.astype(f32)) * jax.nn.softplus(a.astype(f32) + dt_bias.astype(f32))
    o, S = _gated_delta(q, k, v, g, beta, S0.astype(f32))
    o = _rmsnorm(o, norm_w) * jax.nn.silu(z.astype(f32).reshape(B, T, GDN_HEADS, GDN_DV))
    y = o.reshape(B, T, GDN_Z).astype(x.dtype) @ w_out
    return y, S.astype(x.dtype), new_buf


def _mlstm_chunked(q, k, v, ig, lf, C0, n0, m0):
    T = q.shape[1]
    L = math.gcd(T, ML_CHUNK)
    incl = jnp.tril(jnp.ones((L, L), bool))

    def step(carry, xs):
        C, n, m = carry
        q_n, k_n, v_n, ig_n, lf_n = xs
        b = jnp.cumsum(lf_n, axis=-1)
        dmat = jnp.where(incl, b[..., :, None] - b[..., None, :] + ig_n[..., None, :], -jnp.inf)
        inter = b + m[..., None]
        m_t = jnp.maximum(inter, jnp.max(dmat, axis=-1))
        w_intra = jnp.exp(dmat - m_t[..., None])
        w_inter = jnp.exp(inter - m_t)
        s = jnp.einsum('bhid,bhjd->bhij', q_n, k_n) * w_intra
        num = w_inter[..., None] * jnp.einsum('bhld,bhde->bhle', q_n, C) + jnp.einsum('bhij,bhje->bhie', s, v_n)
        den = w_inter * jnp.einsum('bhld,bhd->bhl', q_n, n) + jnp.sum(s, axis=-1)
        h = num / jnp.maximum(jnp.abs(den), jnp.exp(-m_t))[..., None]
        m_new = m_t[..., -1]
        w_state = jnp.exp(dmat[..., -1, :] - m_new[..., None])
        carry_decay = jnp.exp(inter[..., -1] - m_new)
        C = carry_decay[..., None, None] * C + jnp.einsum('bhl,bhld,bhle->bhde', w_state, k_n, v_n)
        n = carry_decay[..., None] * n + jnp.einsum('bhl,bhld->bhd', w_state, k_n)
        return (C, n, m_new), h

    xs = (_to_chunks(q, L), _to_chunks(k, L), _to_chunks(v, L), _to_chunks(ig, L), _to_chunks(lf, L))
    (C, n, m), h = lax.scan(step, (C0, n0, m0), xs)
    return _from_chunks(h), C, n, m


def _mlstm_mixer(x, C0, n0, m0, w_in, i_bias, f_bias, w_out):
    B, T, _ = x.shape
    f32 = jnp.float32
    proj = x @ w_in
    q, k, v, o, ig, fg = jnp.split(
        proj, [ML_QK, 2 * ML_QK, 2 * ML_QK + ML_V, 2 * ML_QK + 2 * ML_V, 2 * ML_QK + 2 * ML_V + ML_HEADS], axis=-1)
    q = q.astype(f32).reshape(B, T, ML_HEADS, ML_DK) * ML_DK ** -0.5
    k = k.astype(f32).reshape(B, T, ML_HEADS, ML_DK)
    v = v.astype(f32).reshape(B, T, ML_HEADS, ML_DV)
    ig = ig.astype(f32) + i_bias.astype(f32)
    lf = jax.nn.log_sigmoid(fg.astype(f32) + f_bias.astype(f32))
    h, C, n, m = _mlstm_chunked(q, k, v, ig, lf, C0.astype(f32), n0.astype(f32), m0.astype(f32))
    h = h * jax.nn.sigmoid(o.astype(f32).reshape(B, T, ML_HEADS, ML_DV))
    y = h.reshape(B, T, ML_V).astype(x.dtype) @ w_out
    return y, C.astype(x.dtype), n.astype(x.dtype), m.astype(x.dtype)


def _mla_project(x, pos, w_in, q_norm, kv_norm, w_uq):
    B, T, _ = x.shape
    proj = x @ w_in
    cq, ckv, kr = jnp.split(proj, [MLA_Q_RANK, MLA_Q_RANK + MLA_KV_RANK], axis=-1)
    cq = _rmsnorm(cq, q_norm)
    ckv = _rmsnorm(ckv, kv_norm)
    q = (cq @ w_uq).reshape(B, T, MLA_HEADS, MLA_NOPE + MLA_ROPE)
    q_nope = q[..., :MLA_NOPE]
    q_rope = _rope(q[..., MLA_NOPE:], pos)
    k_rope = _rope(kr, pos)
    return q_nope, q_rope, ckv, k_rope


def _mla_prompt(x, w_in, q_norm, kv_norm, w_uq, w_ukv, w_out):
    B, T, _ = x.shape
    pos = jnp.arange(T)
    q_nope, q_rope, ckv, k_rope = _mla_project(x, pos, w_in, q_norm, kv_norm, w_uq)
    kv = (ckv @ w_ukv).reshape(B, T, MLA_HEADS, MLA_NOPE + MLA_V)
    k_nope, v = kv[..., :MLA_NOPE], kv[..., MLA_NOPE:]
    qb = math.gcd(T, Q_BLOCK)
    nb = T // qb
    qn_b = q_nope.reshape(B, nb, qb, MLA_HEADS, MLA_NOPE).swapaxes(0, 1)
    qr_b = q_rope.reshape(B, nb, qb, MLA_HEADS, MLA_ROPE).swapaxes(0, 1)
    starts = jnp.arange(nb) * qb
    key_pos = jnp.arange(T)

    def block(args):
        qn, qr, s0 = args
        sc = (jnp.einsum('bqhd,bkhd->bhqk', qn, k_nope)
              + jnp.einsum('bqhd,bkd->bhqk', qr, k_rope)).astype(jnp.float32) * MLA_SCALE
        qpos = s0 + jnp.arange(qb)
        mask = key_pos[None, :] <= qpos[:, None]
        p = jax.nn.softmax(jnp.where(mask, sc, -jnp.inf), axis=-1)
        return jnp.einsum('bhqk,bkhd->bqhd', p.astype(v.dtype), v)

    o = lax.map(block, (qn_b, qr_b, starts))
    o = o.swapaxes(0, 1).reshape(B, T, MLA_HEADS * MLA_V)
    return o @ w_out, ckv, k_rope


def _mla_sample(x, lat_pages, rope_pages, page_table, w_in, q_norm, kv_norm, w_uq, w_ukv, w_out):
    B, T, _ = x.shape
    past = page_table.shape[1] * PAGE_SIZE
    pos = past + jnp.arange(T)
    q_nope, q_rope, ckv, k_rope = _mla_project(x, pos, w_in, q_norm, kv_norm, w_uq)
    lat_past = lat_pages[page_table].reshape(B, past, MLA_KV_RANK)
    rope_past = rope_pages[page_table].reshape(B, past, MLA_ROPE)
    lat_all = jnp.concatenate([lat_past.astype(ckv.dtype), ckv], axis=1)
    rope_all = jnp.concatenate([rope_past.astype(k_rope.dtype), k_rope], axis=1)
    w = w_ukv.reshape(MLA_KV_RANK, MLA_HEADS, MLA_NOPE + MLA_V)
    w_uk, w_uv = w[..., :MLA_NOPE], w[..., MLA_NOPE:]
    q_lat = jnp.einsum('bqhn,rhn->bqhr', q_nope, w_uk)
    sc = (jnp.einsum('bqhr,bkr->bhqk', q_lat, lat_all)
          + jnp.einsum('bqhd,bkd->bhqk', q_rope, rope_all)).astype(jnp.float32) * MLA_SCALE
    mask = jnp.arange(past + T)[None, :] <= pos[:, None]
    p = jax.nn.softmax(jnp.where(mask, sc, -jnp.inf), axis=-1)
    o_lat = jnp.einsum('bhqk,bkr->bqhr', p.astype(lat_all.dtype), lat_all)
    o = jnp.einsum('bqhr,rhv->bqhv', o_lat, w_uv).reshape(B, T, MLA_HEADS * MLA_V)
    return o @ w_out, ckv, k_rope


def _trunk(x, sample, gdn_S, gdn_conv, mla_lat, mla_rope, ml_C, ml_n, ml_m, page_table,
           norm_gains, w_ffn_up, w_ffn_down,
           gdn_w_in, gdn_conv_w, gdn_a_log, gdn_dt_bias, gdn_norm_w, gdn_w_out,
           mla_w_in, mla_q_norm, mla_kv_norm, mla_w_uq, mla_w_ukv, mla_w_out,
           mlstm_w_in, mlstm_i_bias, mlstm_f_bias, mlstm_w_out):
    new = ([], [], [], [], [], [], [])
    for layer in range(DEPTH):
        kind = LAYER_MIXER[layer]
        j = LAYER_MIXER[:layer].count(kind)
        ng = norm_gains[layer]
        x = x + 0.5 * _rmsnorm(_swiglu(_rmsnorm(x, ng[0]), w_ffn_up[layer, 0], w_ffn_down[layer, 0]), ng[1])
        h = _rmsnorm(x, ng[2])
        if kind == 0:
            h, s_new, c_new = _gdn_mixer(h, gdn_S[j], gdn_conv[j], gdn_w_in[j], gdn_conv_w[j], gdn_a_log[j],
                                         gdn_dt_bias[j], gdn_norm_w[j], gdn_w_out[j])
            new[0].append(s_new)
            new[1].append(c_new)
        elif kind == 1:
            if sample:
                h, lat, kr = _mla_sample(h, mla_lat[j], mla_rope[j], page_table, mla_w_in[j], mla_q_norm[j],
                                         mla_kv_norm[j], mla_w_uq[j], mla_w_ukv[j], mla_w_out[j])
            else:
                h, lat, kr = _mla_prompt(h, mla_w_in[j], mla_q_norm[j], mla_kv_norm[j], mla_w_uq[j],
                                         mla_w_ukv[j], mla_w_out[j])
                lat = lat.reshape(-1, PAGE_SIZE, MLA_KV_RANK)
                kr = kr.reshape(-1, PAGE_SIZE, MLA_ROPE)
            new[2].append(lat)
            new[3].append(kr)
        else:
            h, C, n, m = _mlstm_mixer(h, ml_C[j], ml_n[j], ml_m[j], mlstm_w_in[j], mlstm_i_bias[j],
                                      mlstm_f_bias[j], mlstm_w_out[j])
            new[4].append(C)
            new[5].append(n)
            new[6].append(m)
        x = x + _rmsnorm(h, ng[3])
        x = x + 0.5 * _rmsnorm(_swiglu(_rmsnorm(x, ng[4]), w_ffn_up[layer, 1], w_ffn_down[layer, 1]), ng[5])
    return x, [jnp.stack(s) for s in new]


def setup_inputs(seed: int = 0) -> dict:
    key = jax.random.key(seed)
    ks = jax.random.split(key, 32)
    f32 = jnp.float32
    n_pages = PAST_LEN // PAGE_SIZE
    n_phys = (5 * DEC_BATCH * n_pages + 3) // 4

    def nrm(k, shape, s=1.0):
        return s * jax.random.normal(k, shape, f32)

    def gain(k, shape):
        return 1.0 + 0.05 * jax.random.normal(k, shape, f32)

    perm = jax.random.permutation(ks[5], n_phys)
    page_table = perm[:DEC_BATCH * n_pages].reshape(DEC_BATCH, n_pages).astype(jnp.int32)
    dt = jnp.exp(jax.random.uniform(ks[13], (N_GDN, GDN_HEADS), f32, math.log(1e-3), math.log(1e-1)))
    dt_bias = dt + jnp.log(-jnp.expm1(-dt))
    a_log = jnp.log(jax.random.uniform(ks[16], (N_GDN, GDN_HEADS), f32, 1.0, 16.0))
    f_bias = jnp.linspace(3.0, 6.0, ML_HEADS, dtype=f32)[None, :] + nrm(ks[27], (N_MLSTM, ML_HEADS), 0.1)
    return {
        'x_prompt': nrm(ks[0], (BATCH, SEQ, D_MODEL)),
        'x_sample': nrm(ks[1], (DEC_BATCH, DEC_SEQ, D_MODEL)),
        'state_gdn_S': nrm(ks[2], (N_GDN, DEC_BATCH, GDN_HEADS, GDN_DK, GDN_DV), 0.5),
        'state_gdn_conv': nrm(ks[3], (N_GDN, DEC_BATCH, GDN_CONV - 1, GDN_QKV)),
        'cache_mla_latent': nrm(ks[4], (N_MLA, n_phys, PAGE_SIZE, MLA_KV_RANK)),
        'cache_mla_rope': nrm(ks[6], (N_MLA, n_phys, PAGE_SIZE, MLA_ROPE)),
        'state_mlstm_C': nrm(ks[7], (N_MLSTM, DEC_BATCH, ML_HEADS, ML_DK, ML_DV), 0.5),
        'state_mlstm_n': nrm(ks[8], (N_MLSTM, DEC_BATCH, ML_HEADS, ML_DK), 0.5),
        'state_mlstm_m': nrm(ks[9], (N_MLSTM, DEC_BATCH, ML_HEADS)),
        'page_table': page_table,
        'norm_gains': gain(ks[10], (DEPTH, 6, D_MODEL)),
        'w_ffn_up': nrm(ks[11], (DEPTH, 2, D_MODEL, 2 * D_FF), D_MODEL ** -0.5),
        'w_ffn_down': nrm(ks[12], (DEPTH, 2, D_FF, D_MODEL), D_FF ** -0.5),
        'gdn_w_in': nrm(ks[14], (N_GDN, D_MODEL, GDN_QKV + GDN_Z + 2 * GDN_HEADS), D_MODEL ** -0.5),
        'gdn_conv_w': nrm(ks[15], (N_GDN, GDN_CONV, GDN_QKV), GDN_CONV ** -0.5),
        'gdn_a_log': a_log,
        'gdn_dt_bias': dt_bias,
        'gdn_norm_w': gain(ks[17], (N_GDN, GDN_DV)),
        'gdn_w_out': nrm(ks[18], (N_GDN, GDN_Z, D_MODEL), GDN_Z ** -0.5),
        'mla_w_in': nrm(ks[19], (N_MLA, D_MODEL, MLA_Q_RANK + MLA_KV_RANK + MLA_ROPE), D_MODEL ** -0.5),
        'mla_q_norm': gain(ks[20], (N_MLA, MLA_Q_RANK)),
        'mla_kv_norm': gain(ks[21], (N_MLA, MLA_KV_RANK)),
        'mla_w_uq': nrm(ks[22], (N_MLA, MLA_Q_RANK, MLA_HEADS * (MLA_NOPE + MLA_ROPE)), MLA_Q_RANK ** -0.5),
        'mla_w_ukv': nrm(ks[23], (N_MLA, MLA_KV_RANK, MLA_HEADS * (MLA_NOPE + MLA_V)), MLA_KV_RANK ** -0.5),
        'mla_w_out': nrm(ks[24], (N_MLA, MLA_HEADS * MLA_V, D_MODEL), (MLA_HEADS * MLA_V) ** -0.5),
        'mlstm_w_in': nrm(ks[25], (N_MLSTM, D_MODEL, 2 * ML_QK + 2 * ML_V + 2 * ML_HEADS), D_MODEL ** -0.5),
        'mlstm_i_bias': nrm(ks[26], (N_MLSTM, ML_HEADS), 0.1),
        'mlstm_f_bias': f_bias,
        'mlstm_w_out': nrm(ks[28], (N_MLSTM, ML_V, D_MODEL), ML_V ** -0.5),
    }


def reference(x_prompt, x_sample, state_gdn_S, state_gdn_conv, cache_mla_latent, cache_mla_rope,
              state_mlstm_C, state_mlstm_n, state_mlstm_m, page_table,
              norm_gains, w_ffn_up, w_ffn_down,
              gdn_w_in, gdn_conv_w, gdn_a_log, gdn_dt_bias, gdn_norm_w, gdn_w_out,
              mla_w_in, mla_q_norm, mla_kv_norm, mla_w_uq, mla_w_ukv, mla_w_out,
              mlstm_w_in, mlstm_i_bias, mlstm_f_bias, mlstm_w_out):
    B = x_prompt.shape[0]
    dt = x_prompt.dtype
    z_S = jnp.zeros((N_GDN, B, GDN_HEADS, GDN_DK, GDN_DV), dt)
    z_conv = jnp.zeros((N_GDN, B, GDN_CONV - 1, GDN_QKV), dt)
    z_C = jnp.zeros((N_MLSTM, B, ML_HEADS, ML_DK, ML_DV), dt)
    z_n = jnp.zeros((N_MLSTM, B, ML_HEADS, ML_DK), dt)
    z_m = jnp.zeros((N_MLSTM, B, ML_HEADS), dt)
    y_prompt, st_p = _trunk(
        x_prompt, False, z_S, z_conv, None, None, z_C, z_n, z_m, None,
        norm_gains, w_ffn_up, w_ffn_down,
        gdn_w_in, gdn_conv_w, gdn_a_log, gdn_dt_bias, gdn_norm_w, gdn_w_out,
        mla_w_in, mla_q_norm, mla_kv_norm, mla_w_uq, mla_w_ukv, mla_w_out,
        mlstm_w_in, mlstm_i_bias, mlstm_f_bias, mlstm_w_out)
    y_sample, st_s = _trunk(
        x_sample, True, state_gdn_S, state_gdn_conv, cache_mla_latent, cache_mla_rope,
        state_mlstm_C, state_mlstm_n, state_mlstm_m, page_table,
        norm_gains, w_ffn_up, w_ffn_down,
        gdn_w_in, gdn_conv_w, gdn_a_log, gdn_dt_bias, gdn_norm_w, gdn_w_out,
        mla_w_in, mla_q_norm, mla_kv_norm, mla_w_uq, mla_w_ukv, mla_w_out,
        mlstm_w_in, mlstm_i_bias, mlstm_f_bias, mlstm_w_out)
    gdn_S_p, gdn_conv_p, mla_lat_p, mla_rope_p, ml_C_p, ml_n_p, ml_m_p = st_p
    gdn_S_s, gdn_conv_s, mla_lat_s, mla_rope_s, ml_C_s, ml_n_s, ml_m_s = st_s
    return (y_prompt, y_sample, gdn_S_p, gdn_S_s, gdn_conv_p, gdn_conv_s, mla_lat_p, mla_lat_s,
            mla_rope_p, mla_rope_s, ml_C_p, ml_C_s, ml_n_p, ml_n_s, ml_m_p, ml_m_s)
```

```python
import functools
import math

import jax
import jax.numpy as jnp
from jax import lax
from jax.experimental import pallas as pl
from jax.experimental.pallas import tpu as pltpu

F32 = jnp.float32
BF16 = jnp.bfloat16
HIGHEST = lax.Precision.HIGHEST

D_MODEL = 1024
PAGE_SIZE = 128
EPS = 1e-6

GDN_HEADS = 8
GDN_DK = 128
GDN_DV = 128
GDN_CONV = 4
GDN_CHUNK = 64
GDN_QKV = GDN_HEADS * (2 * GDN_DK + GDN_DV)
GDN_Z = GDN_HEADS * GDN_DV

MLA_HEADS = 8
MLA_NOPE = 128
MLA_ROPE = 64
MLA_V = 128
MLA_Q_RANK = 384
MLA_KV_RANK = 256
MLA_SCALE = (MLA_NOPE + MLA_ROPE) ** -0.5
ROPE_THETA = 10000.0

ML_HEADS = 4
ML_DK = 128
ML_DV = 256
ML_CHUNK = 64
ML_QK = ML_HEADS * ML_DK
ML_V = ML_HEADS * ML_DV

D_FF = 2816
FF_CHUNK = 256
N_FF_CHUNKS = D_FF // FF_CHUNK

LANES = 128
SUBLANES = 8
DEC_PAD = SUBLANES
VMEM_LIMIT = 56 * 1024 * 1024
PAGES_PER_STEP = 16


def _rms(x, g):
    return x * lax.rsqrt(jnp.mean(x * x, axis=-1, keepdims=True) + EPS) * g


def _silu(x):
    return x * jax.nn.sigmoid(x)


def _softplus(x):
    return jnp.maximum(x, 0.0) + jnp.log1p(jnp.exp(-jnp.abs(x)))


def _dot(a, b):
    return jnp.dot(a.astype(BF16), b.astype(BF16), preferred_element_type=F32)


def _dot_nt(a, b):
    return lax.dot_general(a.astype(BF16), b.astype(BF16), (((1,), (1,)), ((), ())),
                           preferred_element_type=F32)


def _dot_tn(a, b):
    return lax.dot_general(a.astype(BF16), b.astype(BF16), (((0,), (0,)), ((), ())),
                           preferred_element_type=F32)


def _dot_f32(a, b):
    return jnp.dot(a, b, precision=HIGHEST, preferred_element_type=F32)


def _unit_lower_inverse(a_strict, eye):
    size = a_strict.shape[0]
    m = -a_strict
    p = eye + m
    for _ in range(int(math.log2(size)) - 1):
        m = _dot_f32(m, m)
        p = p + _dot_f32(p, m)
    return p


def _compiler_params(n_axes):
    return pltpu.CompilerParams(dimension_semantics=("arbitrary",) * n_axes,
                                vmem_limit_bytes=VMEM_LIMIT)


def _const_spec(shape):
    nd = len(shape)
    return pl.BlockSpec(shape, lambda *_: (0,) * nd)


def _ffn_kernel(x_ref, gpre_ref, gpost_ref, wup_ref, wdn_ref, o_ref):
    x = x_ref[...]
    xn = _rms(x, gpre_ref[...]).astype(BF16)
    acc = jnp.zeros(x.shape, F32)
    for c in range(N_FF_CHUNKS):
        gu = jnp.dot(xn, wup_ref[c], preferred_element_type=F32)
        act = (_silu(gu[:, :FF_CHUNK]) * gu[:, FF_CHUNK:]).astype(BF16)
        acc = acc + jnp.dot(act, wdn_ref[c], preferred_element_type=F32)
    o_ref[...] = x + 0.5 * _rms(acc, gpost_ref[...])


def _ffn_half(x, gpre, gpost, wup, wdn, tm):
    m = x.shape[0]
    return pl.pallas_call(
        _ffn_kernel,
        grid=(m // tm,),
        in_specs=[pl.BlockSpec((tm, D_MODEL), lambda i: (i, 0)),
                  _const_spec((1, D_MODEL)), _const_spec((1, D_MODEL)),
                  _const_spec(wup.shape), _const_spec(wdn.shape)],
        out_specs=pl.BlockSpec((tm, D_MODEL), lambda i: (i, 0)),
        out_shape=jax.ShapeDtypeStruct((m, D_MODEL), F32),
        compiler_params=_compiler_params(1),
        name="ffn_half",
    )(x, gpre, gpost, wup, wdn)


def _norm_matmul_kernel(x_ref, g_ref, w_ref, *o_refs, splits):
    xn = _rms(x_ref[...], g_ref[...]).astype(BF16)
    off = 0
    for o_ref, n in zip(o_refs, splits):
        o_ref[...] = jnp.dot(xn, w_ref[:, off:off + n], preferred_element_type=F32)
        off += n


def _norm_matmul(x, g, w, splits, tm):
    m = x.shape[0]
    return pl.pallas_call(
        functools.partial(_norm_matmul_kernel, splits=splits),
        grid=(m // tm,),
        in_specs=[pl.BlockSpec((tm, D_MODEL), lambda i: (i, 0)),
                  _const_spec((1, D_MODEL)), _const_spec(w.shape)],
        out_specs=[pl.BlockSpec((tm, n), lambda i: (i, 0)) for n in splits],
        out_shape=[jax.ShapeDtypeStruct((m, n), F32) for n in splits],
        compiler_params=_compiler_params(1),
        name="norm_matmul",
    )(x, g, w)


def _out_proj_kernel(a_ref, x_ref, g_ref, w_ref, o_ref):
    y = jnp.dot(a_ref[...], w_ref[...], preferred_element_type=F32)
    o_ref[...] = x_ref[...] + _rms(y, g_ref[...])


def _out_proj(a, x, g, w, tm):
    m = x.shape[0]
    return pl.pallas_call(
        _out_proj_kernel,
        grid=(m // tm,),
        in_specs=[pl.BlockSpec((tm, D_MODEL), lambda i: (i, 0)),
                  pl.BlockSpec((tm, D_MODEL), lambda i: (i, 0)),
                  _const_spec((1, D_MODEL)), _const_spec(w.shape)],
        out_specs=pl.BlockSpec((tm, D_MODEL), lambda i: (i, 0)),
        out_shape=jax.ShapeDtypeStruct((m, D_MODEL), F32),
        compiler_params=_compiler_params(1),
        name="out_proj",
    )(a, x, g, w)


def _gdn_seq_kernel(qkv_ref, z_ref, ba_ref, conv0_ref, s0_ref, convw_ref, alog_ref, dtb_ref, normw_ref,
                    o_ref, sout_ref, convout_ref,
                    xext, q_s, k_s, v_s, g_s, beta_s, s_s, *, tc, chunk, n_valid):
    t = pl.program_id(1)
    heads = GDN_HEADS
    last = min(chunk, n_valid) - 1

    @pl.when(t == 0)
    def _():
        xext[0:SUBLANES, :] = conv0_ref[...]
        s_s[...] = s0_ref[...]

    xext[SUBLANES:SUBLANES + tc, :] = qkv_ref[...]
    for part, dest in enumerate((q_s, k_s, v_s)):
        for h in range(heads):
            c0 = part * heads * GDN_DK + h * GDN_DK
            cols = slice(c0, c0 + GDN_DK)
            y = xext[5:5 + tc, cols] * convw_ref[0:1, cols]
            for j in range(1, GDN_CONV):
                y = y + xext[5 + j:5 + j + tc, cols] * convw_ref[j:j + 1, cols]
            y = _silu(y)
            if part < 2:
                y = y * lax.rsqrt(jnp.sum(y * y, axis=-1, keepdims=True) + EPS)
            if part == 0:
                y = y * GDN_DK ** -0.5
            dest[h] = y

    @pl.when(t == pl.num_programs(1) - 1)
    def _():
        convout_ref[...] = xext[n_valid:n_valid + SUBLANES, :]

    xext[0:SUBLANES, :] = xext[tc:tc + SUBLANES, :]

    ba = ba_ref[...]
    beta = jax.nn.sigmoid(ba[:, :LANES])
    if n_valid < tc:
        beta = jnp.where(lax.broadcasted_iota(jnp.int32, beta.shape, 0) < n_valid, beta, 0.0)
    beta_s[...] = beta
    g_s[...] = -jnp.exp(alog_ref[...]) * _softplus(ba[:, LANES:] + dtb_ref[...])

    ri = lax.broadcasted_iota(jnp.int32, (chunk, chunk), 0)
    ci = lax.broadcasted_iota(jnp.int32, (chunk, chunk), 1)
    incl = ci <= ri
    strict = ci < ri
    tril = incl.astype(F32)
    eye = (ci == ri).astype(F32)
    normw = normw_ref[...]

    def chunk_body(c, carry):
        r0 = pl.multiple_of(c * chunk, chunk)
        rows = pl.ds(r0, chunk)
        gc = _dot_f32(tril, g_s[rows, :])
        gc_t = gc.T
        beta_c = beta_s[rows, :]
        exp_gc = jnp.exp(gc)
        gc_last = gc[last:last + 1, :]
        exp_rest = jnp.exp(gc_last - gc)
        exp_last = jnp.exp(gc_last)
        for h in range(heads):
            q = q_s[h, rows, :]
            k = k_s[h, rows, :]
            v = v_s[h, rows, :]
            diff = gc[:, h:h + 1] - gc_t[h:h + 1, :]
            decay_incl = jnp.exp(jnp.where(incl, diff, -jnp.inf))
            decay_strict = jnp.where(strict, decay_incl, 0.0)
            b = beta_c[:, h:h + 1]
            kb = k * b
            a_mat = _dot_nt(kb, k) * decay_strict
            t_mat = _unit_lower_inverse(a_mat, eye)
            rhs = jnp.concatenate([v * b, kb * exp_gc[:, h:h + 1]], axis=1)
            sol = _dot_f32(t_mat, rhs)
            u, w = sol[:, :GDN_DV], sol[:, GDN_DV:]
            attn = _dot_nt(q, k) * decay_incl
            qg = q * exp_gc[:, h:h + 1]
            kdec = k * exp_rest[:, h:h + 1]
            s_h = s_s[h]
            v_new = u - _dot(w, s_h)
            o = _dot(qg, s_h) + _dot(attn, v_new)
            s_s[h] = s_h * exp_last[:, h:h + 1] + _dot_tn(kdec, v_new)
            cols = slice(h * GDN_DV, (h + 1) * GDN_DV)
            o_ref[rows, cols] = (_rms(o, normw) * _silu(z_ref[rows, cols])).astype(o_ref.dtype)
        return carry

    lax.fori_loop(0, tc // chunk, chunk_body, 0)

    @pl.when(t == pl.num_programs(1) - 1)
    def _():
        sout_ref[...] = s_s[...]


def _gdn_seq(qkv, z, ba, conv0, s0, convw, alog, dtb, normw, *, nseq, seqlen, tc, chunk, n_valid):
    nt = seqlen // tc
    row = lambda b, t: (b * nt + t, 0)
    seq3 = lambda b, t: (b, 0, 0)
    seq4 = lambda b, t: (b, 0, 0, 0)
    m = nseq * seqlen
    return pl.pallas_call(
        functools.partial(_gdn_seq_kernel, tc=tc, chunk=chunk, n_valid=n_valid),
        grid=(nseq, nt),
        in_specs=[pl.BlockSpec((tc, GDN_QKV), row), pl.BlockSpec((tc, GDN_Z), row),
                  pl.BlockSpec((tc, 2 * LANES), row),
                  pl.BlockSpec((None, SUBLANES, GDN_QKV), seq3),
                  pl.BlockSpec((None, GDN_HEADS, GDN_DK, GDN_DV), seq4),
                  _const_spec((SUBLANES, GDN_QKV)), _const_spec((1, LANES)), _const_spec((1, LANES)),
                  _const_spec((1, GDN_DV))],
        out_specs=[pl.BlockSpec((tc, GDN_Z), row),
                   pl.BlockSpec((None, GDN_HEADS, GDN_DK, GDN_DV), seq4),
                   pl.BlockSpec((None, SUBLANES, GDN_QKV), seq3)],
        out_shape=[jax.ShapeDtypeStruct((m, GDN_Z), BF16),
                   jax.ShapeDtypeStruct((nseq, GDN_HEADS, GDN_DK, GDN_DV), F32),
                   jax.ShapeDtypeStruct((nseq, SUBLANES, GDN_QKV), F32)],
        scratch_shapes=[pltpu.VMEM((tc + SUBLANES, GDN_QKV), F32),
                        pltpu.VMEM((GDN_HEADS, tc, GDN_DK), F32),
                        pltpu.VMEM((GDN_HEADS, tc, GDN_DK), F32),
                        pltpu.VMEM((GDN_HEADS, tc, GDN_DV), F32),
                        pltpu.VMEM((tc, LANES), F32),
                        pltpu.VMEM((tc, LANES), F32),
                        pltpu.VMEM((GDN_HEADS, GDN_DK, GDN_DV), F32)],
        compiler_params=_compiler_params(2),
        name="gdn_seq",
    )(qkv, z, ba, conv0, s0, convw, alog, dtb, normw)


def _mlstm_seq_kernel(q_ref, k_ref, v_ref, og_ref, gates_ref, c0_ref, n0_ref, m0_ref, ibias_ref, fbias_ref,
                      h_ref, cout_ref, nout_ref, mout_ref,
                      cx_s, m_s, ig_s, lf_s, *, tc, chunk, n_valid):
    t = pl.program_id(1)
    heads = ML_HEADS
    last = min(chunk, n_valid) - 1

    @pl.when(t == 0)
    def _():
        cx_s[:, :, :ML_DV] = c0_ref[...]
        cx_s[:, :, ML_DV:] = n0_ref[...]
        m_s[...] = m0_ref[...]

    gates = gates_ref[...]
    ig_s[...] = gates[:, :LANES] + ibias_ref[...]
    lf_s[...] = -_softplus(-(gates[:, LANES:] + fbias_ref[...]))

    ri = lax.broadcasted_iota(jnp.int32, (chunk, chunk), 0)
    ci = lax.broadcasted_iota(jnp.int32, (chunk, chunk), 1)
    mask = (ci <= ri) & (ci < n_valid)
    tril = (ci <= ri).astype(F32)
    col_rows = lax.broadcasted_iota(jnp.int32, (chunk, 1), 0)
    ones_col = (lax.broadcasted_iota(jnp.int32, (chunk, LANES), 1) == 0).astype(F32)
    lane_row = lax.broadcasted_iota(jnp.int32, (1, LANES), 1)

    def chunk_body(c, carry):
        r0 = pl.multiple_of(c * chunk, chunk)
        rows = pl.ds(r0, chunk)
        ig = ig_s[rows, :]
        bcum = _dot_f32(tril, lf_s[rows, :])
        bcum_t = bcum.T
        ig_t = ig.T
        m_row = m_s[0:1, :]
        inter = bcum + m_row
        m_next = m_row
        for h in range(heads):
            bcol = bcum[:, h:h + 1]
            dmat = jnp.where(mask, bcol - bcum_t[h:h + 1, :] + ig_t[h:h + 1, :], -jnp.inf)
            inter_h = inter[:, h:h + 1]
            m_t = jnp.maximum(inter_h, jnp.max(dmat, axis=-1, keepdims=True))
            w_intra = jnp.exp(dmat - m_t)
            w_inter = jnp.exp(inter_h - m_t)
            q = q_ref[rows, h * ML_DK:(h + 1) * ML_DK] * ML_DK ** -0.5
            k = k_ref[rows, h * ML_DK:(h + 1) * ML_DK]
            vcols = slice(h * ML_DV, (h + 1) * ML_DV)
            vx = jnp.concatenate([v_ref[rows, vcols], ones_col], axis=1)
            s = _dot_nt(q, k) * w_intra
            cx = cx_s[h]
            num = w_inter * _dot(q, cx) + _dot(s, vx)
            den = num[:, ML_DV:ML_DV + 1]
            hh = num[:, :ML_DV] / jnp.maximum(jnp.abs(den), jnp.exp(-m_t))
            h_ref[rows, vcols] = (hh * jax.nn.sigmoid(og_ref[rows, vcols])).astype(h_ref.dtype)
            m_new = m_t[last:last + 1, :]
            w_state = jnp.exp(jnp.where(col_rows <= last,
                                        bcum[last:last + 1, h:h + 1] - bcol + ig[:, h:h + 1], -jnp.inf) - m_new)
            carry_decay = jnp.exp(inter_h[last:last + 1, :] - m_new)
            cx_s[h] = carry_decay * cx + _dot_tn(k * w_state, vx)
            m_next = jnp.where(lane_row == h, m_new, m_next)
        m_s[0:1, :] = m_next
        return carry

    lax.fori_loop(0, tc // chunk, chunk_body, 0)

    @pl.when(t == pl.num_programs(1) - 1)
    def _():
        cout_ref[...] = cx_s[:, :, :ML_DV]
        nout_ref[...] = cx_s[:, :, ML_DV:]
        mout_ref[...] = m_s[...]


def _mlstm_seq(q, k, v, og, gates, c0, n0, m0, ibias, fbias, *, nseq, seqlen, tc, chunk, n_valid):
    nt = seqlen // tc
    row = lambda b, t: (b * nt + t, 0)
    seq3 = lambda b, t: (b, 0, 0)
    seq4 = lambda b, t: (b, 0, 0, 0)
    m = nseq * seqlen
    return pl.pallas_call(
        functools.partial(_mlstm_seq_kernel, tc=tc, chunk=chunk, n_valid=n_valid),
        grid=(nseq, nt),
        in_specs=[pl.BlockSpec((tc, ML_QK), row), pl.BlockSpec((tc, ML_QK), row),
                  pl.BlockSpec((tc, ML_V), row), pl.BlockSpec((tc, ML_V), row),
                  pl.BlockSpec((tc, 2 * LANES), row),
                  pl.BlockSpec((None, ML_HEADS, ML_DK, ML_DV), seq4),
                  pl.BlockSpec((None, ML_HEADS, ML_DK, LANES), seq4),
                  pl.BlockSpec((None, SUBLANES, LANES), seq3),
                  _const_spec((1, LANES)), _const_spec((1, LANES))],
        out_specs=[pl.BlockSpec((tc, ML_V), row),
                   pl.BlockSpec((None, ML_HEADS, ML_DK, ML_DV), seq4),
                   pl.BlockSpec((None, ML_HEADS, ML_DK, LANES), seq4),
                   pl.BlockSpec((None, SUBLANES, LANES), seq3)],
        out_shape=[jax.ShapeDtypeStruct((m, ML_V), BF16),
                   jax.ShapeDtypeStruct((nseq, ML_HEADS, ML_DK, ML_DV), F32),
                   jax.ShapeDtypeStruct((nseq, ML_HEADS, ML_DK, LANES), F32),
                   jax.ShapeDtypeStruct((nseq, SUBLANES, LANES), F32)],
        scratch_shapes=[pltpu.VMEM((ML_HEADS, ML_DK, ML_DV + LANES), F32),
                        pltpu.VMEM((SUBLANES, LANES), F32),
                        pltpu.VMEM((tc, LANES), F32),
                        pltpu.VMEM((tc, LANES), F32)],
        compiler_params=_compiler_params(2),
        name="mlstm_seq",
    )(q, k, v, og, gates, c0, n0, m0, ibias, fbias)


def _mla_proj_kernel(x_ref, g_ref, cos_ref, sin_ref, win_ref, qnorm_ref, kvnorm_ref, wuq_ref, wuqsw_ref, wkv_ref,
                     *o_refs, decode):
    xn = _rms(x_ref[...], g_ref[...]).astype(BF16)
    proj = jnp.dot(xn, win_ref[...], preferred_element_type=F32)
    cq = _rms(proj[:, :MLA_Q_RANK], qnorm_ref[...]).astype(BF16)
    ckv = _rms(proj[:, MLA_Q_RANK:MLA_Q_RANK + MLA_KV_RANK], kvnorm_ref[...])
    base = MLA_Q_RANK + MLA_KV_RANK
    cos = cos_ref[...]
    sin = sin_ref[...]
    kr = proj[:, base:base + LANES] * cos + proj[:, base + LANES:base + 2 * LANES] * sin
    q_main = jnp.dot(cq, wuq_ref[...], preferred_element_type=F32)
    q_swap = jnp.dot(cq, wuqsw_ref[...], preferred_element_type=F32)
    if decode:
        qlat_ref, qrope_ref, ckv_ref, kr_ref = o_refs
    else:
        qcat_ref, kcat_ref, v_ref, ckv_ref, kr_ref = o_refs
        kv = jnp.dot(ckv.astype(BF16), wkv_ref[...], preferred_element_type=F32)
    ckv_ref[...] = ckv
    kr_ref[...] = kr[:, :MLA_ROPE]
    for h in range(MLA_HEADS):
        lo = h * 2 * LANES
        q_nope = q_main[:, lo:lo + LANES] * MLA_SCALE
        q_rope = (q_main[:, lo + LANES:lo + 2 * LANES] * cos + q_swap[:, h * LANES:(h + 1) * LANES] * sin) * MLA_SCALE
        if decode:
            qlat_ref[:, lo:lo + 2 * LANES] = _dot(q_nope, wkv_ref[h]).astype(BF16)
            qrope_ref[:, h * LANES:(h + 1) * LANES] = q_rope.astype(BF16)
        else:
            qcat_ref[:, lo:lo + LANES] = q_nope.astype(BF16)
            qcat_ref[:, lo + LANES:lo + 2 * LANES] = q_rope.astype(BF16)
            kcat_ref[:, lo:lo + LANES] = kv[:, lo:lo + LANES].astype(BF16)
            kcat_ref[:, lo + LANES:lo + 2 * LANES] = kr.astype(BF16)
            v_ref[:, h * MLA_V:(h + 1) * MLA_V] = kv[:, lo + LANES:lo + 2 * LANES].astype(BF16)


def _mla_proj(x, g, cos, sin, win, qnorm, kvnorm, wuq, wuqsw, wkv, *, tm, decode):
    m = x.shape[0]
    n_pos_blocks = cos.shape[0] // tm
    row = lambda i: (i, 0)
    pos = lambda i: (i % n_pos_blocks, 0)
    wide = MLA_HEADS * 2 * LANES
    if decode:
        widths = (wide, MLA_HEADS * LANES, MLA_KV_RANK, MLA_ROPE)
        dtypes = (BF16, BF16, F32, F32)
    else:
        widths = (wide, wide, MLA_HEADS * MLA_V, MLA_KV_RANK, MLA_ROPE)
        dtypes = (BF16, BF16, BF16, F32, F32)
    return pl.pallas_call(
        functools.partial(_mla_proj_kernel, decode=decode),
        grid=(m // tm,),
        in_specs=[pl.BlockSpec((tm, D_MODEL), row), _const_spec((1, D_MODEL)),
                  pl.BlockSpec((tm, LANES), pos), pl.BlockSpec((tm, LANES), pos),
                  _const_spec(win.shape), _const_spec((1, MLA_Q_RANK)), _const_spec((1, MLA_KV_RANK)),
                  _const_spec(wuq.shape), _const_spec(wuqsw.shape), _const_spec(wkv.shape)],
        out_specs=[pl.BlockSpec((tm, n), row) for n in widths],
        out_shape=[jax.ShapeDtypeStruct((m, n), dt) for n, dt in zip(widths, dtypes)],
        compiler_params=_compiler_params(1),
        name="mla_proj_decode" if decode else "mla_proj",
    )(x, g, cos, sin, win, qnorm, kvnorm, wuq, wuqsw, wkv)


def _mla_attn_kernel(q_ref, k_ref, v_ref, o_ref, *, tq):
    qi = pl.program_id(2)
    q = q_ref[...]
    ri = lax.broadcasted_iota(jnp.int32, (tq, tq), 0)
    ci = lax.broadcasted_iota(jnp.int32, (tq, tq), 1)

    def step(j, carry):
        m, l, acc = carry
        rows = pl.ds(pl.multiple_of(j * tq, tq), tq)
        s = lax.dot_general(q, k_ref[rows, :], (((1,), (1,)), ((), ())), preferred_element_type=F32)
        s = jnp.where((j * tq + ci) <= (qi * tq + ri), s, -jnp.inf)
        m_new = jnp.maximum(m, jnp.max(s, axis=-1, keepdims=True))
        p = jnp.exp(s - m_new)
        alpha = jnp.exp(m - m_new)
        l = alpha * l + jnp.sum(p, axis=-1, keepdims=True)
        acc = alpha * acc + jnp.dot(p.astype(BF16), v_ref[rows, :], preferred_element_type=F32)
        return m_new, l, acc

    init = (jnp.full((tq, 1), -jnp.inf, F32), jnp.zeros((tq, 1), F32), jnp.zeros((tq, MLA_V), F32))
    _, l, acc = lax.fori_loop(0, qi + 1, step, init)
    o_ref[...] = (acc / l).astype(o_ref.dtype)


def _mla_attn(qcat, kcat, v, *, nseq, seqlen, tq):
    nq = seqlen // tq
    m = nseq * seqlen
    return pl.pallas_call(
        functools.partial(_mla_attn_kernel, tq=tq),
        grid=(nseq, MLA_HEADS, nq),
        in_specs=[pl.BlockSpec((tq, 2 * LANES), lambda b, h, i: (b * nq + i, h)),
                  pl.BlockSpec((seqlen, 2 * LANES), lambda b, h, i: (b, h)),
                  pl.BlockSpec((seqlen, MLA_V), lambda b, h, i: (b, h))],
        out_specs=pl.BlockSpec((tq, MLA_V), lambda b, h, i: (b * nq + i, h)),
        out_shape=jax.ShapeDtypeStruct((m, MLA_HEADS * MLA_V), BF16),
        compiler_params=_compiler_params(3),
        name="mla_attn",
    )(qcat, kcat, v)


def _mla_decode_kernel(pt_ref, qlat_ref, qrope_ref, ckv_ref, kr_ref, wuv_ref, *rest, n_pages):
    lat_refs = rest[:n_pages]
    rope_refs = rest[n_pages:2 * n_pages]
    o_ref = rest[2 * n_pages]
    ql_s, qr_s, m_s, l_s, acc_s = rest[2 * n_pages + 1:]
    g = pl.program_id(1)
    heads = MLA_HEADS
    tok = DEC_PAD

    @pl.when(g == 0)
    def _():
        for h in range(heads):
            ql_s[h * tok:(h + 1) * tok, :] = qlat_ref[:, h * 2 * LANES:(h + 1) * 2 * LANES]
            qr_s[h * tok:(h + 1) * tok, :] = qrope_ref[:, h * LANES:(h + 1) * LANES]
        m_s[...] = jnp.full(m_s.shape, -jnp.inf, F32)
        l_s[...] = jnp.zeros(l_s.shape, F32)
        acc_s[...] = jnp.zeros(acc_s.shape, F32)

    ql = ql_s[...]
    qr = qr_s[:, :MLA_ROPE]

    def update(s, values):
        m_old = m_s[...]
        m_new = jnp.maximum(m_old, jnp.max(s, axis=-1, keepdims=True))
        p = jnp.exp(s - m_new)
        alpha = jnp.exp(m_old - m_new)
        l_s[...] = alpha * l_s[...] + jnp.sum(p, axis=-1, keepdims=True)
        pb = p.astype(BF16)
        acc = alpha * acc_s[...]
        for cols, val in values:
            acc = acc + jnp.dot(pb[:, cols], val, preferred_element_type=F32)
        acc_s[...] = acc
        m_s[...] = m_new

    lats = [r[...].astype(BF16) for r in lat_refs]
    scores = [_dot_nt(ql, lat) + _dot_nt(qr, r[...]) for lat, r in zip(lats, rope_refs)]
    update(jnp.concatenate(scores, axis=1),
           [(slice(i * PAGE_SIZE, (i + 1) * PAGE_SIZE), lat) for i, lat in enumerate(lats)])

    @pl.when(g == pl.num_programs(1) - 1)
    def _():
        ckv = ckv_ref[...].astype(BF16)
        s = _dot_nt(ql, ckv) + _dot_nt(qr, kr_ref[...])
        qt = lax.broadcasted_iota(jnp.int32, s.shape, 0) % tok
        kt = lax.broadcasted_iota(jnp.int32, s.shape, 1)
        update(jnp.where(kt <= qt, s, -jnp.inf), [(slice(0, tok), ckv)])
        o_lat = acc_s[...] / l_s[...]
        for h in range(heads):
            o_ref[:, h * MLA_V:(h + 1) * MLA_V] = _dot(o_lat[h * tok:(h + 1) * tok, :], wuv_ref[h]).astype(o_ref.dtype)


def _mla_decode(page_table, qlat, qrope, ckv, kr, wuv, lat_pages, rope_pages, *, nseq):
    n_pages_seq = page_table.shape[1]
    n = PAGES_PER_STEP
    groups = n_pages_seq // n
    seq = lambda b, g, pt: (b, 0)

    def page_map(i):
        return lambda b, g, pt: (pt[b * n_pages_seq + g * n + i], 0, 0)

    rows = MLA_HEADS * DEC_PAD
    grid_spec = pltpu.PrefetchScalarGridSpec(
        num_scalar_prefetch=1,
        grid=(nseq, groups),
        in_specs=[pl.BlockSpec((DEC_PAD, MLA_HEADS * 2 * LANES), seq),
                  pl.BlockSpec((DEC_PAD, MLA_HEADS * LANES), seq),
                  pl.BlockSpec((DEC_PAD, MLA_KV_RANK), seq),
                  pl.BlockSpec((DEC_PAD, MLA_ROPE), seq),
                  pl.BlockSpec(wuv.shape, lambda b, g, pt: (0, 0, 0))]
                 + [pl.BlockSpec((None, PAGE_SIZE, MLA_KV_RANK), page_map(i)) for i in range(n)]
                 + [pl.BlockSpec((None, PAGE_SIZE, MLA_ROPE), page_map(i)) for i in range(n)],
        out_specs=pl.BlockSpec((DEC_PAD, MLA_HEADS * MLA_V), seq),
        scratch_shapes=[pltpu.VMEM((rows, MLA_KV_RANK), BF16), pltpu.VMEM((rows, LANES), BF16),
                        pltpu.VMEM((rows, 1), F32), pltpu.VMEM((rows, 1), F32),
                        pltpu.VMEM((rows, MLA_KV_RANK), F32)],
    )
    return pl.pallas_call(
        functools.partial(_mla_decode_kernel, n_pages=n),
        grid_spec=grid_spec,
        out_shape=jax.ShapeDtypeStruct((nseq * DEC_PAD, MLA_HEADS * MLA_V), BF16),
        compiler_params=_compiler_params(2),
        name="mla_decode",
    )(page_table.reshape(-1), qlat, qrope, ckv, kr, wuv, *([lat_pages] * n), *([rope_pages] * n))


def _row(v, width=None):
    v = v.astype(F32).reshape(1, -1)
    if width is not None and v.shape[1] < width:
        v = jnp.pad(v, ((0, 0), (0, width - v.shape[1])))
    return v


def _pad_cols(w, width):
    return jnp.pad(w, ((0, 0), (0, width - w.shape[1])))


def _prep_ffn(w_up, w_down):
    gate = w_up[:, :D_FF].reshape(D_MODEL, N_FF_CHUNKS, FF_CHUNK)
    up = w_up[:, D_FF:].reshape(D_MODEL, N_FF_CHUNKS, FF_CHUNK)
    wup = jnp.concatenate([gate, up], axis=-1).transpose(1, 0, 2).astype(BF16)
    wdn = w_down.reshape(N_FF_CHUNKS, FF_CHUNK, D_MODEL).astype(BF16)
    return wup, wdn


def _prep_gdn(w_in, conv_w, a_log, dt_bias, norm_w, w_out):
    main = GDN_QKV + GDN_Z
    w = jnp.concatenate([w_in[:, :main], _pad_cols(w_in[:, main:main + GDN_HEADS], LANES),
                         _pad_cols(w_in[:, main + GDN_HEADS:], LANES)], axis=1).astype(BF16)
    convw = jnp.pad(conv_w.astype(F32), ((0, SUBLANES - GDN_CONV), (0, 0)))
    return w, convw, _row(a_log, LANES), _row(dt_bias, LANES), _row(norm_w), w_out.astype(BF16)


def _prep_mlstm(w_in, i_bias, f_bias, w_out):
    main = 2 * ML_QK + 2 * ML_V
    w = jnp.concatenate([w_in[:, :main], _pad_cols(w_in[:, main:main + ML_HEADS], LANES),
                         _pad_cols(w_in[:, main + ML_HEADS:], LANES)], axis=1).astype(BF16)
    return w, _row(i_bias, LANES), _row(f_bias, LANES), w_out.astype(BF16)


def _swap_halves(w):
    half = w.shape[-1] // 2
    return jnp.concatenate([w[..., half:], w[..., :half]], axis=-1)


def _prep_mla(w_in, q_norm, kv_norm, w_uq, w_ukv, w_out):
    base = MLA_Q_RANK + MLA_KV_RANK
    kr = w_in[:, base:]
    win = jnp.concatenate([w_in[:, :base], _pad_cols(kr, LANES), _pad_cols(_swap_halves(kr), LANES)],
                          axis=1).astype(BF16)
    wq = w_uq.reshape(MLA_Q_RANK, MLA_HEADS, MLA_NOPE + MLA_ROPE)
    nope, rope = wq[..., :MLA_NOPE], wq[..., MLA_NOPE:]
    zeros = jnp.zeros((MLA_Q_RANK, MLA_HEADS, LANES - MLA_ROPE), w_uq.dtype)
    wuq = jnp.concatenate([nope, rope, zeros], axis=-1).reshape(MLA_Q_RANK, -1).astype(BF16)
    wuqsw = jnp.concatenate([_swap_halves(rope), zeros], axis=-1).reshape(MLA_Q_RANK, -1).astype(BF16)
    wkv3 = w_ukv.reshape(MLA_KV_RANK, MLA_HEADS, MLA_NOPE + MLA_V)
    wuk_t = wkv3[..., :MLA_NOPE].transpose(1, 2, 0).astype(BF16)
    wuv = wkv3[..., MLA_NOPE:].transpose(1, 0, 2).astype(BF16)
    return dict(win=win, qnorm=_row(q_norm), kvnorm=_row(kv_norm), wuq=wuq, wuqsw=wuqsw,
                wkv=w_ukv.astype(BF16), wuk_t=wuk_t, wuv=wuv, wout=w_out.astype(BF16))


def _rope_tables(pos):
    half = MLA_ROPE // 2
    inv = ROPE_THETA ** (-jnp.arange(half, dtype=F32) / half)
    ang = pos.astype(F32)[:, None] * inv
    cos, sin = jnp.cos(ang), jnp.sin(ang)
    pad = jnp.zeros((pos.shape[0], LANES - MLA_ROPE), F32)
    return jnp.concatenate([cos, cos, pad], axis=1), jnp.concatenate([-sin, sin, pad], axis=1)


def _trunk(x, *, nseq, seqlen, n_valid, decode, states, page_table, weights, tm, tc):
    gdn_S, gdn_conv, mla_lat, mla_rope, ml_C, ml_n, ml_m = states
    chunk_g = min(GDN_CHUNK, seqlen)
    chunk_m = min(ML_CHUNK, seqlen)
    new = ([], [], [], [], [], [], [])
    counts = [0, 0, 0]
    for layer in range(4):
        kind = layer % 3
        j = counts[kind]
        counts[kind] += 1
        ng = weights["gains"][layer]
        wup, wdn = weights["ffn"][layer][0]
        x = _ffn_half(x, ng[0], ng[1], wup, wdn, tm)
        if kind == 0:
            w, convw, alog, dtb, normw, wout = weights["gdn"][j]
            qkv, z, ba = _norm_matmul(x, ng[2], w, (GDN_QKV, GDN_Z, 2 * LANES), min(tm, 256))
            o, s_new, conv_new = _gdn_seq(qkv, z, ba, gdn_conv[j], gdn_S[j], convw, alog, dtb, normw,
                                          nseq=nseq, seqlen=seqlen, tc=tc, chunk=chunk_g, n_valid=n_valid)
            new[0].append(s_new)
            new[1].append(conv_new[:, SUBLANES - (GDN_CONV - 1):, :])
        elif kind == 1:
            p = weights["mla"][j]
            cos, sin = weights["rope"]
            if decode:
                qlat, qrope, ckv, kr = _mla_proj(x, ng[2], cos, sin, p["win"], p["qnorm"], p["kvnorm"], p["wuq"],
                                                 p["wuqsw"], p["wuk_t"], tm=tm, decode=True)
                o = _mla_decode(page_table, qlat, qrope, ckv, kr, p["wuv"], mla_lat[j], mla_rope[j], nseq=nseq)
            else:
                qcat, kcat, v, ckv, kr = _mla_proj(x, ng[2], cos, sin, p["win"], p["qnorm"], p["kvnorm"], p["wuq"],
                                                   p["wuqsw"], p["wkv"], tm=tm, decode=False)
                o = _mla_attn(qcat, kcat, v, nseq=nseq, seqlen=seqlen, tq=min(256, seqlen))
            wout = p["wout"]
            new[2].append(ckv)
            new[3].append(kr)
        else:
            w, ibias, fbias, wout = weights["mlstm"][j]
            q, k, v, og, gates = _norm_matmul(x, ng[2], w, (ML_QK, ML_QK, ML_V, ML_V, 2 * LANES), min(tm, 256))
            o, c_new, n_new, m_new = _mlstm_seq(q, k, v, og, gates, ml_C[j], ml_n[j], ml_m[j], ibias, fbias,
                                                nseq=nseq, seqlen=seqlen, tc=tc, chunk=chunk_m, n_valid=n_valid)
            new[4].append(c_new)
            new[5].append(n_new[..., 0])
            new[6].append(m_new[:, 0, :ML_HEADS])
        x = _out_proj(o, x, ng[3], wout, tm)
        wup, wdn = weights["ffn"][layer][1]
        x = _ffn_half(x, ng[4], ng[5], wup, wdn, tm)
    return x, [jnp.stack(s) for s in new]


def kernel(x_prompt, x_sample, state_gdn_S, state_gdn_conv, cache_mla_latent, cache_mla_rope, state_mlstm_C, state_mlstm_n, state_mlstm_m, page_table, norm_gains, w_ffn_up, w_ffn_down, gdn_w_in, gdn_conv_w, gdn_a_log, gdn_dt_bias, gdn_norm_w, gdn_w_out, mla_w_in, mla_q_norm, mla_kv_norm, mla_w_uq, mla_w_ukv, mla_w_out, mlstm_w_in, mlstm_i_bias, mlstm_f_bias, mlstm_w_out):
    nb, seq, _ = x_prompt.shape
    db, dseq, _ = x_sample.shape
    n_gdn, n_mla, n_ml = gdn_w_in.shape[0], mla_w_in.shape[0], mlstm_w_in.shape[0]
    past = page_table.shape[1] * PAGE_SIZE

    weights = dict(
        gains=[[_row(norm_gains[l, i]) for i in range(6)] for l in range(4)],
        ffn=[[_prep_ffn(w_ffn_up[l, i], w_ffn_down[l, i]) for i in range(2)] for l in range(4)],
        gdn=[_prep_gdn(gdn_w_in[j], gdn_conv_w[j], gdn_a_log[j], gdn_dt_bias[j], gdn_norm_w[j], gdn_w_out[j])
             for j in range(n_gdn)],
        mla=[_prep_mla(mla_w_in[j], mla_q_norm[j], mla_kv_norm[j], mla_w_uq[j], mla_w_ukv[j], mla_w_out[j])
             for j in range(n_mla)],
        mlstm=[_prep_mlstm(mlstm_w_in[j], mlstm_i_bias[j], mlstm_f_bias[j], mlstm_w_out[j]) for j in range(n_ml)],
    )

    zeros = lambda *s: jnp.zeros(s, F32)
    states_p = (zeros(n_gdn, nb, GDN_HEADS, GDN_DK, GDN_DV), zeros(n_gdn, nb, SUBLANES, GDN_QKV), None, None,
                zeros(n_ml, nb, ML_HEADS, ML_DK, ML_DV), zeros(n_ml, nb, ML_HEADS, ML_DK, LANES),
                zeros(n_ml, nb, SUBLANES, LANES))
    tc = min(256, seq)
    y_p, st_p = _trunk(x_prompt.reshape(nb * seq, D_MODEL), nseq=nb, seqlen=seq, n_valid=tc, decode=False,
                       states=states_p, page_table=None,
                       weights=dict(weights, rope=_rope_tables(jnp.arange(seq))), tm=min(512, seq), tc=tc)

    pad_t = DEC_PAD - dseq
    x_s = jnp.pad(x_sample, ((0, 0), (0, pad_t), (0, 0))).reshape(db * DEC_PAD, D_MODEL)
    conv0 = jnp.pad(state_gdn_conv, ((0, 0), (0, 0), (SUBLANES - (GDN_CONV - 1), 0), (0, 0)))
    n0 = jnp.pad(state_mlstm_n[..., None], ((0, 0),) * 4 + ((0, LANES - 1),))
    m0 = jnp.pad(state_mlstm_m[:, :, None, :], ((0, 0), (0, 0), (0, SUBLANES - 1), (0, LANES - ML_HEADS)))
    states_s = (state_gdn_S, conv0,
                cache_mla_latent.reshape((n_mla, -1) + cache_mla_latent.shape[2:]),
                cache_mla_rope.reshape((n_mla, -1) + cache_mla_rope.shape[2:]),
                state_mlstm_C, n0, m0)
    pos_s = jnp.tile(past + jnp.arange(DEC_PAD), db)
    y_s, st_s = _trunk(x_s, nseq=db, seqlen=DEC_PAD, n_valid=dseq, decode=True, states=states_s,
                       page_table=page_table, weights=dict(weights, rope=_rope_tables(pos_s)),
                       tm=db * DEC_PAD, tc=DEC_PAD)

    gdn_S_p, gdn_conv_p, lat_p, rope_p, ml_C_p, ml_n_p, ml_m_p = st_p
    gdn_S_s, gdn_conv_s, lat_s, rope_s, ml_C_s, ml_n_s, ml_m_s = st_s
    unpad = lambda a: a.reshape(a.shape[0], db, DEC_PAD, a.shape[-1])[:, :, :dseq]
    return (y_p.reshape(nb, seq, D_MODEL), unpad(y_s[None])[0],
            gdn_S_p, gdn_S_s, gdn_conv_p, gdn_conv_s,
            lat_p.reshape(n_mla, -1, PAGE_SIZE, MLA_KV_RANK), unpad(lat_s),
            rope_p.reshape(n_mla, -1, PAGE_SIZE, MLA_ROPE), unpad(rope_s),
            ml_C_p, ml_C_s, ml_n_p, ml_n_s, ml_m_p, ml_m_s)
```

```python
import functools
import math

import jax
import jax.numpy as jnp
from jax import lax
from jax.experimental import pallas as pl
from jax.experimental.pallas import tpu as pltpu

F32 = jnp.float32
BF16 = jnp.bfloat16
HIGHEST = lax.Precision.HIGHEST

D_MODEL = 1024
PAGE_SIZE = 128
EPS = 1e-6

GDN_HEADS = 8
GDN_DK = 128
GDN_DV = 128
GDN_CONV = 4
GDN_CHUNK = 64
GDN_QKV = GDN_HEADS * (2 * GDN_DK + GDN_DV)
GDN_Z = GDN_HEADS * GDN_DV

MLA_HEADS = 8
MLA_NOPE = 128
MLA_ROPE = 64
MLA_V = 128
MLA_Q_RANK = 384
MLA_KV_RANK = 256
MLA_SCALE = (MLA_NOPE + MLA_ROPE) ** -0.5
ROPE_THETA = 10000.0

ML_HEADS = 4
ML_DK = 128
ML_DV = 256
ML_CHUNK = 64
ML_QK = ML_HEADS * ML_DK
ML_V = ML_HEADS * ML_DV

D_FF = 2816
FF_CHUNK = 256
N_FF_CHUNKS = D_FF // FF_CHUNK

LANES = 128
SUBLANES = 8
DEC_PAD = SUBLANES
VMEM_LIMIT = 56 * 1024 * 1024
PAGES_PER_STEP = 16


def _rms(x, g):
    return x * lax.rsqrt(jnp.mean(x * x, axis=-1, keepdims=True) + EPS) * g


def _silu(x):
    return x * jax.nn.sigmoid(x)


def _softplus(x):
    return jnp.maximum(x, 0.0) + jnp.log1p(jnp.exp(-jnp.abs(x)))


def _dot(a, b):
    return jnp.dot(a.astype(BF16), b.astype(BF16), preferred_element_type=F32)


def _dot_nt(a, b):
    return lax.dot_general(a.astype(BF16), b.astype(BF16), (((1,), (1,)), ((), ())),
                           preferred_element_type=F32)


def _dot_tn(a, b):
    return lax.dot_general(a.astype(BF16), b.astype(BF16), (((0,), (0,)), ((), ())),
                           preferred_element_type=F32)


def _dot_f32(a, b):
    return jnp.dot(a, b, precision=HIGHEST, preferred_element_type=F32)


def _split(x):
    hi = x.astype(BF16)
    return hi, (x - hi.astype(F32)).astype(BF16)


def _dot3(a, b):
    (a_hi, a_lo), (b_hi, b_lo) = a, b
    m = a_hi.shape[0]
    r = jnp.dot(jnp.concatenate([a_hi, a_lo], axis=0), b_hi, preferred_element_type=F32)
    return r[:m] + r[m:] + jnp.dot(a_hi, b_lo, preferred_element_type=F32)


def _unit_lower_inverse_minus_eye(a_mats, ri, ci, chunk):
    rs = None
    s = 1
    while s < chunk:
        shift = s.bit_length() - 1
        off = ((ri >> (shift + 1)) == (ci >> (shift + 1))) & ((ri >> shift) != (ci >> shift))
        a_offs = [jnp.where(off, a, 0.0) for a in a_mats]
        if rs is None:
            rs = [-a for a in a_offs]
        else:
            r_sp = [_split(r) for r in rs]
            bs = [a + _dot3(r, _split(a)) for a, r in zip(a_offs, r_sp)]
            rs = [r - b - _dot3(_split(b), rp) for r, b, rp in zip(rs, bs, r_sp)]
        s *= 2
    return rs


def _compiler_params(n_axes):
    return pltpu.CompilerParams(dimension_semantics=("arbitrary",) * n_axes,
                                vmem_limit_bytes=VMEM_LIMIT)


def _const_spec(shape):
    nd = len(shape)
    return pl.BlockSpec(shape, lambda *_: (0,) * nd)


def _ffn_kernel(x_ref, gpre_ref, gpost_ref, wup_ref, wdn_ref, o_ref):
    x = x_ref[...]
    xn = _rms(x, gpre_ref[...]).astype(BF16)
    acc = jnp.zeros(x.shape, F32)
    for c in range(N_FF_CHUNKS):
        gu = jnp.dot(xn, wup_ref[c], preferred_element_type=F32)
        act = (_silu(gu[:, :FF_CHUNK]) * gu[:, FF_CHUNK:]).astype(BF16)
        acc = acc + jnp.dot(act, wdn_ref[c], preferred_element_type=F32)
    o_ref[...] = x + 0.5 * _rms(acc, gpost_ref[...])


def _ffn_half(x, gpre, gpost, wup, wdn, tm):
    m = x.shape[0]
    return pl.pallas_call(
        _ffn_kernel,
        grid=(m // tm,),
        in_specs=[pl.BlockSpec((tm, D_MODEL), lambda i: (i, 0)),
                  _const_spec((1, D_MODEL)), _const_spec((1, D_MODEL)),
                  _const_spec(wup.shape), _const_spec(wdn.shape)],
        out_specs=pl.BlockSpec((tm, D_MODEL), lambda i: (i, 0)),
        out_shape=jax.ShapeDtypeStruct((m, D_MODEL), F32),
        compiler_params=_compiler_params(1),
        name="ffn_half",
    )(x, gpre, gpost, wup, wdn)


def _norm_matmul_kernel(x_ref, g_ref, w_ref, *o_refs, splits):
    xn = _rms(x_ref[...], g_ref[...]).astype(BF16)
    off = 0
    for o_ref, n in zip(o_refs, splits):
        o_ref[...] = jnp.dot(xn, w_ref[:, off:off + n], preferred_element_type=F32)
        off += n


def _norm_matmul(x, g, w, splits, tm):
    m = x.shape[0]
    return pl.pallas_call(
        functools.partial(_norm_matmul_kernel, splits=splits),
        grid=(m // tm,),
        in_specs=[pl.BlockSpec((tm, D_MODEL), lambda i: (i, 0)),
                  _const_spec((1, D_MODEL)), _const_spec(w.shape)],
        out_specs=[pl.BlockSpec((tm, n), lambda i: (i, 0)) for n in splits],
        out_shape=[jax.ShapeDtypeStruct((m, n), F32) for n in splits],
        compiler_params=_compiler_params(1),
        name="norm_matmul",
    )(x, g, w)


def _out_proj_kernel(a_ref, x_ref, g_ref, w_ref, o_ref):
    y = jnp.dot(a_ref[...], w_ref[...], preferred_element_type=F32)
    o_ref[...] = x_ref[...] + _rms(y, g_ref[...])


def _out_proj(a, x, g, w, tm):
    m = x.shape[0]
    return pl.pallas_call(
        _out_proj_kernel,
        grid=(m // tm,),
        in_specs=[pl.BlockSpec((tm, D_MODEL), lambda i: (i, 0)),
                  pl.BlockSpec((tm, D_MODEL), lambda i: (i, 0)),
                  _const_spec((1, D_MODEL)), _const_spec(w.shape)],
        out_specs=pl.BlockSpec((tm, D_MODEL), lambda i: (i, 0)),
        out_shape=jax.ShapeDtypeStruct((m, D_MODEL), F32),
        compiler_params=_compiler_params(1),
        name="out_proj",
    )(a, x, g, w)


def _gdn_seq_kernel(qkv_ref, z_ref, ba_ref, conv0_ref, s0_ref, convw_ref, alog_ref, dtb_ref, normw_ref,
                    o_ref, sout_ref, convout_ref,
                    xext, q_s, k_s, v_s, s_s, *, tc, chunk, n_valid):
    t = pl.program_id(1)
    heads = GDN_HEADS
    sub = tc // chunk
    last = min(chunk, n_valid) - 1

    @pl.when(t == 0)
    def _():
        xext[0:SUBLANES, :] = conv0_ref[...]
        s_s[...] = s0_ref[...]

    xext[SUBLANES:SUBLANES + tc, :] = qkv_ref[...]
    for part, dest in enumerate((q_s, k_s, v_s)):
        for h in range(heads):
            c0 = part * heads * GDN_DK + h * GDN_DK
            cols = slice(c0, c0 + GDN_DK)
            y = xext[5:5 + tc, cols] * convw_ref[0:1, cols]
            for j in range(1, GDN_CONV):
                y = y + xext[5 + j:5 + j + tc, cols] * convw_ref[j:j + 1, cols]
            y = _silu(y)
            if part < 2:
                y = y * lax.rsqrt(jnp.sum(y * y, axis=-1, keepdims=True) + EPS)
            if part == 0:
                y = y * GDN_DK ** -0.5
            dest[h] = y

    @pl.when(t == pl.num_programs(1) - 1)
    def _():
        convout_ref[...] = xext[n_valid:n_valid + SUBLANES, :]

    xext[0:SUBLANES, :] = xext[tc:tc + SUBLANES, :]

    ba = ba_ref[...]
    beta = jax.nn.sigmoid(ba[:, :LANES])
    if n_valid < tc:
        beta = jnp.where(lax.broadcasted_iota(jnp.int32, beta.shape, 0) < n_valid, beta, 0.0)
    g = -jnp.exp(alog_ref[...]) * _softplus(ba[:, LANES:] + dtb_ref[...])

    ri = lax.broadcasted_iota(jnp.int32, (tc, tc), 0)
    ci = lax.broadcasted_iota(jnp.int32, (tc, tc), 1)
    incl = ci <= ri
    if sub > 1:
        incl = incl & ((ri // chunk) == (ci // chunk))
    strict = incl & (ci < ri)
    gc = _dot_f32(incl.astype(F32), g)
    gc_t = gc.T
    exp_gc = jnp.exp(gc)
    chunk_rows = [slice(s * chunk, (s + 1) * chunk) for s in range(sub)]
    gc_last = [gc[s * chunk + last:s * chunk + last + 1, :] for s in range(sub)]
    exp_last = [jnp.exp(v) for v in gc_last]
    exp_rest = jnp.exp(jnp.concatenate([jnp.broadcast_to(v, (chunk, LANES)) for v in gc_last], axis=0) - gc)
    hs = range(heads)

    ks = [k_s[h] for h in hs]
    bs = [beta[:, h:h + 1] for h in hs]
    kbs = [k * b for k, b in zip(ks, bs)]
    decays = [jnp.exp(jnp.where(incl, gc[:, h:h + 1] - gc_t[h:h + 1, :], -jnp.inf)) for h in hs]
    a_mats = [_dot_nt(kb, k) * jnp.where(strict, d, 0.0) for kb, k, d in zip(kbs, ks, decays)]
    rs = _unit_lower_inverse_minus_eye(a_mats, ri, ci, chunk)
    rhss = [jnp.concatenate([v_s[h] * bs[h], kbs[h] * exp_gc[:, h:h + 1]], axis=1) for h in hs]
    sols = [rhs + _dot3(_split(r), _split(rhs)) for r, rhs in zip(rs, rhss)]
    qs = [q_s[h] for h in hs]
    attns = [_dot_nt(q, k) * d for q, k, d in zip(qs, ks, decays)]
    qgs = [qs[h] * exp_gc[:, h:h + 1] for h in hs]
    kdecs = [ks[h] * exp_rest[:, h:h + 1] for h in hs]
    states = [s_s[h] for h in hs]
    v_new = [[] for _ in hs]
    o_inter = [[] for _ in hs]
    for s, r in enumerate(chunk_rows):
        for h in hs:
            vn = sols[h][r, :GDN_DV] - _dot(sols[h][r, GDN_DV:], states[h])
            o_inter[h].append(_dot(qgs[h][r], states[h]))
            states[h] = states[h] * exp_last[s][:, h:h + 1] + _dot_tn(kdecs[h][r], vn)
            v_new[h].append(vn)
    normw = normw_ref[...]
    for h in hs:
        s_s[h] = states[h]
        o = jnp.concatenate(o_inter[h], axis=0) + _dot(attns[h], jnp.concatenate(v_new[h], axis=0))
        cols = slice(h * GDN_DV, (h + 1) * GDN_DV)
        o_ref[:, cols] = (_rms(o, normw) * _silu(z_ref[:, cols])).astype(o_ref.dtype)

    @pl.when(t == pl.num_programs(1) - 1)
    def _():
        sout_ref[...] = s_s[...]


def _gdn_seq(qkv, z, ba, conv0, s0, convw, alog, dtb, normw, *, nseq, seqlen, tc, chunk, n_valid):
    nt = seqlen // tc
    row = lambda b, t: (b * nt + t, 0)
    seq3 = lambda b, t: (b, 0, 0)
    seq4 = lambda b, t: (b, 0, 0, 0)
    m = nseq * seqlen
    return pl.pallas_call(
        functools.partial(_gdn_seq_kernel, tc=tc, chunk=chunk, n_valid=n_valid),
        grid=(nseq, nt),
        in_specs=[pl.BlockSpec((tc, GDN_QKV), row), pl.BlockSpec((tc, GDN_Z), row),
                  pl.BlockSpec((tc, 2 * LANES), row),
                  pl.BlockSpec((None, SUBLANES, GDN_QKV), seq3),
                  pl.BlockSpec((None, GDN_HEADS, GDN_DK, GDN_DV), seq4),
                  _const_spec((SUBLANES, GDN_QKV)), _const_spec((1, LANES)), _const_spec((1, LANES)),
                  _const_spec((1, GDN_DV))],
        out_specs=[pl.BlockSpec((tc, GDN_Z), row),
                   pl.BlockSpec((None, GDN_HEADS, GDN_DK, GDN_DV), seq4),
                   pl.BlockSpec((None, SUBLANES, GDN_QKV), seq3)],
        out_shape=[jax.ShapeDtypeStruct((m, GDN_Z), BF16),
                   jax.ShapeDtypeStruct((nseq, GDN_HEADS, GDN_DK, GDN_DV), F32),
                   jax.ShapeDtypeStruct((nseq, SUBLANES, GDN_QKV), F32)],
        scratch_shapes=[pltpu.VMEM((tc + SUBLANES, GDN_QKV), F32),
                        pltpu.VMEM((GDN_HEADS, tc, GDN_DK), F32),
                        pltpu.VMEM((GDN_HEADS, tc, GDN_DK), F32),
                        pltpu.VMEM((GDN_HEADS, tc, GDN_DV), F32),
                        pltpu.VMEM((GDN_HEADS, GDN_DK, GDN_DV), F32)],
        compiler_params=_compiler_params(2),
        name="gdn_seq",
    )(qkv, z, ba, conv0, s0, convw, alog, dtb, normw)


def _mlstm_seq_kernel(q_ref, k_ref, v_ref, og_ref, gates_ref, c0_ref, n0_ref, m0_ref, ibias_ref, fbias_ref,
                      h_ref, cout_ref, nout_ref, mout_ref,
                      cx_s, m_s, ig_s, lf_s, *, tc, chunk, n_valid):
    t = pl.program_id(1)
    heads = ML_HEADS
    last = min(chunk, n_valid) - 1

    @pl.when(t == 0)
    def _():
        cx_s[:, :, :ML_DV] = c0_ref[...]
        cx_s[:, :, ML_DV:] = n0_ref[...]
        m_s[...] = m0_ref[...]

    gates = gates_ref[...]
    ig_s[...] = gates[:, :LANES] + ibias_ref[...]
    lf_s[...] = -_softplus(-(gates[:, LANES:] + fbias_ref[...]))

    ri = lax.broadcasted_iota(jnp.int32, (chunk, chunk), 0)
    ci = lax.broadcasted_iota(jnp.int32, (chunk, chunk), 1)
    mask = (ci <= ri) & (ci < n_valid)
    tril = (ci <= ri).astype(F32)
    col_rows = lax.broadcasted_iota(jnp.int32, (chunk, 1), 0)
    ones_col = (lax.broadcasted_iota(jnp.int32, (chunk, LANES), 1) == 0).astype(F32)
    lane_row = lax.broadcasted_iota(jnp.int32, (1, LANES), 1)

    def chunk_body(c, carry):
        r0 = pl.multiple_of(c * chunk, chunk)
        rows = pl.ds(r0, chunk)
        ig = ig_s[rows, :]
        bcum = _dot_f32(tril, lf_s[rows, :])
        bcum_t = bcum.T
        ig_t = ig.T
        m_row = m_s[0:1, :]
        inter = bcum + m_row
        m_next = m_row
        for h in range(heads):
            bcol = bcum[:, h:h + 1]
            dmat = jnp.where(mask, bcol - bcum_t[h:h + 1, :] + ig_t[h:h + 1, :], -jnp.inf)
            inter_h = inter[:, h:h + 1]
            m_t = jnp.maximum(inter_h, jnp.max(dmat, axis=-1, keepdims=True))
            w_intra = jnp.exp(dmat - m_t)
            w_inter = jnp.exp(inter_h - m_t)
            q = q_ref[rows, h * ML_DK:(h + 1) * ML_DK] * ML_DK ** -0.5
            k = k_ref[rows, h * ML_DK:(h + 1) * ML_DK]
            vcols = slice(h * ML_DV, (h + 1) * ML_DV)
            vx = jnp.concatenate([v_ref[rows, vcols], ones_col], axis=1)
            s = _dot_nt(q, k) * w_intra
            cx = cx_s[h]
            num = w_inter * _dot(q, cx) + _dot(s, vx)
            den = num[:, ML_DV:ML_DV + 1]
            hh = num[:, :ML_DV] / jnp.maximum(jnp.abs(den), jnp.exp(-m_t))
            h_ref[rows, vcols] = (hh * jax.nn.sigmoid(og_ref[rows, vcols])).astype(h_ref.dtype)
            m_new = m_t[last:last + 1, :]
            w_state = jnp.exp(jnp.where(col_rows <= last,
                                        bcum[last:last + 1, h:h + 1] - bcol + ig[:, h:h + 1], -jnp.inf) - m_new)
            carry_decay = jnp.exp(inter_h[last:last + 1, :] - m_new)
            cx_s[h] = carry_decay * cx + _dot_tn(k * w_state, vx)
            m_next = jnp.where(lane_row == h, m_new, m_next)
        m_s[0:1, :] = m_next
        return carry

    lax.fori_loop(0, tc // chunk, chunk_body, 0)

    @pl.when(t == pl.num_programs(1) - 1)
    def _():
        cout_ref[...] = cx_s[:, :, :ML_DV]
        nout_ref[...] = cx_s[:, :, ML_DV:]
        mout_ref[...] = m_s[...]


def _mlstm_seq(q, k, v, og, gates, c0, n0, m0, ibias, fbias, *, nseq, seqlen, tc, chunk, n_valid):
    nt = seqlen // tc
    row = lambda b, t: (b * nt + t, 0)
    seq3 = lambda b, t: (b, 0, 0)
    seq4 = lambda b, t: (b, 0, 0, 0)
    m = nseq * seqlen
    return pl.pallas_call(
        functools.partial(_mlstm_seq_kernel, tc=tc, chunk=chunk, n_valid=n_valid),
        grid=(nseq, nt),
        in_specs=[pl.BlockSpec((tc, ML_QK), row), pl.BlockSpec((tc, ML_QK), row),
                  pl.BlockSpec((tc, ML_V), row), pl.BlockSpec((tc, ML_V), row),
                  pl.BlockSpec((tc, 2 * LANES), row),
                  pl.BlockSpec((None, ML_HEADS, ML_DK, ML_DV), seq4),
                  pl.BlockSpec((None, ML_HEADS, ML_DK, LANES), seq4),
                  pl.BlockSpec((None, SUBLANES, LANES), seq3),
                  _const_spec((1, LANES)), _const_spec((1, LANES))],
        out_specs=[pl.BlockSpec((tc, ML_V), row),
                   pl.BlockSpec((None, ML_HEADS, ML_DK, ML_DV), seq4),
                   pl.BlockSpec((None, ML_HEADS, ML_DK, LANES), seq4),
                   pl.BlockSpec((None, SUBLANES, LANES), seq3)],
        out_shape=[jax.ShapeDtypeStruct((m, ML_V), BF16),
                   jax.ShapeDtypeStruct((nseq, ML_HEADS, ML_DK, ML_DV), F32),
                   jax.ShapeDtypeStruct((nseq, ML_HEADS, ML_DK, LANES), F32),
                   jax.ShapeDtypeStruct((nseq, SUBLANES, LANES), F32)],
        scratch_shapes=[pltpu.VMEM((ML_HEADS, ML_DK, ML_DV + LANES), F32),
                        pltpu.VMEM((SUBLANES, LANES), F32),
                        pltpu.VMEM((tc, LANES), F32),
                        pltpu.VMEM((tc, LANES), F32)],
        compiler_params=_compiler_params(2),
        name="mlstm_seq",
    )(q, k, v, og, gates, c0, n0, m0, ibias, fbias)


def _mla_proj_kernel(x_ref, g_ref, cos_ref, sin_ref, win_ref, qnorm_ref, kvnorm_ref, wuq_ref, wuqsw_ref, wkv_ref,
                     *o_refs, decode):
    xn = _rms(x_ref[...], g_ref[...]).astype(BF16)
    proj = jnp.dot(xn, win_ref[...], preferred_element_type=F32)
    cq = _rms(proj[:, :MLA_Q_RANK], qnorm_ref[...]).astype(BF16)
    ckv = _rms(proj[:, MLA_Q_RANK:MLA_Q_RANK + MLA_KV_RANK], kvnorm_ref[...])
    base = MLA_Q_RANK + MLA_KV_RANK
    cos = cos_ref[...]
    sin = sin_ref[...]
    kr = proj[:, base:base + LANES] * cos + proj[:, base + LANES:base + 2 * LANES] * sin
    q_main = jnp.dot(cq, wuq_ref[...], preferred_element_type=F32)
    q_swap = jnp.dot(cq, wuqsw_ref[...], preferred_element_type=F32)
    if decode:
        qlat_ref, qrope_ref, ckv_ref, kr_ref = o_refs
    else:
        qcat_ref, kcat_ref, v_ref, ckv_ref, kr_ref = o_refs
        kv = jnp.dot(ckv.astype(BF16), wkv_ref[...], preferred_element_type=F32)
    ckv_ref[...] = ckv
    kr_ref[...] = kr[:, :MLA_ROPE]
    for h in range(MLA_HEADS):
        lo = h * 2 * LANES
        q_nope = q_main[:, lo:lo + LANES] * MLA_SCALE
        q_rope = (q_main[:, lo + LANES:lo + 2 * LANES] * cos + q_swap[:, h * LANES:(h + 1) * LANES] * sin) * MLA_SCALE
        if decode:
            qlat_ref[:, lo:lo + 2 * LANES] = _dot(q_nope, wkv_ref[h]).astype(BF16)
            qrope_ref[:, h * LANES:(h + 1) * LANES] = q_rope.astype(BF16)
        else:
            qcat_ref[:, lo:lo + LANES] = q_nope.astype(BF16)
            qcat_ref[:, lo + LANES:lo + 2 * LANES] = q_rope.astype(BF16)
            kcat_ref[:, lo:lo + LANES] = kv[:, lo:lo + LANES].astype(BF16)
            kcat_ref[:, lo + LANES:lo + 2 * LANES] = kr.astype(BF16)
            v_ref[:, h * MLA_V:(h + 1) * MLA_V] = kv[:, lo + LANES:lo + 2 * LANES].astype(BF16)


def _mla_proj(x, g, cos, sin, win, qnorm, kvnorm, wuq, wuqsw, wkv, *, tm, decode):
    m = x.shape[0]
    n_pos_blocks = cos.shape[0] // tm
    row = lambda i: (i, 0)
    pos = lambda i: (i % n_pos_blocks, 0)
    wide = MLA_HEADS * 2 * LANES
    if decode:
        widths = (wide, MLA_HEADS * LANES, MLA_KV_RANK, MLA_ROPE)
        dtypes = (BF16, BF16, F32, F32)
    else:
        widths = (wide, wide, MLA_HEADS * MLA_V, MLA_KV_RANK, MLA_ROPE)
        dtypes = (BF16, BF16, BF16, F32, F32)
    return pl.pallas_call(
        functools.partial(_mla_proj_kernel, decode=decode),
        grid=(m // tm,),
        in_specs=[pl.BlockSpec((tm, D_MODEL), row), _const_spec((1, D_MODEL)),
                  pl.BlockSpec((tm, LANES), pos), pl.BlockSpec((tm, LANES), pos),
                  _const_spec(win.shape), _const_spec((1, MLA_Q_RANK)), _const_spec((1, MLA_KV_RANK)),
                  _const_spec(wuq.shape), _const_spec(wuqsw.shape), _const_spec(wkv.shape)],
        out_specs=[pl.BlockSpec((tm, n), row) for n in widths],
        out_shape=[jax.ShapeDtypeStruct((m, n), dt) for n, dt in zip(widths, dtypes)],
        compiler_params=_compiler_params(1),
        name="mla_proj_decode" if decode else "mla_proj",
    )(x, g, cos, sin, win, qnorm, kvnorm, wuq, wuqsw, wkv)


def _mla_attn_kernel(q_ref, k_ref, v_ref, o_ref, *, tq):
    qi = pl.program_id(2)
    q = q_ref[...]
    ri = lax.broadcasted_iota(jnp.int32, (tq, tq), 0)
    ci = lax.broadcasted_iota(jnp.int32, (tq, tq), 1)

    def step(j, carry):
        m, l, acc = carry
        rows = pl.ds(pl.multiple_of(j * tq, tq), tq)
        s = lax.dot_general(q, k_ref[rows, :], (((1,), (1,)), ((), ())), preferred_element_type=F32)
        s = jnp.where((j * tq + ci) <= (qi * tq + ri), s, -jnp.inf)
        m_new = jnp.maximum(m, jnp.max(s, axis=-1, keepdims=True))
        p = jnp.exp(s - m_new)
        alpha = jnp.exp(m - m_new)
        l = alpha * l + jnp.sum(p, axis=-1, keepdims=True)
        acc = alpha * acc + jnp.dot(p.astype(BF16), v_ref[rows, :], preferred_element_type=F32)
        return m_new, l, acc

    init = (jnp.full((tq, 1), -jnp.inf, F32), jnp.zeros((tq, 1), F32), jnp.zeros((tq, MLA_V), F32))
    _, l, acc = lax.fori_loop(0, qi + 1, step, init)
    o_ref[...] = (acc / l).astype(o_ref.dtype)


def _mla_attn(qcat, kcat, v, *, nseq, seqlen, tq):
    nq = seqlen // tq
    m = nseq * seqlen
    return pl.pallas_call(
        functools.partial(_mla_attn_kernel, tq=tq),
        grid=(nseq, MLA_HEADS, nq),
        in_specs=[pl.BlockSpec((tq, 2 * LANES), lambda b, h, i: (b * nq + i, h)),
                  pl.BlockSpec((seqlen, 2 * LANES), lambda b, h, i: (b, h)),
                  pl.BlockSpec((seqlen, MLA_V), lambda b, h, i: (b, h))],
        out_specs=pl.BlockSpec((tq, MLA_V), lambda b, h, i: (b * nq + i, h)),
        out_shape=jax.ShapeDtypeStruct((m, MLA_HEADS * MLA_V), BF16),
        compiler_params=_compiler_params(3),
        name="mla_attn",
    )(qcat, kcat, v)


def _mla_decode_kernel(pt_ref, qlat_ref, qrope_ref, ckv_ref, kr_ref, wuv_ref, *rest, n_pages):
    lat_refs = rest[:n_pages]
    rope_refs = rest[n_pages:2 * n_pages]
    o_ref = rest[2 * n_pages]
    ql_s, qr_s, m_s, l_s, acc_s = rest[2 * n_pages + 1:]
    g = pl.program_id(1)
    heads = MLA_HEADS
    tok = DEC_PAD

    @pl.when(g == 0)
    def _():
        for h in range(heads):
            ql_s[h * tok:(h + 1) * tok, :] = qlat_ref[:, h * 2 * LANES:(h + 1) * 2 * LANES]
            qr_s[h * tok:(h + 1) * tok, :] = qrope_ref[:, h * LANES:(h + 1) * LANES]
        m_s[...] = jnp.full(m_s.shape, -jnp.inf, F32)
        l_s[...] = jnp.zeros(l_s.shape, F32)
        acc_s[...] = jnp.zeros(acc_s.shape, F32)

    ql = ql_s[...]
    qr = qr_s[:, :MLA_ROPE]

    def update(s, values):
        m_old = m_s[...]
        m_new = jnp.maximum(m_old, jnp.max(s, axis=-1, keepdims=True))
        p = jnp.exp(s - m_new)
        alpha = jnp.exp(m_old - m_new)
        l_s[...] = alpha * l_s[...] + jnp.sum(p, axis=-1, keepdims=True)
        pb = p.astype(BF16)
        acc = alpha * acc_s[...]
        for cols, val in values:
            acc = acc + jnp.dot(pb[:, cols], val, preferred_element_type=F32)
        acc_s[...] = acc
        m_s[...] = m_new

    lats = [r[...].astype(BF16) for r in lat_refs]
    scores = [_dot_nt(ql, lat) + _dot_nt(qr, r[...]) for lat, r in zip(lats, rope_refs)]
    update(jnp.concatenate(scores, axis=1),
           [(slice(i * PAGE_SIZE, (i + 1) * PAGE_SIZE), lat) for i, lat in enumerate(lats)])

    @pl.when(g == pl.num_programs(1) - 1)
    def _():
        ckv = ckv_ref[...].astype(BF16)
        s = _dot_nt(ql, ckv) + _dot_nt(qr, kr_ref[...])
        qt = lax.broadcasted_iota(jnp.int32, s.shape, 0) % tok
        kt = lax.broadcasted_iota(jnp.int32, s.shape, 1)
        update(jnp.where(kt <= qt, s, -jnp.inf), [(slice(0, tok), ckv)])
        o_lat = acc_s[...] / l_s[...]
        for h in range(heads):
            o_ref[:, h * MLA_V:(h + 1) * MLA_V] = _dot(o_lat[h * tok:(h + 1) * tok, :], wuv_ref[h]).astype(o_ref.dtype)


def _mla_decode(page_table, qlat, qrope, ckv, kr, wuv, lat_pages, rope_pages, *, nseq):
    n_pages_seq = page_table.shape[1]
    n = PAGES_PER_STEP
    groups = n_pages_seq // n
    seq = lambda b, g, pt: (b, 0)

    def page_map(i):
        return lambda b, g, pt: (pt[b * n_pages_seq + g * n + i], 0, 0)

    rows = MLA_HEADS * DEC_PAD
    grid_spec = pltpu.PrefetchScalarGridSpec(
        num_scalar_prefetch=1,
        grid=(nseq, groups),
        in_specs=[pl.BlockSpec((DEC_PAD, MLA_HEADS * 2 * LANES), seq),
                  pl.BlockSpec((DEC_PAD, MLA_HEADS * LANES), seq),
                  pl.BlockSpec((DEC_PAD, MLA_KV_RANK), seq),
                  pl.BlockSpec((DEC_PAD, MLA_ROPE), seq),
                  pl.BlockSpec(wuv.shape, lambda b, g, pt: (0, 0, 0))]
                 + [pl.BlockSpec((None, PAGE_SIZE, MLA_KV_RANK), page_map(i)) for i in range(n)]
                 + [pl.BlockSpec((None, PAGE_SIZE, MLA_ROPE), page_map(i)) for i in range(n)],
        out_specs=pl.BlockSpec((DEC_PAD, MLA_HEADS * MLA_V), seq),
        scratch_shapes=[pltpu.VMEM((rows, MLA_KV_RANK), BF16), pltpu.VMEM((rows, LANES), BF16),
                        pltpu.VMEM((rows, 1), F32), pltpu.VMEM((rows, 1), F32),
                        pltpu.VMEM((rows, MLA_KV_RANK), F32)],
    )
    return pl.pallas_call(
        functools.partial(_mla_decode_kernel, n_pages=n),
        grid_spec=grid_spec,
        out_shape=jax.ShapeDtypeStruct((nseq * DEC_PAD, MLA_HEADS * MLA_V), BF16),
        compiler_params=_compiler_params(2),
        name="mla_decode",
    )(page_table.reshape(-1), qlat, qrope, ckv, kr, wuv, *([lat_pages] * n), *([rope_pages] * n))


def _row(v, width=None):
    v = v.astype(F32).reshape(1, -1)
    if width is not None and v.shape[1] < width:
        v = jnp.pad(v, ((0, 0), (0, width - v.shape[1])))
    return v


def _pad_cols(w, width):
    return jnp.pad(w, ((0, 0), (0, width - w.shape[1])))


def _prep_ffn(w_up, w_down):
    gate = w_up[:, :D_FF].reshape(D_MODEL, N_FF_CHUNKS, FF_CHUNK)
    up = w_up[:, D_FF:].reshape(D_MODEL, N_FF_CHUNKS, FF_CHUNK)
    wup = jnp.concatenate([gate, up], axis=-1).transpose(1, 0, 2).astype(BF16)
    wdn = w_down.reshape(N_FF_CHUNKS, FF_CHUNK, D_MODEL).astype(BF16)
    return wup, wdn


def _prep_gdn(w_in, conv_w, a_log, dt_bias, norm_w, w_out):
    main = GDN_QKV + GDN_Z
    w = jnp.concatenate([w_in[:, :main], _pad_cols(w_in[:, main:main + GDN_HEADS], LANES),
                         _pad_cols(w_in[:, main + GDN_HEADS:], LANES)], axis=1).astype(BF16)
    convw = jnp.pad(conv_w.astype(F32), ((0, SUBLANES - GDN_CONV), (0, 0)))
    return w, convw, _row(a_log, LANES), _row(dt_bias, LANES), _row(norm_w), w_out.astype(BF16)


def _prep_mlstm(w_in, i_bias, f_bias, w_out):
    main = 2 * ML_QK + 2 * ML_V
    w = jnp.concatenate([w_in[:, :main], _pad_cols(w_in[:, main:main + ML_HEADS], LANES),
                         _pad_cols(w_in[:, main + ML_HEADS:], LANES)], axis=1).astype(BF16)
    return w, _row(i_bias, LANES), _row(f_bias, LANES), w_out.astype(BF16)


def _swap_halves(w):
    half = w.shape[-1] // 2
    return jnp.concatenate([w[..., half:], w[..., :half]], axis=-1)


def _prep_mla(w_in, q_norm, kv_norm, w_uq, w_ukv, w_out):
    base = MLA_Q_RANK + MLA_KV_RANK
    kr = w_in[:, base:]
    win = jnp.concatenate([w_in[:, :base], _pad_cols(kr, LANES), _pad_cols(_swap_halves(kr), LANES)],
                          axis=1).astype(BF16)
    wq = w_uq.reshape(MLA_Q_RANK, MLA_HEADS, MLA_NOPE + MLA_ROPE)
    nope, rope = wq[..., :MLA_NOPE], wq[..., MLA_NOPE:]
    zeros = jnp.zeros((MLA_Q_RANK, MLA_HEADS, LANES - MLA_ROPE), w_uq.dtype)
    wuq = jnp.concatenate([nope, rope, zeros], axis=-1).reshape(MLA_Q_RANK, -1).astype(BF16)
    wuqsw = jnp.concatenate([_swap_halves(rope), zeros], axis=-1).reshape(MLA_Q_RANK, -1).astype(BF16)
    wkv3 = w_ukv.reshape(MLA_KV_RANK, MLA_HEADS, MLA_NOPE + MLA_V)
    wuk_t = wkv3[..., :MLA_NOPE].transpose(1, 2, 0).astype(BF16)
    wuv = wkv3[..., MLA_NOPE:].transpose(1, 0, 2).astype(BF16)
    return dict(win=win, qnorm=_row(q_norm), kvnorm=_row(kv_norm), wuq=wuq, wuqsw=wuqsw,
                wkv=w_ukv.astype(BF16), wuk_t=wuk_t, wuv=wuv, wout=w_out.astype(BF16))


def _rope_tables(pos):
    half = MLA_ROPE // 2
    inv = ROPE_THETA ** (-jnp.arange(half, dtype=F32) / half)
    ang = pos.astype(F32)[:, None] * inv
    cos, sin = jnp.cos(ang), jnp.sin(ang)
    pad = jnp.zeros((pos.shape[0], LANES - MLA_ROPE), F32)
    return jnp.concatenate([cos, cos, pad], axis=1), jnp.concatenate([-sin, sin, pad], axis=1)


def _tiles(nseq, seqlen):
    m = nseq * seqlen
    return dict(ffn=min(512, m), proj=min(256, m), mla_proj=min(512, seqlen), attn_q=min(256, seqlen),
                gdn=min(2 * GDN_CHUNK, seqlen), mlstm=min(256, seqlen))


def _trunk(x, *, nseq, seqlen, n_tokens, decode, states, page_table, weights):
    gdn_S, gdn_conv, mla_lat, mla_rope, ml_C, ml_n, ml_m = states
    chunk_g = min(GDN_CHUNK, seqlen)
    chunk_m = min(ML_CHUNK, seqlen)
    tiles = _tiles(nseq, seqlen)
    tm = tiles["ffn"]
    new = ([], [], [], [], [], [], [])
    counts = [0, 0, 0]
    for layer in range(4):
        kind = layer % 3
        j = counts[kind]
        counts[kind] += 1
        ng = weights["gains"][layer]
        wup, wdn = weights["ffn"][layer][0]
        x = _ffn_half(x, ng[0], ng[1], wup, wdn, tm)
        if kind == 0:
            w, convw, alog, dtb, normw, wout = weights["gdn"][j]
            qkv, z, ba = _norm_matmul(x, ng[2], w, (GDN_QKV, GDN_Z, 2 * LANES), tiles["proj"])
            o, s_new, conv_new = _gdn_seq(qkv, z, ba, gdn_conv[j], gdn_S[j], convw, alog, dtb, normw,
                                          nseq=nseq, seqlen=seqlen, tc=tiles["gdn"], chunk=chunk_g,
                                          n_valid=min(tiles["gdn"], n_tokens))
            new[0].append(s_new)
            new[1].append(conv_new[:, SUBLANES - (GDN_CONV - 1):, :])
        elif kind == 1:
            p = weights["mla"][j]
            cos, sin = weights["rope"]
            if decode:
                qlat, qrope, ckv, kr = _mla_proj(x, ng[2], cos, sin, p["win"], p["qnorm"], p["kvnorm"], p["wuq"],
                                                 p["wuqsw"], p["wuk_t"], tm=tiles["mla_proj"], decode=True)
                o = _mla_decode(page_table, qlat, qrope, ckv, kr, p["wuv"], mla_lat[j], mla_rope[j], nseq=nseq)
            else:
                qcat, kcat, v, ckv, kr = _mla_proj(x, ng[2], cos, sin, p["win"], p["qnorm"], p["kvnorm"], p["wuq"],
                                                   p["wuqsw"], p["wkv"], tm=tiles["mla_proj"], decode=False)
                o = _mla_attn(qcat, kcat, v, nseq=nseq, seqlen=seqlen, tq=tiles["attn_q"])
            wout = p["wout"]
            new[2].append(ckv)
            new[3].append(kr)
        else:
            w, ibias, fbias, wout = weights["mlstm"][j]
            q, k, v, og, gates = _norm_matmul(x, ng[2], w, (ML_QK, ML_QK, ML_V, ML_V, 2 * LANES), tiles["proj"])
            o, c_new, n_new, m_new = _mlstm_seq(q, k, v, og, gates, ml_C[j], ml_n[j], ml_m[j], ibias, fbias,
                                                nseq=nseq, seqlen=seqlen, tc=tiles["mlstm"], chunk=chunk_m,
                                                n_valid=min(tiles["mlstm"], n_tokens))
            new[4].append(c_new)
            new[5].append(n_new[..., 0])
            new[6].append(m_new[:, 0, :ML_HEADS])
        x = _out_proj(o, x, ng[3], wout, tm)
        wup, wdn = weights["ffn"][layer][1]
        x = _ffn_half(x, ng[4], ng[5], wup, wdn, tm)
    return x, [jnp.stack(s) for s in new]


def kernel(x_prompt, x_sample, state_gdn_S, state_gdn_conv, cache_mla_latent, cache_mla_rope, state_mlstm_C, state_mlstm_n, state_mlstm_m, page_table, norm_gains, w_ffn_up, w_ffn_down, gdn_w_in, gdn_conv_w, gdn_a_log, gdn_dt_bias, gdn_norm_w, gdn_w_out, mla_w_in, mla_q_norm, mla_kv_norm, mla_w_uq, mla_w_ukv, mla_w_out, mlstm_w_in, mlstm_i_bias, mlstm_f_bias, mlstm_w_out):
    nb, seq, _ = x_prompt.shape
    db, dseq, _ = x_sample.shape
    n_gdn, n_mla, n_ml = gdn_w_in.shape[0], mla_w_in.shape[0], mlstm_w_in.shape[0]
    past = page_table.shape[1] * PAGE_SIZE

    weights = dict(
        gains=[[_row(norm_gains[l, i]) for i in range(6)] for l in range(4)],
        ffn=[[_prep_ffn(w_ffn_up[l, i], w_ffn_down[l, i]) for i in range(2)] for l in range(4)],
        gdn=[_prep_gdn(gdn_w_in[j], gdn_conv_w[j], gdn_a_log[j], gdn_dt_bias[j], gdn_norm_w[j], gdn_w_out[j])
             for j in range(n_gdn)],
        mla=[_prep_mla(mla_w_in[j], mla_q_norm[j], mla_kv_norm[j], mla_w_uq[j], mla_w_ukv[j], mla_w_out[j])
             for j in range(n_mla)],
        mlstm=[_prep_mlstm(mlstm_w_in[j], mlstm_i_bias[j], mlstm_f_bias[j], mlstm_w_out[j]) for j in range(n_ml)],
    )

    zeros = lambda *s: jnp.zeros(s, F32)
    states_p = (zeros(n_gdn, nb, GDN_HEADS, GDN_DK, GDN_DV), zeros(n_gdn, nb, SUBLANES, GDN_QKV), None, None,
                zeros(n_ml, nb, ML_HEADS, ML_DK, ML_DV), zeros(n_ml, nb, ML_HEADS, ML_DK, LANES),
                zeros(n_ml, nb, SUBLANES, LANES))
    y_p, st_p = _trunk(x_prompt.reshape(nb * seq, D_MODEL), nseq=nb, seqlen=seq, n_tokens=seq, decode=False,
                       states=states_p, page_table=None,
                       weights=dict(weights, rope=_rope_tables(jnp.arange(seq))))

    pad_t = DEC_PAD - dseq
    x_s = jnp.pad(x_sample, ((0, 0), (0, pad_t), (0, 0))).reshape(db * DEC_PAD, D_MODEL)
    conv0 = jnp.pad(state_gdn_conv, ((0, 0), (0, 0), (SUBLANES - (GDN_CONV - 1), 0), (0, 0)))
    n0 = jnp.pad(state_mlstm_n[..., None], ((0, 0),) * 4 + ((0, LANES - 1),))
    m0 = jnp.pad(state_mlstm_m[:, :, None, :], ((0, 0), (0, 0), (0, SUBLANES - 1), (0, LANES - ML_HEADS)))
    states_s = (state_gdn_S, conv0,
                cache_mla_latent.reshape((n_mla, -1) + cache_mla_latent.shape[2:]),
                cache_mla_rope.reshape((n_mla, -1) + cache_mla_rope.shape[2:]),
                state_mlstm_C, n0, m0)
    pos_s = jnp.tile(past + jnp.arange(DEC_PAD), db)
    y_s, st_s = _trunk(x_s, nseq=db, seqlen=DEC_PAD, n_tokens=dseq, decode=True, states=states_s,
                       page_table=page_table, weights=dict(weights, rope=_rope_tables(pos_s)))

    gdn_S_p, gdn_conv_p, lat_p, rope_p, ml_C_p, ml_n_p, ml_m_p = st_p
    gdn_S_s, gdn_conv_s, lat_s, rope_s, ml_C_s, ml_n_s, ml_m_s = st_s
    unpad = lambda a: a.reshape(a.shape[0], db, DEC_PAD, a.shape[-1])[:, :, :dseq]
    return (y_p.reshape(nb, seq, D_MODEL), unpad(y_s[None])[0],
            gdn_S_p, gdn_S_s, gdn_conv_p, gdn_conv_s,
            lat_p.reshape(n_mla, -1, PAGE_SIZE, MLA_KV_RANK), unpad(lat_s),
            rope_p.reshape(n_mla, -1, PAGE_SIZE, MLA_ROPE), unpad(rope_s),
            ml_C_p, ml_C_s, ml_n_p, ml_n_s, ml_m_p, ml_m_s)
```

```python
import functools
import math

import jax
import jax.numpy as jnp
from jax import lax
from jax.experimental import pallas as pl
from jax.experimental.pallas import tpu as pltpu

F32 = jnp.float32
BF16 = jnp.bfloat16
HIGHEST = lax.Precision.HIGHEST

D_MODEL = 1024
PAGE_SIZE = 128
EPS = 1e-6

GDN_HEADS = 8
GDN_DK = 128
GDN_DV = 128
GDN_CONV = 4
GDN_CHUNK = 64
GDN_QKV = GDN_HEADS * (2 * GDN_DK + GDN_DV)
GDN_Z = GDN_HEADS * GDN_DV

MLA_HEADS = 8
MLA_NOPE = 128
MLA_ROPE = 64
MLA_V = 128
MLA_Q_RANK = 384
MLA_KV_RANK = 256
MLA_SCALE = (MLA_NOPE + MLA_ROPE) ** -0.5
ROPE_THETA = 10000.0

ML_HEADS = 4
ML_DK = 128
ML_DV = 256
ML_CHUNK = 64
ML_QK = ML_HEADS * ML_DK
ML_V = ML_HEADS * ML_DV

D_FF = 2816
FF_CHUNK = 256
N_FF_CHUNKS = D_FF // FF_CHUNK

LANES = 128
SUBLANES = 8
DEC_PAD = SUBLANES
VMEM_LIMIT = 56 * 1024 * 1024
PAGES_PER_STEP = 16
ATTN_HEAD_GROUP = 4


def _rms(x, g):
    return x * lax.rsqrt(jnp.mean(x * x, axis=-1, keepdims=True) + EPS) * g


def _silu(x):
    return x * jax.nn.sigmoid(x)


def _softplus(x):
    return jnp.maximum(x, 0.0) + jnp.log1p(jnp.exp(-jnp.abs(x)))


def _dot(a, b):
    return jnp.dot(a.astype(BF16), b.astype(BF16), preferred_element_type=F32)


def _dot_nt(a, b):
    return lax.dot_general(a.astype(BF16), b.astype(BF16), (((1,), (1,)), ((), ())),
                           preferred_element_type=F32)


def _dot_tn(a, b):
    return lax.dot_general(a.astype(BF16), b.astype(BF16), (((0,), (0,)), ((), ())),
                           preferred_element_type=F32)


def _dot_f32(a, b):
    return jnp.dot(a, b, precision=HIGHEST, preferred_element_type=F32)


def _split(x):
    hi = x.astype(BF16)
    return hi, (x - hi.astype(F32)).astype(BF16)


def _dot3(a, b):
    (a_hi, a_lo), (b_hi, b_lo) = a, b
    m = a_hi.shape[0]
    r = jnp.dot(jnp.concatenate([a_hi, a_lo], axis=0), b_hi, preferred_element_type=F32)
    return r[:m] + r[m:] + jnp.dot(a_hi, b_lo, preferred_element_type=F32)


def _unit_lower_inverse_minus_eye(a_mats, ri, ci, chunk):
    rs = None
    s = 1
    while s < chunk:
        shift = s.bit_length() - 1
        off = ((ri >> (shift + 1)) == (ci >> (shift + 1))) & ((ri >> shift) != (ci >> shift))
        a_offs = [jnp.where(off, a, 0.0) for a in a_mats]
        if rs is None:
            rs = [-a for a in a_offs]
        else:
            r_sp = [_split(r) for r in rs]
            bs = [a + _dot3(r, _split(a)) for a, r in zip(a_offs, r_sp)]
            rs = [r - b - _dot3(_split(b), rp) for r, b, rp in zip(rs, bs, r_sp)]
        s *= 2
    return rs


def _compiler_params(n_axes):
    return pltpu.CompilerParams(dimension_semantics=("arbitrary",) * n_axes,
                                vmem_limit_bytes=VMEM_LIMIT)


def _const_spec(shape):
    nd = len(shape)
    return pl.BlockSpec(shape, lambda *_: (0,) * nd)


def _ffn_kernel(x_ref, gpre_ref, gpost_ref, wup_ref, wdn_ref, o_ref):
    x = x_ref[...]
    xn = _rms(x, gpre_ref[...]).astype(BF16)
    acc = jnp.zeros(x.shape, F32)
    for c in range(N_FF_CHUNKS):
        lo = c * FF_CHUNK
        gate = jnp.dot(xn, wup_ref[:, lo:lo + FF_CHUNK], preferred_element_type=F32)
        up = jnp.dot(xn, wup_ref[:, D_FF + lo:D_FF + lo + FF_CHUNK], preferred_element_type=F32)
        act = (_silu(gate) * up).astype(BF16)
        acc = acc + jnp.dot(act, wdn_ref[c], preferred_element_type=F32)
    o_ref[...] = x + 0.5 * _rms(acc, gpost_ref[...])


def _ffn_half(x, gpre, gpost, wup, wdn, tm):
    m = x.shape[0]
    return pl.pallas_call(
        _ffn_kernel,
        grid=(m // tm,),
        in_specs=[pl.BlockSpec((tm, D_MODEL), lambda i: (i, 0)),
                  _const_spec((1, D_MODEL)), _const_spec((1, D_MODEL)),
                  _const_spec(wup.shape), _const_spec(wdn.shape)],
        out_specs=pl.BlockSpec((tm, D_MODEL), lambda i: (i, 0)),
        out_shape=jax.ShapeDtypeStruct((m, D_MODEL), F32),
        compiler_params=_compiler_params(1),
        name="ffn_half",
    )(x, gpre, gpost, wup, wdn)


def _norm_matmul_kernel(x_ref, g_ref, w_ref, *o_refs, splits):
    xn = _rms(x_ref[...], g_ref[...]).astype(BF16)
    off = 0
    for o_ref, n in zip(o_refs, splits):
        o_ref[...] = jnp.dot(xn, w_ref[:, off:off + n], preferred_element_type=F32)
        off += n


def _norm_matmul(x, g, w, splits, tm):
    m = x.shape[0]
    return pl.pallas_call(
        functools.partial(_norm_matmul_kernel, splits=splits),
        grid=(m // tm,),
        in_specs=[pl.BlockSpec((tm, D_MODEL), lambda i: (i, 0)),
                  _const_spec((1, D_MODEL)), _const_spec(w.shape)],
        out_specs=[pl.BlockSpec((tm, n), lambda i: (i, 0)) for n in splits],
        out_shape=[jax.ShapeDtypeStruct((m, n), F32) for n in splits],
        compiler_params=_compiler_params(1),
        name="norm_matmul",
    )(x, g, w)


def _out_proj_kernel(a_ref, x_ref, g_ref, w_ref, o_ref):
    y = jnp.dot(a_ref[...], w_ref[...], preferred_element_type=F32)
    o_ref[...] = x_ref[...] + _rms(y, g_ref[...])


def _out_proj(a, x, g, w, tm):
    m = x.shape[0]
    return pl.pallas_call(
        _out_proj_kernel,
        grid=(m // tm,),
        in_specs=[pl.BlockSpec((tm, D_MODEL), lambda i: (i, 0)),
                  pl.BlockSpec((tm, D_MODEL), lambda i: (i, 0)),
                  _const_spec((1, D_MODEL)), _const_spec(w.shape)],
        out_specs=pl.BlockSpec((tm, D_MODEL), lambda i: (i, 0)),
        out_shape=jax.ShapeDtypeStruct((m, D_MODEL), F32),
        compiler_params=_compiler_params(1),
        name="out_proj",
    )(a, x, g, w)


def _gdn_seq_kernel(qkv_ref, z_ref, ba_ref, conv0_ref, s0_ref, convw_ref, alog_ref, dtb_ref, normw_ref,
                    o_ref, sout_ref, convout_ref,
                    xext, q_s, k_s, v_s, s_s, *, tc, chunk, n_valid):
    t = pl.program_id(1)
    heads = GDN_HEADS
    sub = tc // chunk
    last = min(chunk, n_valid) - 1

    @pl.when(t == 0)
    def _():
        xext[0:SUBLANES, :] = conv0_ref[...]
        s_s[...] = s0_ref[...]

    xext[SUBLANES:SUBLANES + tc, :] = qkv_ref[...]
    for part, dest in enumerate((q_s, k_s, v_s)):
        for h in range(heads):
            c0 = part * heads * GDN_DK + h * GDN_DK
            cols = slice(c0, c0 + GDN_DK)
            y = xext[5:5 + tc, cols] * convw_ref[0:1, cols]
            for j in range(1, GDN_CONV):
                y = y + xext[5 + j:5 + j + tc, cols] * convw_ref[j:j + 1, cols]
            y = _silu(y)
            if part < 2:
                y = y * lax.rsqrt(jnp.sum(y * y, axis=-1, keepdims=True) + EPS)
            if part == 0:
                y = y * GDN_DK ** -0.5
            dest[h] = y

    @pl.when(t == pl.num_programs(1) - 1)
    def _():
        convout_ref[...] = xext[n_valid:n_valid + SUBLANES, :]

    xext[0:SUBLANES, :] = xext[tc:tc + SUBLANES, :]

    ba = ba_ref[...]
    beta = jax.nn.sigmoid(ba[:, :LANES])
    if n_valid < tc:
        beta = jnp.where(lax.broadcasted_iota(jnp.int32, beta.shape, 0) < n_valid, beta, 0.0)
    g = -jnp.exp(alog_ref[...]) * _softplus(ba[:, LANES:] + dtb_ref[...])

    ri = lax.broadcasted_iota(jnp.int32, (tc, tc), 0)
    ci = lax.broadcasted_iota(jnp.int32, (tc, tc), 1)
    incl = ci <= ri
    if sub > 1:
        incl = incl & ((ri // chunk) == (ci // chunk))
    strict = incl & (ci < ri)
    gc = _dot_f32(incl.astype(F32), g)
    gc_t = gc.T
    exp_gc = jnp.exp(gc)
    chunk_rows = [slice(s * chunk, (s + 1) * chunk) for s in range(sub)]
    gc_last = [gc[s * chunk + last:s * chunk + last + 1, :] for s in range(sub)]
    exp_last = [jnp.exp(v) for v in gc_last]
    exp_rest = jnp.exp(jnp.concatenate([jnp.broadcast_to(v, (chunk, LANES)) for v in gc_last], axis=0) - gc)
    hs = range(heads)

    ks = [k_s[h] for h in hs]
    bs = [beta[:, h:h + 1] for h in hs]
    kbs = [k * b for k, b in zip(ks, bs)]
    decays = [jnp.exp(jnp.where(incl, gc[:, h:h + 1] - gc_t[h:h + 1, :], -jnp.inf)) for h in hs]
    a_mats = [_dot_nt(kb, k) * jnp.where(strict, d, 0.0) for kb, k, d in zip(kbs, ks, decays)]
    rs = _unit_lower_inverse_minus_eye(a_mats, ri, ci, chunk)
    rhss = [jnp.concatenate([v_s[h] * bs[h], kbs[h] * exp_gc[:, h:h + 1]], axis=1) for h in hs]
    sols = [rhs + _dot3(_split(r), _split(rhs)) for r, rhs in zip(rs, rhss)]
    qs = [q_s[h] for h in hs]
    attns = [_dot_nt(q, k) * d for q, k, d in zip(qs, ks, decays)]
    qgs = [qs[h] * exp_gc[:, h:h + 1] for h in hs]
    kdecs = [ks[h] * exp_rest[:, h:h + 1] for h in hs]
    states = [s_s[h] for h in hs]
    v_new = [[] for _ in hs]
    o_inter = [[] for _ in hs]
    for s, r in enumerate(chunk_rows):
        for h in hs:
            vn = sols[h][r, :GDN_DV] - _dot(sols[h][r, GDN_DV:], states[h])
            o_inter[h].append(_dot(qgs[h][r], states[h]))
            states[h] = states[h] * exp_last[s][:, h:h + 1] + _dot_tn(kdecs[h][r], vn)
            v_new[h].append(vn)
    normw = normw_ref[...]
    for h in hs:
        s_s[h] = states[h]
        o = jnp.concatenate(o_inter[h], axis=0) + _dot(attns[h], jnp.concatenate(v_new[h], axis=0))
        cols = slice(h * GDN_DV, (h + 1) * GDN_DV)
        o_ref[:, cols] = (_rms(o, normw) * _silu(z_ref[:, cols])).astype(o_ref.dtype)

    @pl.when(t == pl.num_programs(1) - 1)
    def _():
        sout_ref[...] = s_s[...]


def _gdn_seq(qkv, z, ba, conv0, s0, convw, alog, dtb, normw, *, nseq, seqlen, tc, chunk, n_valid):
    nt = seqlen // tc
    row = lambda b, t: (b * nt + t, 0)
    seq3 = lambda b, t: (b, 0, 0)
    seq4 = lambda b, t: (b, 0, 0, 0)
    m = nseq * seqlen
    return pl.pallas_call(
        functools.partial(_gdn_seq_kernel, tc=tc, chunk=chunk, n_valid=n_valid),
        grid=(nseq, nt),
        in_specs=[pl.BlockSpec((tc, GDN_QKV), row), pl.BlockSpec((tc, GDN_Z), row),
                  pl.BlockSpec((tc, 2 * LANES), row),
                  pl.BlockSpec((None, SUBLANES, GDN_QKV), seq3),
                  pl.BlockSpec((None, GDN_HEADS, GDN_DK, GDN_DV), seq4),
                  _const_spec((SUBLANES, GDN_QKV)), _const_spec((1, LANES)), _const_spec((1, LANES)),
                  _const_spec((1, GDN_DV))],
        out_specs=[pl.BlockSpec((tc, GDN_Z), row),
                   pl.BlockSpec((None, GDN_HEADS, GDN_DK, GDN_DV), seq4),
                   pl.BlockSpec((None, SUBLANES, GDN_QKV), seq3)],
        out_shape=[jax.ShapeDtypeStruct((m, GDN_Z), BF16),
                   jax.ShapeDtypeStruct((nseq, GDN_HEADS, GDN_DK, GDN_DV), F32),
                   jax.ShapeDtypeStruct((nseq, SUBLANES, GDN_QKV), F32)],
        scratch_shapes=[pltpu.VMEM((tc + SUBLANES, GDN_QKV), F32),
                        pltpu.VMEM((GDN_HEADS, tc, GDN_DK), F32),
                        pltpu.VMEM((GDN_HEADS, tc, GDN_DK), F32),
                        pltpu.VMEM((GDN_HEADS, tc, GDN_DV), F32),
                        pltpu.VMEM((GDN_HEADS, GDN_DK, GDN_DV), F32)],
        compiler_params=_compiler_params(2),
        name="gdn_seq",
    )(qkv, z, ba, conv0, s0, convw, alog, dtb, normw)


def _mlstm_seq_kernel(q_ref, k_ref, v_ref, og_ref, gates_ref, c0_ref, n0_ref, m0_ref, ibias_ref, fbias_ref,
                      h_ref, cout_ref, nout_ref, mout_ref,
                      cx_s, m_s, *, tc, chunk, n_valid):
    t = pl.program_id(1)
    heads = ML_HEADS
    sub = tc // chunk
    last = min(chunk, n_valid) - 1

    @pl.when(t == 0)
    def _():
        cx_s[:, :, :ML_DV] = c0_ref[...]
        cx_s[:, :, ML_DV:] = n0_ref[...]
        m_s[...] = m0_ref[...]

    gates = gates_ref[...]
    ig = gates[:, :LANES] + ibias_ref[...]
    lf = -_softplus(-(gates[:, LANES:] + fbias_ref[...]))

    ri = lax.broadcasted_iota(jnp.int32, (tc, tc), 0)
    ci = lax.broadcasted_iota(jnp.int32, (tc, tc), 1)
    incl = ci <= ri
    if sub > 1:
        incl = incl & ((ri // chunk) == (ci // chunk))
    mask = incl & (ci < n_valid) if n_valid < chunk else incl
    col_rows = lax.broadcasted_iota(jnp.int32, (chunk, 1), 0)
    ones_col = (lax.broadcasted_iota(jnp.int32, (tc, LANES), 1) == 0).astype(F32)
    lane_row = lax.broadcasted_iota(jnp.int32, (1, LANES), 1)
    chunk_rows = [slice(s * chunk, (s + 1) * chunk) for s in range(sub)]
    last_rows = [slice(s * chunk + last, s * chunk + last + 1) for s in range(sub)]
    hs = range(heads)

    bcum = _dot_f32(incl.astype(F32), lf)
    bcum_t = bcum.T
    ig_t = ig.T
    dmats = [jnp.where(mask, bcum[:, h:h + 1] - bcum_t[h:h + 1, :] + ig_t[h:h + 1, :], -jnp.inf) for h in hs]
    dmaxs = [jnp.max(d, axis=-1, keepdims=True) for d in dmats]
    m_row = m_s[0:1, :]
    m_in = [m_row[:, h:h + 1] for h in hs]
    inters = [[] for _ in hs]
    m_ts = [[] for _ in hs]
    m_news = [[] for _ in hs]
    for r in chunk_rows:
        for h in hs:
            inter = bcum[r, h:h + 1] + m_in[h]
            m_t = jnp.maximum(inter, dmaxs[h][r])
            m_in[h] = m_t[last:last + 1, :]
            inters[h].append(inter)
            m_ts[h].append(m_t)
            m_news[h].append(m_in[h])
    inters = [jnp.concatenate(v, axis=0) for v in inters]
    m_ts = [jnp.concatenate(v, axis=0) for v in m_ts]
    w_intras = [jnp.exp(d - m) for d, m in zip(dmats, m_ts)]
    w_inters = [jnp.exp(i - m) for i, m in zip(inters, m_ts)]
    qs = [q_ref[:, h * ML_DK:(h + 1) * ML_DK] * ML_DK ** -0.5 for h in hs]
    ks = [k_ref[:, h * ML_DK:(h + 1) * ML_DK] for h in hs]
    vxs = [jnp.concatenate([v_ref[:, h * ML_DV:(h + 1) * ML_DV], ones_col], axis=1) for h in hs]
    scores = [_dot_nt(q, k) * w for q, k, w in zip(qs, ks, w_intras)]
    intra = [_dot(s, vx) for s, vx in zip(scores, vxs)]
    cxs = [cx_s[h] for h in hs]
    inter_num = [[] for _ in hs]
    for s, r in enumerate(chunk_rows):
        for h in hs:
            inter_num[h].append(_dot(qs[h][r], cxs[h]))
            m_new = m_news[h][s]
            w_state = jnp.exp(jnp.where(col_rows <= last, bcum[last_rows[s], h:h + 1] - bcum[r, h:h + 1]
                                        + ig[r, h:h + 1], -jnp.inf) - m_new)
            carry_decay = jnp.exp(inters[h][last_rows[s], :] - m_new)
            cxs[h] = carry_decay * cxs[h] + _dot_tn(ks[h][r] * w_state, vxs[h][r])
    m_next = m_row
    for h in hs:
        cx_s[h] = cxs[h]
        num = w_inters[h] * jnp.concatenate(inter_num[h], axis=0) + intra[h]
        den = num[:, ML_DV:ML_DV + 1]
        hh = num[:, :ML_DV] / jnp.maximum(jnp.abs(den), jnp.exp(-m_ts[h]))
        vcols = slice(h * ML_DV, (h + 1) * ML_DV)
        h_ref[:, vcols] = (hh * jax.nn.sigmoid(og_ref[:, vcols])).astype(h_ref.dtype)
        m_next = jnp.where(lane_row == h, m_in[h], m_next)
    m_s[0:1, :] = m_next

    @pl.when(t == pl.num_programs(1) - 1)
    def _():
        cout_ref[...] = cx_s[:, :, :ML_DV]
        nout_ref[...] = cx_s[:, :, ML_DV:]
        mout_ref[...] = m_s[...]


def _mlstm_seq(q, k, v, og, gates, c0, n0, m0, ibias, fbias, *, nseq, seqlen, tc, chunk, n_valid):
    nt = seqlen // tc
    row = lambda b, t: (b * nt + t, 0)
    seq3 = lambda b, t: (b, 0, 0)
    seq4 = lambda b, t: (b, 0, 0, 0)
    m = nseq * seqlen
    return pl.pallas_call(
        functools.partial(_mlstm_seq_kernel, tc=tc, chunk=chunk, n_valid=n_valid),
        grid=(nseq, nt),
        in_specs=[pl.BlockSpec((tc, ML_QK), row), pl.BlockSpec((tc, ML_QK), row),
                  pl.BlockSpec((tc, ML_V), row), pl.BlockSpec((tc, ML_V), row),
                  pl.BlockSpec((tc, 2 * LANES), row),
                  pl.BlockSpec((None, ML_HEADS, ML_DK, ML_DV), seq4),
                  pl.BlockSpec((None, ML_HEADS, ML_DK, LANES), seq4),
                  pl.BlockSpec((None, SUBLANES, LANES), seq3),
                  _const_spec((1, LANES)), _const_spec((1, LANES))],
        out_specs=[pl.BlockSpec((tc, ML_V), row),
                   pl.BlockSpec((None, ML_HEADS, ML_DK, ML_DV), seq4),
                   pl.BlockSpec((None, ML_HEADS, ML_DK, LANES), seq4),
                   pl.BlockSpec((None, SUBLANES, LANES), seq3)],
        out_shape=[jax.ShapeDtypeStruct((m, ML_V), BF16),
                   jax.ShapeDtypeStruct((nseq, ML_HEADS, ML_DK, ML_DV), F32),
                   jax.ShapeDtypeStruct((nseq, ML_HEADS, ML_DK, LANES), F32),
                   jax.ShapeDtypeStruct((nseq, SUBLANES, LANES), F32)],
        scratch_shapes=[pltpu.VMEM((ML_HEADS, ML_DK, ML_DV + LANES), F32),
                        pltpu.VMEM((SUBLANES, LANES), F32)],
        compiler_params=_compiler_params(2),
        name="mlstm_seq",
    )(q, k, v, og, gates, c0, n0, m0, ibias, fbias)


def _mla_proj_kernel(x_ref, g_ref, cos_ref, sin_ref, win_ref, qnorm_ref, kvnorm_ref, wuq_ref, wuqsw_ref, wkv_ref,
                     *o_refs, decode):
    xn = _rms(x_ref[...], g_ref[...]).astype(BF16)
    proj = jnp.dot(xn, win_ref[...], preferred_element_type=F32)
    cq = _rms(proj[:, :MLA_Q_RANK], qnorm_ref[...]).astype(BF16)
    ckv = _rms(proj[:, MLA_Q_RANK:MLA_Q_RANK + MLA_KV_RANK], kvnorm_ref[...])
    base = MLA_Q_RANK + MLA_KV_RANK
    cos = cos_ref[...]
    sin = sin_ref[...]
    kr = proj[:, base:base + LANES] * cos + proj[:, base + LANES:base + 2 * LANES] * sin
    q_main = jnp.dot(cq, wuq_ref[...], preferred_element_type=F32)
    q_swap = jnp.dot(cq, wuqsw_ref[...], preferred_element_type=F32)
    if decode:
        qlat_ref, qrope_ref, ckv_ref, kr_ref = o_refs
    else:
        qcat_ref, kcat_ref, vt_ref, ckv_ref, kr_ref = o_refs
        kv = jnp.dot(ckv.astype(BF16), wkv_ref[...], preferred_element_type=F32)
    ckv_ref[...] = ckv
    kr_ref[...] = kr[:, :MLA_ROPE]
    for h in range(MLA_HEADS):
        lo = h * 2 * LANES
        q_nope = q_main[:, lo:lo + LANES] * MLA_SCALE
        q_rope = (q_main[:, lo + LANES:lo + 2 * LANES] * cos + q_swap[:, h * LANES:(h + 1) * LANES] * sin) * MLA_SCALE
        if decode:
            qlat_ref[:, lo:lo + 2 * LANES] = _dot(q_nope, wkv_ref[h]).astype(BF16)
            qrope_ref[:, h * LANES:(h + 1) * LANES] = q_rope.astype(BF16)
        else:
            qcat_ref[:, lo:lo + LANES] = q_nope.astype(BF16)
            qcat_ref[:, lo + LANES:lo + 2 * LANES] = q_rope.astype(BF16)
            kcat_ref[:, lo:lo + LANES] = kv[:, lo:lo + LANES].astype(BF16)
            kcat_ref[:, lo + LANES:lo + 2 * LANES] = kr.astype(BF16)
            key_tile = vt_ref.shape[-1]
            for s in range(vt_ref.shape[0]):
                v_tile = kv[s * key_tile:(s + 1) * key_tile, lo + LANES:lo + 2 * LANES]
                vt_ref[s, h * MLA_V:(h + 1) * MLA_V, :] = v_tile.T.astype(BF16)


def _mla_proj(x, g, cos, sin, win, qnorm, kvnorm, wuq, wuqsw, wkv, *, tm, decode, key_tile=None):
    m = x.shape[0]
    n_pos_blocks = cos.shape[0] // tm
    row = lambda i: (i, 0)
    pos = lambda i: (i % n_pos_blocks, 0)
    wide = MLA_HEADS * 2 * LANES
    if decode:
        widths = (wide, MLA_HEADS * LANES, MLA_KV_RANK, MLA_ROPE)
        dtypes = (BF16, BF16, F32, F32)
    else:
        widths = (wide, wide, MLA_KV_RANK, MLA_ROPE)
        dtypes = (BF16, BF16, F32, F32)
    out_specs = [pl.BlockSpec((tm, n), row) for n in widths]
    out_shape = [jax.ShapeDtypeStruct((m, n), dt) for n, dt in zip(widths, dtypes)]
    if not decode:
        out_specs.insert(2, pl.BlockSpec((tm // key_tile, MLA_HEADS * MLA_V, key_tile), lambda i: (i, 0, 0)))
        out_shape.insert(2, jax.ShapeDtypeStruct((m // key_tile, MLA_HEADS * MLA_V, key_tile), BF16))
    return pl.pallas_call(
        functools.partial(_mla_proj_kernel, decode=decode),
        grid=(m // tm,),
        in_specs=[pl.BlockSpec((tm, D_MODEL), row), _const_spec((1, D_MODEL)),
                  pl.BlockSpec((tm, LANES), pos), pl.BlockSpec((tm, LANES), pos),
                  _const_spec(win.shape), _const_spec((1, MLA_Q_RANK)), _const_spec((1, MLA_KV_RANK)),
                  _const_spec(wuq.shape), _const_spec(wuqsw.shape), _const_spec(wkv.shape)],
        out_specs=out_specs,
        out_shape=out_shape,
        compiler_params=_compiler_params(1),
        name="mla_proj_decode" if decode else "mla_proj",
    )(x, g, cos, sin, win, qnorm, kvnorm, wuq, wuqsw, wkv)


def _mla_attn_kernel(q_ref, k_ref, vt_ref, o_ref, m_s, l_s, acc_s, *, tq):
    qi = pl.program_id(1)
    heads = MLA_HEADS
    wide = 2 * LANES
    m_s[...] = jnp.full(m_s.shape, -jnp.inf, F32)
    l_s[...] = jnp.zeros(l_s.shape, F32)
    acc_s[...] = jnp.zeros(acc_s.shape, F32)
    key_i = lax.broadcasted_iota(jnp.int32, (tq, tq), 0)
    query_i = lax.broadcasted_iota(jnp.int32, (tq, tq), 1)

    def key_tile(j, diagonal):
        rows = pl.ds(pl.multiple_of(j * tq, tq), tq)
        for g0 in range(0, heads, ATTN_HEAD_GROUP):
            hs = range(g0, g0 + ATTN_HEAD_GROUP)
            ss = [lax.dot_general(k_ref[rows, h * wide:(h + 1) * wide], q_ref[:, h * wide:(h + 1) * wide],
                                  (((1,), (1,)), ((), ())), preferred_element_type=F32) for h in hs]
            if diagonal:
                ss = [jnp.where(key_i <= query_i, s, -jnp.inf) for s in ss]
            m_old = [m_s[h] for h in hs]
            m_new = [jnp.maximum(m, jnp.max(s, axis=0, keepdims=True)) for m, s in zip(m_old, ss)]
            ps = [jnp.exp(s - m) for s, m in zip(ss, m_new)]
            alphas = [jnp.exp(mo - mn) for mo, mn in zip(m_old, m_new)]
            pvs = [jnp.dot(vt_ref[j, h * MLA_V:(h + 1) * MLA_V, :], p.astype(BF16), preferred_element_type=F32)
                   for p, h in zip(ps, hs)]
            for i, h in enumerate(hs):
                l_s[h] = alphas[i] * l_s[h] + jnp.sum(ps[i], axis=0, keepdims=True)
                acc_s[h] = alphas[i] * acc_s[h] + pvs[i]
                m_s[h] = m_new[i]

    def body(j, carry):
        key_tile(j, False)
        return carry

    lax.fori_loop(0, qi, body, 0)
    key_tile(qi, True)
    for h in range(heads):
        o_ref[:, h * MLA_V:(h + 1) * MLA_V] = (acc_s[h] / l_s[h]).T.astype(o_ref.dtype)


def _mla_attn(qcat, kcat, vt, *, nseq, seqlen, tq):
    nq = seqlen // tq
    m = nseq * seqlen
    return pl.pallas_call(
        functools.partial(_mla_attn_kernel, tq=tq),
        grid=(nseq, nq),
        in_specs=[pl.BlockSpec((tq, MLA_HEADS * 2 * LANES), lambda b, i: (b * nq + i, 0)),
                  pl.BlockSpec((seqlen, MLA_HEADS * 2 * LANES), lambda b, i: (b, 0)),
                  pl.BlockSpec((nq, MLA_HEADS * MLA_V, tq), lambda b, i: (b, 0, 0))],
        out_specs=pl.BlockSpec((tq, MLA_HEADS * MLA_V), lambda b, i: (b * nq + i, 0)),
        out_shape=jax.ShapeDtypeStruct((m, MLA_HEADS * MLA_V), BF16),
        scratch_shapes=[pltpu.VMEM((MLA_HEADS, 1, tq), F32), pltpu.VMEM((MLA_HEADS, 1, tq), F32),
                        pltpu.VMEM((MLA_HEADS, MLA_V, tq), F32)],
        compiler_params=_compiler_params(2),
        name="mla_attn",
    )(qcat, kcat, vt)


def _mla_decode_kernel(pt_ref, qlat_ref, qrope_ref, ckv_ref, kr_ref, wuv_ref, *rest, n_pages):
    lat_refs = rest[:n_pages]
    rope_refs = rest[n_pages:2 * n_pages]
    o_ref = rest[2 * n_pages]
    ql_s, qr_s, m_s, l_s, acc_s = rest[2 * n_pages + 1:]
    g = pl.program_id(1)
    heads = MLA_HEADS
    tok = DEC_PAD

    @pl.when(g == 0)
    def _():
        for h in range(heads):
            ql_s[h * tok:(h + 1) * tok, :] = qlat_ref[:, h * 2 * LANES:(h + 1) * 2 * LANES]
            qr_s[h * tok:(h + 1) * tok, :] = qrope_ref[:, h * LANES:(h + 1) * LANES]
        m_s[...] = jnp.full(m_s.shape, -jnp.inf, F32)
        l_s[...] = jnp.zeros(l_s.shape, F32)
        acc_s[...] = jnp.zeros(acc_s.shape, F32)

    ql = ql_s[...]
    qr = qr_s[:, :MLA_ROPE]

    def update(s, values):
        m_old = m_s[...]
        m_new = jnp.maximum(m_old, jnp.max(s, axis=-1, keepdims=True))
        p = jnp.exp(s - m_new)
        alpha = jnp.exp(m_old - m_new)
        l_s[...] = alpha * l_s[...] + jnp.sum(p, axis=-1, keepdims=True)
        pb = p.astype(BF16)
        acc = alpha * acc_s[...]
        for cols, val in values:
            acc = acc + jnp.dot(pb[:, cols], val, preferred_element_type=F32)
        acc_s[...] = acc
        m_s[...] = m_new

    lats = [r[...].astype(BF16) for r in lat_refs]
    scores = [_dot_nt(ql, lat) + _dot(qr, r[...]) for lat, r in zip(lats, rope_refs)]
    update(jnp.concatenate(scores, axis=1),
           [(slice(i * PAGE_SIZE, (i + 1) * PAGE_SIZE), lat) for i, lat in enumerate(lats)])

    @pl.when(g == pl.num_programs(1) - 1)
    def _():
        ckv = ckv_ref[...].astype(BF16)
        s = _dot_nt(ql, ckv) + _dot_nt(qr, kr_ref[...])
        qt = lax.broadcasted_iota(jnp.int32, s.shape, 0) % tok
        kt = lax.broadcasted_iota(jnp.int32, s.shape, 1)
        update(jnp.where(kt <= qt, s, -jnp.inf), [(slice(0, tok), ckv)])
        o_lat = acc_s[...] / l_s[...]
        for h in range(heads):
            o_ref[:, h * MLA_V:(h + 1) * MLA_V] = _dot(o_lat[h * tok:(h + 1) * tok, :], wuv_ref[h]).astype(o_ref.dtype)


def _mla_decode(page_table, qlat, qrope, ckv, kr, wuv, lat_pages, rope_pages, *, nseq):
    n_pages_seq = page_table.shape[1]
    n = PAGES_PER_STEP
    groups = n_pages_seq // n
    seq = lambda b, g, pt: (b, 0)

    def page_map(i):
        return lambda b, g, pt: (pt[b * n_pages_seq + g * n + i], 0, 0)

    rows = MLA_HEADS * DEC_PAD
    grid_spec = pltpu.PrefetchScalarGridSpec(
        num_scalar_prefetch=1,
        grid=(nseq, groups),
        in_specs=[pl.BlockSpec((DEC_PAD, MLA_HEADS * 2 * LANES), seq),
                  pl.BlockSpec((DEC_PAD, MLA_HEADS * LANES), seq),
                  pl.BlockSpec((DEC_PAD, MLA_KV_RANK), seq),
                  pl.BlockSpec((DEC_PAD, MLA_ROPE), seq),
                  pl.BlockSpec(wuv.shape, lambda b, g, pt: (0, 0, 0))]
                 + [pl.BlockSpec((None, PAGE_SIZE, MLA_KV_RANK), page_map(i)) for i in range(n)]
                 + [pl.BlockSpec((None, MLA_ROPE, PAGE_SIZE), page_map(i)) for i in range(n)],
        out_specs=pl.BlockSpec((DEC_PAD, MLA_HEADS * MLA_V), seq),
        scratch_shapes=[pltpu.VMEM((rows, MLA_KV_RANK), BF16), pltpu.VMEM((rows, LANES), BF16),
                        pltpu.VMEM((rows, 1), F32), pltpu.VMEM((rows, 1), F32),
                        pltpu.VMEM((rows, MLA_KV_RANK), F32)],
    )
    return pl.pallas_call(
        functools.partial(_mla_decode_kernel, n_pages=n),
        grid_spec=grid_spec,
        out_shape=jax.ShapeDtypeStruct((nseq * DEC_PAD, MLA_HEADS * MLA_V), BF16),
        compiler_params=_compiler_params(2),
        name="mla_decode",
    )(page_table.reshape(-1), qlat, qrope, ckv, kr, wuv, *([lat_pages] * n), *([rope_pages] * n))


def _row(v, width=None):
    v = v.astype(F32).reshape(1, -1)
    if width is not None and v.shape[1] < width:
        v = jnp.pad(v, ((0, 0), (0, width - v.shape[1])))
    return v


def _pad_cols(w, width):
    return jnp.pad(w, ((0, 0), (0, width - w.shape[1])))


def _prep_ffn(w_up, w_down):
    return w_up.astype(BF16), w_down.reshape(N_FF_CHUNKS, FF_CHUNK, D_MODEL).astype(BF16)


def _prep_gdn(w_in, conv_w, a_log, dt_bias, norm_w, w_out):
    main = GDN_QKV + GDN_Z
    w = jnp.concatenate([w_in[:, :main], _pad_cols(w_in[:, main:main + GDN_HEADS], LANES),
                         _pad_cols(w_in[:, main + GDN_HEADS:], LANES)], axis=1).astype(BF16)
    convw = jnp.pad(conv_w.astype(F32), ((0, SUBLANES - GDN_CONV), (0, 0)))
    return w, convw, _row(a_log, LANES), _row(dt_bias, LANES), _row(norm_w), w_out.astype(BF16)


def _prep_mlstm(w_in, i_bias, f_bias, w_out):
    main = 2 * ML_QK + 2 * ML_V
    w = jnp.concatenate([w_in[:, :main], _pad_cols(w_in[:, main:main + ML_HEADS], LANES),
                         _pad_cols(w_in[:, main + ML_HEADS:], LANES)], axis=1).astype(BF16)
    return w, _row(i_bias, LANES), _row(f_bias, LANES), w_out.astype(BF16)


def _swap_halves(w):
    half = w.shape[-1] // 2
    return jnp.concatenate([w[..., half:], w[..., :half]], axis=-1)


def _prep_mla(w_in, q_norm, kv_norm, w_uq, w_ukv, w_out):
    base = MLA_Q_RANK + MLA_KV_RANK
    kr = w_in[:, base:]
    win = jnp.concatenate([w_in[:, :base], _pad_cols(kr, LANES), _pad_cols(_swap_halves(kr), LANES)],
                          axis=1).astype(BF16)
    wq = w_uq.reshape(MLA_Q_RANK, MLA_HEADS, MLA_NOPE + MLA_ROPE)
    nope, rope = wq[..., :MLA_NOPE], wq[..., MLA_NOPE:]
    zeros = jnp.zeros((MLA_Q_RANK, MLA_HEADS, LANES - MLA_ROPE), w_uq.dtype)
    wuq = jnp.concatenate([nope, rope, zeros], axis=-1).reshape(MLA_Q_RANK, -1).astype(BF16)
    wuqsw = jnp.concatenate([_swap_halves(rope), zeros], axis=-1).reshape(MLA_Q_RANK, -1).astype(BF16)
    wkv3 = w_ukv.reshape(MLA_KV_RANK, MLA_HEADS, MLA_NOPE + MLA_V)
    wuk_t = wkv3[..., :MLA_NOPE].transpose(1, 2, 0).astype(BF16)
    wuv = wkv3[..., MLA_NOPE:].transpose(1, 0, 2).astype(BF16)
    return dict(win=win, qnorm=_row(q_norm), kvnorm=_row(kv_norm), wuq=wuq, wuqsw=wuqsw,
                wkv=w_ukv.astype(BF16), wuk_t=wuk_t, wuv=wuv, wout=w_out.astype(BF16))


def _rope_tables(pos):
    half = MLA_ROPE // 2
    inv = ROPE_THETA ** (-jnp.arange(half, dtype=F32) / half)
    ang = pos.astype(F32)[:, None] * inv
    cos, sin = jnp.cos(ang), jnp.sin(ang)
    pad = jnp.zeros((pos.shape[0], LANES - MLA_ROPE), F32)
    return jnp.concatenate([cos, cos, pad], axis=1), jnp.concatenate([-sin, sin, pad], axis=1)


def _tiles(nseq, seqlen):
    m = nseq * seqlen
    return dict(ffn=min(512, m), proj=min(256, m), mla_proj=min(512, seqlen), attn_q=min(256, seqlen),
                gdn=min(2 * GDN_CHUNK, seqlen), mlstm=min(2 * ML_CHUNK, seqlen))


def _trunk(x, *, nseq, seqlen, n_tokens, decode, states, page_table, weights):
    gdn_S, gdn_conv, mla_lat, mla_rope, ml_C, ml_n, ml_m = states
    chunk_g = min(GDN_CHUNK, seqlen)
    chunk_m = min(ML_CHUNK, seqlen)
    tiles = _tiles(nseq, seqlen)
    tm = tiles["ffn"]
    new = ([], [], [], [], [], [], [])
    counts = [0, 0, 0]
    for layer in range(4):
        kind = layer % 3
        j = counts[kind]
        counts[kind] += 1
        ng = weights["gains"][layer]
        wup, wdn = weights["ffn"][layer][0]
        x = _ffn_half(x, ng[0], ng[1], wup, wdn, tm)
        if kind == 0:
            w, convw, alog, dtb, normw, wout = weights["gdn"][j]
            qkv, z, ba = _norm_matmul(x, ng[2], w, (GDN_QKV, GDN_Z, 2 * LANES), tiles["proj"])
            o, s_new, conv_new = _gdn_seq(qkv, z, ba, gdn_conv[j], gdn_S[j], convw, alog, dtb, normw,
                                          nseq=nseq, seqlen=seqlen, tc=tiles["gdn"], chunk=chunk_g,
                                          n_valid=min(tiles["gdn"], n_tokens))
            new[0].append(s_new)
            new[1].append(conv_new[:, SUBLANES - (GDN_CONV - 1):, :])
        elif kind == 1:
            p = weights["mla"][j]
            cos, sin = weights["rope"]
            if decode:
                qlat, qrope, ckv, kr = _mla_proj(x, ng[2], cos, sin, p["win"], p["qnorm"], p["kvnorm"], p["wuq"],
                                                 p["wuqsw"], p["wuk_t"], tm=tiles["mla_proj"], decode=True)
                o = _mla_decode(page_table, qlat, qrope, ckv, kr, p["wuv"], mla_lat[j], mla_rope[j], nseq=nseq)
            else:
                qcat, kcat, vt, ckv, kr = _mla_proj(x, ng[2], cos, sin, p["win"], p["qnorm"], p["kvnorm"], p["wuq"],
                                                    p["wuqsw"], p["wkv"], tm=tiles["mla_proj"], decode=False,
                                                    key_tile=tiles["attn_q"])
                o = _mla_attn(qcat, kcat, vt, nseq=nseq, seqlen=seqlen, tq=tiles["attn_q"])
            wout = p["wout"]
            new[2].append(ckv)
            new[3].append(kr)
        else:
            w, ibias, fbias, wout = weights["mlstm"][j]
            q, k, v, og, gates = _norm_matmul(x, ng[2], w, (ML_QK, ML_QK, ML_V, ML_V, 2 * LANES), tiles["proj"])
            o, c_new, n_new, m_new = _mlstm_seq(q, k, v, og, gates, ml_C[j], ml_n[j], ml_m[j], ibias, fbias,
                                                nseq=nseq, seqlen=seqlen, tc=tiles["mlstm"], chunk=chunk_m,
                                                n_valid=min(tiles["mlstm"], n_tokens))
            new[4].append(c_new)
            new[5].append(n_new[..., 0])
            new[6].append(m_new[:, 0, :ML_HEADS])
        x = _out_proj(o, x, ng[3], wout, tm)
        wup, wdn = weights["ffn"][layer][1]
        x = _ffn_half(x, ng[4], ng[5], wup, wdn, tm)
    return x, [jnp.stack(s) for s in new]


def kernel(x_prompt, x_sample, state_gdn_S, state_gdn_conv, cache_mla_latent, cache_mla_rope, state_mlstm_C, state_mlstm_n, state_mlstm_m, page_table, norm_gains, w_ffn_up, w_ffn_down, gdn_w_in, gdn_conv_w, gdn_a_log, gdn_dt_bias, gdn_norm_w, gdn_w_out, mla_w_in, mla_q_norm, mla_kv_norm, mla_w_uq, mla_w_ukv, mla_w_out, mlstm_w_in, mlstm_i_bias, mlstm_f_bias, mlstm_w_out):
    nb, seq, _ = x_prompt.shape
    db, dseq, _ = x_sample.shape
    n_gdn, n_mla, n_ml = gdn_w_in.shape[0], mla_w_in.shape[0], mlstm_w_in.shape[0]
    past = page_table.shape[1] * PAGE_SIZE

    weights = dict(
        gains=[[_row(norm_gains[l, i]) for i in range(6)] for l in range(4)],
        ffn=[[_prep_ffn(w_ffn_up[l, i], w_ffn_down[l, i]) for i in range(2)] for l in range(4)],
        gdn=[_prep_gdn(gdn_w_in[j], gdn_conv_w[j], gdn_a_log[j], gdn_dt_bias[j], gdn_norm_w[j], gdn_w_out[j])
             for j in range(n_gdn)],
        mla=[_prep_mla(mla_w_in[j], mla_q_norm[j], mla_kv_norm[j], mla_w_uq[j], mla_w_ukv[j], mla_w_out[j])
             for j in range(n_mla)],
        mlstm=[_prep_mlstm(mlstm_w_in[j], mlstm_i_bias[j], mlstm_f_bias[j], mlstm_w_out[j]) for j in range(n_ml)],
    )

    zeros = lambda *s: jnp.zeros(s, F32)
    states_p = (zeros(n_gdn, nb, GDN_HEADS, GDN_DK, GDN_DV), zeros(n_gdn, nb, SUBLANES, GDN_QKV), None, None,
                zeros(n_ml, nb, ML_HEADS, ML_DK, ML_DV), zeros(n_ml, nb, ML_HEADS, ML_DK, LANES),
                zeros(n_ml, nb, SUBLANES, LANES))
    y_p, st_p = _trunk(x_prompt.reshape(nb * seq, D_MODEL), nseq=nb, seqlen=seq, n_tokens=seq, decode=False,
                       states=states_p, page_table=None,
                       weights=dict(weights, rope=_rope_tables(jnp.arange(seq))))

    pad_t = DEC_PAD - dseq
    x_s = jnp.pad(x_sample, ((0, 0), (0, pad_t), (0, 0))).reshape(db * DEC_PAD, D_MODEL)
    conv0 = jnp.pad(state_gdn_conv, ((0, 0), (0, 0), (SUBLANES - (GDN_CONV - 1), 0), (0, 0)))
    n0 = jnp.pad(state_mlstm_n[..., None], ((0, 0),) * 4 + ((0, LANES - 1),))
    m0 = jnp.pad(state_mlstm_m[:, :, None, :], ((0, 0), (0, 0), (0, SUBLANES - 1), (0, LANES - ML_HEADS)))
    states_s = (state_gdn_S, conv0,
                cache_mla_latent.reshape((n_mla, -1) + cache_mla_latent.shape[2:]),
                jnp.swapaxes(cache_mla_rope.reshape((n_mla, -1) + cache_mla_rope.shape[2:]), -1, -2),
                state_mlstm_C, n0, m0)
    pos_s = jnp.tile(past + jnp.arange(DEC_PAD), db)
    y_s, st_s = _trunk(x_s, nseq=db, seqlen=DEC_PAD, n_tokens=dseq, decode=True, states=states_s,
                       page_table=page_table, weights=dict(weights, rope=_rope_tables(pos_s)))

    gdn_S_p, gdn_conv_p, lat_p, rope_p, ml_C_p, ml_n_p, ml_m_p = st_p
    gdn_S_s, gdn_conv_s, lat_s, rope_s, ml_C_s, ml_n_s, ml_m_s = st_s
    unpad = lambda a: a.reshape(a.shape[0], db, DEC_PAD, a.shape[-1])[:, :, :dseq]
    return (y_p.reshape(nb, seq, D_MODEL), unpad(y_s[None])[0],
            gdn_S_p, gdn_S_s, gdn_conv_p, gdn_conv_s,
            lat_p.reshape(n_mla, -1, PAGE_SIZE, MLA_KV_RANK), unpad(lat_s),
            rope_p.reshape(n_mla, -1, PAGE_SIZE, MLA_ROPE), unpad(rope_s),
            ml_C_p, ml_C_s, ml_n_p, ml_n_s, ml_m_p, ml_m_s)
```

```python
import functools
import math

import jax
import jax.numpy as jnp
from jax import lax
from jax.experimental import pallas as pl
from jax.experimental.pallas import tpu as pltpu

F32 = jnp.float32
BF16 = jnp.bfloat16
HIGHEST = lax.Precision.HIGHEST

D_MODEL = 1024
PAGE_SIZE = 128
EPS = 1e-6

GDN_HEADS = 8
GDN_DK = 128
GDN_DV = 128
GDN_CONV = 4
GDN_CHUNK = 64
GDN_QKV = GDN_HEADS * (2 * GDN_DK + GDN_DV)
GDN_Z = GDN_HEADS * GDN_DV

MLA_HEADS = 8
MLA_NOPE = 128
MLA_ROPE = 64
MLA_V = 128
MLA_Q_RANK = 384
MLA_KV_RANK = 256
MLA_SCALE = (MLA_NOPE + MLA_ROPE) ** -0.5
ROPE_THETA = 10000.0

ML_HEADS = 4
ML_DK = 128
ML_DV = 256
ML_CHUNK = 64
ML_QK = ML_HEADS * ML_DK
ML_V = ML_HEADS * ML_DV

D_FF = 2816
FF_CHUNK = 256
N_FF_CHUNKS = D_FF // FF_CHUNK

LANES = 128
SUBLANES = 8
DEC_PAD = SUBLANES
VMEM_LIMIT = 56 * 1024 * 1024
PAGES_PER_STEP = 16
DECODE_CHAINS = 4
ATTN_HEAD_GROUP = 4
GDN_SEQ_GROUP = 2
SOLVE_REFINEMENTS = 2


def _rms(x, g):
    return x * lax.rsqrt(jnp.mean(x * x, axis=-1, keepdims=True) + EPS) * g


def _silu(x):
    return x * jax.nn.sigmoid(x)


def _softplus(x):
    return jnp.maximum(x, 0.0) + jnp.log1p(jnp.exp(-jnp.abs(x)))


def _dot(a, b):
    return jnp.dot(a.astype(BF16), b.astype(BF16), preferred_element_type=F32)


def _dot_nt(a, b):
    return lax.dot_general(a.astype(BF16), b.astype(BF16), (((1,), (1,)), ((), ())),
                           preferred_element_type=F32)


def _dot_tn(a, b):
    return lax.dot_general(a.astype(BF16), b.astype(BF16), (((0,), (0,)), ((), ())),
                           preferred_element_type=F32)


def _dot_f32(a, b):
    return jnp.dot(a, b, precision=HIGHEST, preferred_element_type=F32)


def _split(x):
    hi = x.astype(BF16)
    return hi, (x - hi.astype(F32)).astype(BF16)


def _dot3(a, b):
    (a_hi, a_lo), (b_hi, b_lo) = a, b
    if a_hi.shape[1] % LANES == 0:
        return jnp.dot(jnp.concatenate([a_hi, a_lo, a_hi], axis=1), jnp.concatenate([b_hi, b_hi, b_lo], axis=0),
                       preferred_element_type=F32)
    m = a_hi.shape[0]
    r = jnp.dot(jnp.concatenate([a_hi, a_lo], axis=0), b_hi, preferred_element_type=F32)
    return r[:m] + r[m:] + jnp.dot(a_hi, b_lo, preferred_element_type=F32)


def _unit_lower_inverse_minus_eye(a_mats, ri, ci, chunk):
    rs = None
    s = 1
    while s < chunk:
        shift = s.bit_length() - 1
        off = ((ri >> (shift + 1)) == (ci >> (shift + 1))) & ((ri >> shift) != (ci >> shift))
        a_offs = [jnp.where(off, a, 0.0) for a in a_mats]
        if rs is None:
            rs = [-a for a in a_offs]
        else:
            bs = [a + _dot(r, a) for a, r in zip(a_offs, rs)]
            rs = [r - b - _dot(b, r) for r, b in zip(rs, bs)]
        s *= 2
    return rs


def _unit_lower_solve(a_mats, rhss, ri, ci, chunk):
    rs = _unit_lower_inverse_minus_eye(a_mats, ri, ci, chunk)
    a_sp = [_split(a) for a in a_mats]
    xs = [rhs + _dot(r, rhs) for r, rhs in zip(rs, rhss)]
    for _ in range(SOLVE_REFINEMENTS):
        resid = [rhs - x - _dot3(a, _split(x)) for rhs, x, a in zip(rhss, xs, a_sp)]
        xs = [x + e + _dot(r, e) for x, e, r in zip(xs, resid, rs)]
    return xs


def _compiler_params(n_axes):
    return pltpu.CompilerParams(dimension_semantics=("arbitrary",) * n_axes,
                                vmem_limit_bytes=VMEM_LIMIT)


def _const_spec(shape):
    nd = len(shape)
    return pl.BlockSpec(shape, lambda *_: (0,) * nd)


def _ffn_kernel(x_ref, gpre_ref, gpost_ref, wup_ref, wdn_ref, o_ref):
    x = x_ref[...]
    xn = _rms(x, gpre_ref[...]).astype(BF16)
    acc = jnp.zeros(x.shape, F32)
    for c in range(N_FF_CHUNKS):
        lo = c * FF_CHUNK
        gate = jnp.dot(xn, wup_ref[:, lo:lo + FF_CHUNK], preferred_element_type=F32)
        up = jnp.dot(xn, wup_ref[:, D_FF + lo:D_FF + lo + FF_CHUNK], preferred_element_type=F32)
        act = (_silu(gate) * up).astype(BF16)
        acc = acc + jnp.dot(act, wdn_ref[c], preferred_element_type=F32)
    o_ref[...] = x + 0.5 * _rms(acc, gpost_ref[...])


def _ffn_half(x, gpre, gpost, wup, wdn, layer, half, tm):
    m = x.shape[0]
    return pl.pallas_call(
        _ffn_kernel,
        grid=(m // tm,),
        in_specs=[pl.BlockSpec((tm, D_MODEL), lambda i: (i, 0)),
                  _const_spec((1, D_MODEL)), _const_spec((1, D_MODEL)),
                  pl.BlockSpec((None, None) + wup.shape[2:], lambda i: (layer, half, 0, 0)),
                  pl.BlockSpec((None, None) + wdn.shape[2:], lambda i: (layer, half, 0, 0, 0))],
        out_specs=pl.BlockSpec((tm, D_MODEL), lambda i: (i, 0)),
        out_shape=jax.ShapeDtypeStruct((m, D_MODEL), F32),
        compiler_params=_compiler_params(1),
        name="ffn_half",
    )(x, gpre, gpost, wup, wdn)


def _norm_matmul_kernel(x_ref, g_ref, w_ref, *o_refs, splits):
    xn = _rms(x_ref[...], g_ref[...]).astype(BF16)
    off = 0
    for o_ref, n in zip(o_refs, splits):
        o_ref[...] = jnp.dot(xn, w_ref[:, off:off + n], preferred_element_type=F32)
        off += n


def _norm_matmul(x, g, w, splits, tm):
    m = x.shape[0]
    return pl.pallas_call(
        functools.partial(_norm_matmul_kernel, splits=splits),
        grid=(m // tm,),
        in_specs=[pl.BlockSpec((tm, D_MODEL), lambda i: (i, 0)),
                  _const_spec((1, D_MODEL)), _const_spec(w.shape)],
        out_specs=[pl.BlockSpec((tm, n), lambda i: (i, 0)) for n in splits],
        out_shape=[jax.ShapeDtypeStruct((m, n), F32) for n in splits],
        compiler_params=_compiler_params(1),
        name="norm_matmul",
    )(x, g, w)


def _out_proj_kernel(a_ref, x_ref, g_ref, w_ref, o_ref):
    y = jnp.dot(a_ref[...], w_ref[...], preferred_element_type=F32)
    o_ref[...] = x_ref[...] + _rms(y, g_ref[...])


def _out_proj(a, x, g, w, tm):
    m = x.shape[0]
    return pl.pallas_call(
        _out_proj_kernel,
        grid=(m // tm,),
        in_specs=[pl.BlockSpec((tm, D_MODEL), lambda i: (i, 0)),
                  pl.BlockSpec((tm, D_MODEL), lambda i: (i, 0)),
                  _const_spec((1, D_MODEL)), _const_spec(w.shape)],
        out_specs=pl.BlockSpec((tm, D_MODEL), lambda i: (i, 0)),
        out_shape=jax.ShapeDtypeStruct((m, D_MODEL), F32),
        compiler_params=_compiler_params(1),
        name="out_proj",
    )(a, x, g, w)


def _gdn_seq_kernel(qkv_ref, z_ref, ba_ref, conv0_ref, s0_ref, convw_ref, alog_ref, dtb_ref, normw_ref,
                    o_ref, sout_ref, convout_ref,
                    xext, q_s, k_s, v_s, s_s, *, tc, chunk, n_valid):
    t = pl.program_id(1)
    heads = GDN_HEADS
    group = qkv_ref.shape[0]
    sub = tc // chunk
    last = min(chunk, n_valid) - 1
    prev_rows = SUBLANES - (GDN_CONV - 1)

    @pl.when(t == 0)
    def _():
        xext[:, 0:SUBLANES, :] = conv0_ref[...]
        s_s[...] = s0_ref[...]

    xext[:, SUBLANES:SUBLANES + tc, :] = qkv_ref[...]
    for p in range(group):
        for part, dest in enumerate((q_s, k_s, v_s)):
            for h in range(heads):
                c0 = part * heads * GDN_DK + h * GDN_DK
                cols = slice(c0, c0 + GDN_DK)
                y = xext[p, prev_rows:prev_rows + tc, cols] * convw_ref[0:1, cols]
                for j in range(1, GDN_CONV):
                    y = y + xext[p, prev_rows + j:prev_rows + j + tc, cols] * convw_ref[j:j + 1, cols]
                y = _silu(y)
                if part < 2:
                    y = y * lax.rsqrt(jnp.sum(y * y, axis=-1, keepdims=True) + EPS)
                if part == 0:
                    y = y * GDN_DK ** -0.5
                dest[p, h] = y

    @pl.when(t == pl.num_programs(1) - 1)
    def _():
        convout_ref[...] = xext[:, n_valid:n_valid + SUBLANES, :]

    xext[:, 0:SUBLANES, :] = xext[:, tc:tc + SUBLANES, :]

    ri = lax.broadcasted_iota(jnp.int32, (tc, tc), 0)
    ci = lax.broadcasted_iota(jnp.int32, (tc, tc), 1)
    incl = ci <= ri
    if sub > 1:
        incl = incl & ((ri // chunk) == (ci // chunk))
    strict = incl & (ci < ri)
    tril = incl.astype(F32)
    chunk_rows = [slice(s * chunk, (s + 1) * chunk) for s in range(sub)]
    beta, gc, gc_t, exp_gc, exp_last, exp_rest = [], [], [], [], [], []
    for p in range(group):
        ba = ba_ref[p]
        b = jax.nn.sigmoid(ba[:, :LANES])
        if n_valid < tc:
            b = jnp.where(lax.broadcasted_iota(jnp.int32, b.shape, 0) < n_valid, b, 0.0)
        g = -jnp.exp(alog_ref[...]) * _softplus(ba[:, LANES:] + dtb_ref[...])
        c = _dot_f32(tril, g)
        c_last = [c[s * chunk + last:s * chunk + last + 1, :] for s in range(sub)]
        beta.append(b)
        gc.append(c)
        gc_t.append(c.T)
        exp_gc.append(jnp.exp(c))
        exp_last.append([jnp.exp(v) for v in c_last])
        exp_rest.append(jnp.exp(jnp.concatenate([jnp.broadcast_to(v, (chunk, LANES)) for v in c_last], axis=0) - c))
    normw = normw_ref[...]

    units = [(p, h) for p in range(group) for h in range(heads)]
    col = lambda x, h: x[:, h:h + 1]
    ks = {u: k_s[u] for u in units}
    bs = {(p, h): col(beta[p], h) for p, h in units}
    kbs = {u: ks[u] * bs[u] for u in units}
    decays = {(p, h): jnp.exp(jnp.where(incl, col(gc[p], h) - gc_t[p][h:h + 1, :], -jnp.inf)) for p, h in units}
    a_mats = [_dot_nt(kbs[u], ks[u]) * jnp.where(strict, decays[u], 0.0) for u in units]
    rhss = [jnp.concatenate([v_s[p, h] * bs[p, h], kbs[p, h] * col(exp_gc[p], h)], axis=1) for p, h in units]
    sols = dict(zip(units, _unit_lower_solve(a_mats, rhss, ri, ci, chunk)))
    qs = {u: q_s[u] for u in units}
    attns = {u: _dot_nt(qs[u], ks[u]) * decays[u] for u in units}
    qgs = {(p, h): qs[p, h] * col(exp_gc[p], h) for p, h in units}
    kdecs = {(p, h): ks[p, h] * col(exp_rest[p], h) for p, h in units}
    states = {u: s_s[u] for u in units}
    v_new = {u: [] for u in units}
    o_inter = {u: [] for u in units}
    for s, r in enumerate(chunk_rows):
        for p, h in units:
            u = (p, h)
            vn = sols[u][r, :GDN_DV] - _dot(sols[u][r, GDN_DV:], states[u])
            o_inter[u].append(_dot(qgs[u][r], states[u]))
            states[u] = states[u] * col(exp_last[p][s], h) + _dot_tn(kdecs[u][r], vn)
            v_new[u].append(vn)
    for p, h in units:
        u = (p, h)
        s_s[u] = states[u]
        o = jnp.concatenate(o_inter[u], axis=0) + _dot(attns[u], jnp.concatenate(v_new[u], axis=0))
        cols = slice(h * GDN_DV, (h + 1) * GDN_DV)
        o_ref[p, :, cols] = (_rms(o, normw) * _silu(z_ref[p, :, cols])).astype(o_ref.dtype)

    @pl.when(t == pl.num_programs(1) - 1)
    def _():
        sout_ref[...] = s_s[...]


def _gdn_seq(qkv, z, ba, conv0, s0, convw, alog, dtb, normw, *, layer, nseq, seqlen, tc, chunk, n_valid, group):
    nt = seqlen // tc
    tok = lambda b, t: (b, t, 0)
    seq3 = lambda b, t: (b, 0, 0)
    seq4 = lambda b, t: (b, 0, 0, 0)
    view = lambda a: a.reshape(nseq, seqlen, a.shape[-1])
    o, s_new, conv_new = pl.pallas_call(
        functools.partial(_gdn_seq_kernel, tc=tc, chunk=chunk, n_valid=n_valid),
        grid=(nseq // group, nt),
        in_specs=[pl.BlockSpec((group, tc, GDN_QKV), tok), pl.BlockSpec((group, tc, GDN_Z), tok),
                  pl.BlockSpec((group, tc, 2 * LANES), tok),
                  pl.BlockSpec((None, group, SUBLANES, GDN_QKV), lambda b, t: (layer, b, 0, 0)),
                  pl.BlockSpec((None, group, GDN_HEADS, GDN_DK, GDN_DV), lambda b, t: (layer, b, 0, 0, 0)),
                  _const_spec((SUBLANES, GDN_QKV)), _const_spec((1, LANES)), _const_spec((1, LANES)),
                  _const_spec((1, GDN_DV))],
        out_specs=[pl.BlockSpec((group, tc, GDN_Z), tok),
                   pl.BlockSpec((group, GDN_HEADS, GDN_DK, GDN_DV), seq4),
                   pl.BlockSpec((group, SUBLANES, GDN_QKV), seq3)],
        out_shape=[jax.ShapeDtypeStruct((nseq, seqlen, GDN_Z), BF16),
                   jax.ShapeDtypeStruct((nseq, GDN_HEADS, GDN_DK, GDN_DV), F32),
                   jax.ShapeDtypeStruct((nseq, SUBLANES, GDN_QKV), F32)],
        scratch_shapes=[pltpu.VMEM((group, tc + SUBLANES, GDN_QKV), F32),
                        pltpu.VMEM((group, GDN_HEADS, tc, GDN_DK), F32),
                        pltpu.VMEM((group, GDN_HEADS, tc, GDN_DK), F32),
                        pltpu.VMEM((group, GDN_HEADS, tc, GDN_DV), F32),
                        pltpu.VMEM((group, GDN_HEADS, GDN_DK, GDN_DV), F32)],
        compiler_params=_compiler_params(2),
        name="gdn_seq",
    )(view(qkv), view(z), view(ba), conv0, s0, convw, alog, dtb, normw)
    return o.reshape(nseq * seqlen, GDN_Z), s_new, conv_new


def _mlstm_seq_kernel(q_ref, k_ref, v_ref, og_ref, gates_ref, c0_ref, n0_ref, m0_ref, ibias_ref, fbias_ref,
                      h_ref, cout_ref, nout_ref, mout_ref,
                      cx_s, m_s, *, tc, chunk, n_valid):
    t = pl.program_id(1)
    heads = ML_HEADS
    sub = tc // chunk
    last = min(chunk, n_valid) - 1

    @pl.when(t == 0)
    def _():
        cx_s[:, :, :ML_DV] = c0_ref[...]
        cx_s[:, :, ML_DV:] = n0_ref[...]
        m_s[...] = m0_ref[...]

    gates = gates_ref[...]
    ig = gates[:, :LANES] + ibias_ref[...]
    lf = -_softplus(-(gates[:, LANES:] + fbias_ref[...]))

    ri = lax.broadcasted_iota(jnp.int32, (tc, tc), 0)
    ci = lax.broadcasted_iota(jnp.int32, (tc, tc), 1)
    incl = ci <= ri
    if sub > 1:
        incl = incl & ((ri // chunk) == (ci // chunk))
    mask = incl & (ci < n_valid) if n_valid < chunk else incl
    col_rows = lax.broadcasted_iota(jnp.int32, (chunk, 1), 0)
    ones_col = (lax.broadcasted_iota(jnp.int32, (tc, LANES), 1) == 0).astype(F32)
    lane_row = lax.broadcasted_iota(jnp.int32, (1, LANES), 1)
    chunk_rows = [slice(s * chunk, (s + 1) * chunk) for s in range(sub)]
    last_rows = [slice(s * chunk + last, s * chunk + last + 1) for s in range(sub)]
    hs = range(heads)

    bcum = _dot_f32(incl.astype(F32), lf)
    bcum_t = bcum.T
    ig_t = ig.T
    dmats = [jnp.where(mask, bcum[:, h:h + 1] - bcum_t[h:h + 1, :] + ig_t[h:h + 1, :], -jnp.inf) for h in hs]
    dmaxs = [jnp.max(d, axis=-1, keepdims=True) for d in dmats]
    m_row = m_s[0:1, :]
    m_in = [m_row[:, h:h + 1] for h in hs]
    inters = [[] for _ in hs]
    m_ts = [[] for _ in hs]
    m_news = [[] for _ in hs]
    for r in chunk_rows:
        for h in hs:
            inter = bcum[r, h:h + 1] + m_in[h]
            m_t = jnp.maximum(inter, dmaxs[h][r])
            m_in[h] = m_t[last:last + 1, :]
            inters[h].append(inter)
            m_ts[h].append(m_t)
            m_news[h].append(m_in[h])
    inters = [jnp.concatenate(v, axis=0) for v in inters]
    m_ts = [jnp.concatenate(v, axis=0) for v in m_ts]
    w_intras = [jnp.exp(d - m) for d, m in zip(dmats, m_ts)]
    w_inters = [jnp.exp(i - m) for i, m in zip(inters, m_ts)]
    qs = [q_ref[:, h * ML_DK:(h + 1) * ML_DK] * ML_DK ** -0.5 for h in hs]
    ks = [k_ref[:, h * ML_DK:(h + 1) * ML_DK] for h in hs]
    vxs = [jnp.concatenate([v_ref[:, h * ML_DV:(h + 1) * ML_DV], ones_col], axis=1) for h in hs]
    scores = [_dot_nt(q, k) * w for q, k, w in zip(qs, ks, w_intras)]
    intra = [_dot(s, vx) for s, vx in zip(scores, vxs)]
    cxs = [cx_s[h] for h in hs]
    inter_num = [[] for _ in hs]
    for s, r in enumerate(chunk_rows):
        for h in hs:
            inter_num[h].append(_dot(qs[h][r], cxs[h]))
            m_new = m_news[h][s]
            w_state = jnp.exp(jnp.where(col_rows <= last, bcum[last_rows[s], h:h + 1] - bcum[r, h:h + 1]
                                        + ig[r, h:h + 1], -jnp.inf) - m_new)
            carry_decay = jnp.exp(inters[h][last_rows[s], :] - m_new)
            cxs[h] = carry_decay * cxs[h] + _dot_tn(ks[h][r] * w_state, vxs[h][r])
    m_next = m_row
    for h in hs:
        cx_s[h] = cxs[h]
        num = w_inters[h] * jnp.concatenate(inter_num[h], axis=0) + intra[h]
        den = num[:, ML_DV:ML_DV + 1]
        hh = num[:, :ML_DV] / jnp.maximum(jnp.abs(den), jnp.exp(-m_ts[h]))
        vcols = slice(h * ML_DV, (h + 1) * ML_DV)
        h_ref[:, vcols] = (hh * jax.nn.sigmoid(og_ref[:, vcols])).astype(h_ref.dtype)
        m_next = jnp.where(lane_row == h, m_in[h], m_next)
    m_s[0:1, :] = m_next

    @pl.when(t == pl.num_programs(1) - 1)
    def _():
        cout_ref[...] = cx_s[:, :, :ML_DV]
        nout_ref[...] = cx_s[:, :, ML_DV:]
        mout_ref[...] = m_s[...]


def _mlstm_seq(q, k, v, og, gates, c0, n0, m0, ibias, fbias, *, nseq, seqlen, tc, chunk, n_valid):
    nt = seqlen // tc
    row = lambda b, t: (b * nt + t, 0)
    seq3 = lambda b, t: (b, 0, 0)
    seq4 = lambda b, t: (b, 0, 0, 0)
    m = nseq * seqlen
    return pl.pallas_call(
        functools.partial(_mlstm_seq_kernel, tc=tc, chunk=chunk, n_valid=n_valid),
        grid=(nseq, nt),
        in_specs=[pl.BlockSpec((tc, ML_QK), row), pl.BlockSpec((tc, ML_QK), row),
                  pl.BlockSpec((tc, ML_V), row), pl.BlockSpec((tc, ML_V), row),
                  pl.BlockSpec((tc, 2 * LANES), row),
                  pl.BlockSpec((None, ML_HEADS, ML_DK, ML_DV), seq4),
                  pl.BlockSpec((None, ML_HEADS, ML_DK, LANES), seq4),
                  pl.BlockSpec((None, SUBLANES, LANES), seq3),
                  _const_spec((1, LANES)), _const_spec((1, LANES))],
        out_specs=[pl.BlockSpec((tc, ML_V), row),
                   pl.BlockSpec((None, ML_HEADS, ML_DK, ML_DV), seq4),
                   pl.BlockSpec((None, ML_HEADS, ML_DK, LANES), seq4),
                   pl.BlockSpec((None, SUBLANES, LANES), seq3)],
        out_shape=[jax.ShapeDtypeStruct((m, ML_V), BF16),
                   jax.ShapeDtypeStruct((nseq, ML_HEADS, ML_DK, ML_DV), F32),
                   jax.ShapeDtypeStruct((nseq, ML_HEADS, ML_DK, LANES), F32),
                   jax.ShapeDtypeStruct((nseq, SUBLANES, LANES), F32)],
        scratch_shapes=[pltpu.VMEM((ML_HEADS, ML_DK, ML_DV + LANES), F32),
                        pltpu.VMEM((SUBLANES, LANES), F32)],
        compiler_params=_compiler_params(2),
        name="mlstm_seq",
    )(q, k, v, og, gates, c0, n0, m0, ibias, fbias)


def _mla_proj_kernel(x_ref, g_ref, cos_ref, sin_ref, win_ref, qnorm_ref, kvnorm_ref, wuq_ref, wuqsw_ref, wkv_ref,
                     *o_refs, decode):
    xn = _rms(x_ref[...], g_ref[...]).astype(BF16)
    proj = jnp.dot(xn, win_ref[...], preferred_element_type=F32)
    cq = _rms(proj[:, :MLA_Q_RANK], qnorm_ref[...]).astype(BF16)
    ckv = _rms(proj[:, MLA_Q_RANK:MLA_Q_RANK + MLA_KV_RANK], kvnorm_ref[...])
    base = MLA_Q_RANK + MLA_KV_RANK
    cos = cos_ref[...]
    sin = sin_ref[...]
    kr = proj[:, base:base + LANES] * cos + proj[:, base + LANES:base + 2 * LANES] * sin
    q_main = jnp.dot(cq, wuq_ref[...], preferred_element_type=F32)
    q_swap = jnp.dot(cq, wuqsw_ref[...], preferred_element_type=F32)
    if decode:
        qlat_ref, qrope_ref, ckv_ref, kr_ref = o_refs
    else:
        qcat_ref, kcat_ref, vt_ref, ckv_ref, kr_ref = o_refs
        kv = jnp.dot(ckv.astype(BF16), wkv_ref[...], preferred_element_type=F32)
    ckv_ref[...] = ckv
    kr_ref[...] = kr[:, :MLA_ROPE]
    for h in range(MLA_HEADS):
        lo = h * 2 * LANES
        q_nope = q_main[:, lo:lo + LANES] * MLA_SCALE
        q_rope = (q_main[:, lo + LANES:lo + 2 * LANES] * cos + q_swap[:, h * LANES:(h + 1) * LANES] * sin) * MLA_SCALE
        if decode:
            qlat_ref[:, lo:lo + 2 * LANES] = _dot(q_nope, wkv_ref[h]).astype(BF16)
            qrope_ref[:, h * LANES:(h + 1) * LANES] = q_rope.astype(BF16)
        else:
            qcat_ref[:, lo:lo + LANES] = q_nope.astype(BF16)
            qcat_ref[:, lo + LANES:lo + 2 * LANES] = q_rope.astype(BF16)
            kcat_ref[:, lo:lo + LANES] = kv[:, lo:lo + LANES].astype(BF16)
            kcat_ref[:, lo + LANES:lo + 2 * LANES] = kr.astype(BF16)
            key_tile = vt_ref.shape[-1]
            for s in range(vt_ref.shape[0]):
                v_tile = kv[s * key_tile:(s + 1) * key_tile, lo + LANES:lo + 2 * LANES]
                vt_ref[s, h * MLA_V:(h + 1) * MLA_V, :] = v_tile.T.astype(BF16)


def _mla_proj(x, g, cos, sin, win, qnorm, kvnorm, wuq, wuqsw, wkv, *, tm, decode, key_tile=None):
    m = x.shape[0]
    n_pos_blocks = cos.shape[0] // tm
    row = lambda i: (i, 0)
    pos = lambda i: (i % n_pos_blocks, 0)
    wide = MLA_HEADS * 2 * LANES
    if decode:
        widths = (wide, MLA_HEADS * LANES, MLA_KV_RANK, MLA_ROPE)
        dtypes = (BF16, BF16, F32, F32)
    else:
        widths = (wide, wide, MLA_KV_RANK, MLA_ROPE)
        dtypes = (BF16, BF16, F32, F32)
    out_specs = [pl.BlockSpec((tm, n), row) for n in widths]
    out_shape = [jax.ShapeDtypeStruct((m, n), dt) for n, dt in zip(widths, dtypes)]
    if not decode:
        out_specs.insert(2, pl.BlockSpec((tm // key_tile, MLA_HEADS * MLA_V, key_tile), lambda i: (i, 0, 0)))
        out_shape.insert(2, jax.ShapeDtypeStruct((m // key_tile, MLA_HEADS * MLA_V, key_tile), BF16))
    return pl.pallas_call(
        functools.partial(_mla_proj_kernel, decode=decode),
        grid=(m // tm,),
        in_specs=[pl.BlockSpec((tm, D_MODEL), row), _const_spec((1, D_MODEL)),
                  pl.BlockSpec((tm, LANES), pos), pl.BlockSpec((tm, LANES), pos),
                  _const_spec(win.shape), _const_spec((1, MLA_Q_RANK)), _const_spec((1, MLA_KV_RANK)),
                  _const_spec(wuq.shape), _const_spec(wuqsw.shape), _const_spec(wkv.shape)],
        out_specs=out_specs,
        out_shape=out_shape,
        compiler_params=_compiler_params(1),
        name="mla_proj_decode" if decode else "mla_proj",
    )(x, g, cos, sin, win, qnorm, kvnorm, wuq, wuqsw, wkv)


def _mla_attn_kernel(q_ref, k_ref, vt_ref, o_ref, m_s, l_s, acc_s, *, tq):
    qi = pl.program_id(1)
    heads = MLA_HEADS
    wide = 2 * LANES
    m_s[...] = jnp.full(m_s.shape, -jnp.inf, F32)
    l_s[...] = jnp.zeros(l_s.shape, F32)
    acc_s[...] = jnp.zeros(acc_s.shape, F32)
    key_i = lax.broadcasted_iota(jnp.int32, (tq, tq), 0)
    query_i = lax.broadcasted_iota(jnp.int32, (tq, tq), 1)

    def key_tile(j, diagonal):
        rows = pl.ds(pl.multiple_of(j * tq, tq), tq)
        for g0 in range(0, heads, ATTN_HEAD_GROUP):
            hs = range(g0, g0 + ATTN_HEAD_GROUP)
            ss = [lax.dot_general(k_ref[rows, h * wide:(h + 1) * wide], q_ref[:, h * wide:(h + 1) * wide],
                                  (((1,), (1,)), ((), ())), preferred_element_type=F32) for h in hs]
            if diagonal:
                ss = [jnp.where(key_i <= query_i, s, -jnp.inf) for s in ss]
            m_old = [m_s[h] for h in hs]
            m_new = [jnp.maximum(m, jnp.max(s, axis=0, keepdims=True)) for m, s in zip(m_old, ss)]
            ps = [jnp.exp(s - m) for s, m in zip(ss, m_new)]
            alphas = [jnp.exp(mo - mn) for mo, mn in zip(m_old, m_new)]
            pvs = [jnp.dot(vt_ref[j, h * MLA_V:(h + 1) * MLA_V, :], p.astype(BF16), preferred_element_type=F32)
                   for p, h in zip(ps, hs)]
            for i, h in enumerate(hs):
                l_s[h] = alphas[i] * l_s[h] + jnp.sum(ps[i], axis=0, keepdims=True)
                acc_s[h] = alphas[i] * acc_s[h] + pvs[i]
                m_s[h] = m_new[i]

    def body(j, carry):
        key_tile(j, False)
        return carry

    lax.fori_loop(0, qi, body, 0)
    key_tile(qi, True)
    for h in range(heads):
        o_ref[:, h * MLA_V:(h + 1) * MLA_V] = (acc_s[h] / l_s[h]).T.astype(o_ref.dtype)


def _mla_attn(qcat, kcat, vt, *, nseq, seqlen, tq):
    nq = seqlen // tq
    m = nseq * seqlen
    return pl.pallas_call(
        functools.partial(_mla_attn_kernel, tq=tq),
        grid=(nseq, nq),
        in_specs=[pl.BlockSpec((tq, MLA_HEADS * 2 * LANES), lambda b, i: (b * nq + i, 0)),
                  pl.BlockSpec((seqlen, MLA_HEADS * 2 * LANES), lambda b, i: (b, 0)),
                  pl.BlockSpec((nq, MLA_HEADS * MLA_V, tq), lambda b, i: (b, 0, 0))],
        out_specs=pl.BlockSpec((tq, MLA_HEADS * MLA_V), lambda b, i: (b * nq + i, 0)),
        out_shape=jax.ShapeDtypeStruct((m, MLA_HEADS * MLA_V), BF16),
        scratch_shapes=[pltpu.VMEM((MLA_HEADS, 1, tq), F32), pltpu.VMEM((MLA_HEADS, 1, tq), F32),
                        pltpu.VMEM((MLA_HEADS, MLA_V, tq), F32)],
        compiler_params=_compiler_params(2),
        name="mla_attn",
    )(qcat, kcat, vt)


def _mla_decode_kernel(pt_ref, qlat_ref, qrope_ref, ckv_ref, kr_ref, wuv_ref, *rest, n_pages):
    lat_refs = rest[:n_pages]
    rope_refs = rest[n_pages:2 * n_pages]
    o_ref = rest[2 * n_pages]
    ql_s, qr_s, m_s, l_s, acc_s = rest[2 * n_pages + 1:]
    g = pl.program_id(1)
    heads = MLA_HEADS
    tok = DEC_PAD

    @pl.when(g == 0)
    def _():
        for h in range(heads):
            ql_s[h * tok:(h + 1) * tok, :] = qlat_ref[:, h * 2 * LANES:(h + 1) * 2 * LANES]
            qr_s[h * tok:(h + 1) * tok, :] = qrope_ref[:, h * LANES:(h + 1) * LANES]
        m_s[...] = jnp.full(m_s.shape, -jnp.inf, F32)
        l_s[...] = jnp.zeros(l_s.shape, F32)
        acc_s[...] = jnp.zeros(acc_s.shape, F32)

    ql = ql_s[...]
    qr = qr_s[:, :MLA_ROPE]

    def update(chains, scores, values):
        m_old = [m_s[c] for c in chains]
        m_new = [jnp.maximum(m, jnp.max(s, axis=-1, keepdims=True)) for m, s in zip(m_old, scores)]
        ps = [jnp.exp(s - m) for s, m in zip(scores, m_new)]
        alphas = [jnp.exp(mo - mn) for mo, mn in zip(m_old, m_new)]
        for i, c in enumerate(chains):
            l_s[c] = alphas[i] * l_s[c] + jnp.sum(ps[i], axis=-1, keepdims=True)
            pb = ps[i].astype(BF16)
            acc = alphas[i] * acc_s[c]
            for cols, val in values[i]:
                acc = acc + jnp.dot(pb[:, cols], val, preferred_element_type=F32)
            acc_s[c] = acc
            m_s[c] = m_new[i]

    n_chains = m_s.shape[0]
    per = n_pages // n_chains
    lats = [r[...].astype(BF16) for r in lat_refs]
    scores = [_dot_nt(ql, lat) + _dot(qr, r[...]) for lat, r in zip(lats, rope_refs)]
    update(range(n_chains),
           [jnp.concatenate(scores[c * per:(c + 1) * per], axis=1) for c in range(n_chains)],
           [[(slice(i * PAGE_SIZE, (i + 1) * PAGE_SIZE), lats[c * per + i]) for i in range(per)]
            for c in range(n_chains)])

    @pl.when(g == pl.num_programs(1) - 1)
    def _():
        ckv = ckv_ref[...].astype(BF16)
        s = _dot_nt(ql, ckv) + _dot_nt(qr, kr_ref[...])
        qt = lax.broadcasted_iota(jnp.int32, s.shape, 0) % tok
        kt = lax.broadcasted_iota(jnp.int32, s.shape, 1)
        update([0], [jnp.where(kt <= qt, s, -jnp.inf)], [[(slice(0, tok), ckv)]])
        m_all = m_s[0]
        for c in range(1, n_chains):
            m_all = jnp.maximum(m_all, m_s[c])
        weights = [jnp.exp(m_s[c] - m_all) for c in range(n_chains)]
        l_all = sum(w * l_s[c] for c, w in enumerate(weights))
        acc_all = sum(w * acc_s[c] for c, w in enumerate(weights))
        o_lat = acc_all / l_all
        for h in range(heads):
            o_ref[:, h * MLA_V:(h + 1) * MLA_V] = _dot(o_lat[h * tok:(h + 1) * tok, :], wuv_ref[h]).astype(o_ref.dtype)


def _mla_decode(page_table, qlat, qrope, ckv, kr, wuv, lat_pages, rope_pages, *, nseq):
    n_pages_seq = page_table.shape[1]
    n = PAGES_PER_STEP
    groups = n_pages_seq // n
    seq = lambda b, g, pt: (b, 0)

    def page_map(i):
        return lambda b, g, pt: (pt[b * n_pages_seq + g * n + i], 0, 0)

    rows = MLA_HEADS * DEC_PAD
    grid_spec = pltpu.PrefetchScalarGridSpec(
        num_scalar_prefetch=1,
        grid=(nseq, groups),
        in_specs=[pl.BlockSpec((DEC_PAD, MLA_HEADS * 2 * LANES), seq),
                  pl.BlockSpec((DEC_PAD, MLA_HEADS * LANES), seq),
                  pl.BlockSpec((DEC_PAD, MLA_KV_RANK), seq),
                  pl.BlockSpec((DEC_PAD, MLA_ROPE), seq),
                  pl.BlockSpec(wuv.shape, lambda b, g, pt: (0, 0, 0))]
                 + [pl.BlockSpec((None, PAGE_SIZE, MLA_KV_RANK), page_map(i)) for i in range(n)]
                 + [pl.BlockSpec((None, MLA_ROPE, PAGE_SIZE), page_map(i)) for i in range(n)],
        out_specs=pl.BlockSpec((DEC_PAD, MLA_HEADS * MLA_V), seq),
        scratch_shapes=[pltpu.VMEM((rows, MLA_KV_RANK), BF16), pltpu.VMEM((rows, LANES), BF16),
                        pltpu.VMEM((DECODE_CHAINS, rows, 1), F32), pltpu.VMEM((DECODE_CHAINS, rows, 1), F32),
                        pltpu.VMEM((DECODE_CHAINS, rows, MLA_KV_RANK), F32)],
    )
    return pl.pallas_call(
        functools.partial(_mla_decode_kernel, n_pages=n),
        grid_spec=grid_spec,
        out_shape=jax.ShapeDtypeStruct((nseq * DEC_PAD, MLA_HEADS * MLA_V), BF16),
        compiler_params=_compiler_params(2),
        name="mla_decode",
    )(page_table.reshape(-1), qlat, qrope, ckv, kr, wuv, *([lat_pages] * n), *([rope_pages] * n))


def _row(v, width=None):
    v = v.astype(F32).reshape(1, -1)
    if width is not None and v.shape[1] < width:
        v = jnp.pad(v, ((0, 0), (0, width - v.shape[1])))
    return v


def _pad_cols(w, width):
    return jnp.pad(w, ((0, 0), (0, width - w.shape[1])))


def _prep_ffn(w_up, w_down):
    return w_up.astype(BF16), w_down.reshape(w_down.shape[:2] + (N_FF_CHUNKS, FF_CHUNK, D_MODEL)).astype(BF16)


def _prep_gdn(w_in, conv_w, a_log, dt_bias, norm_w, w_out):
    main = GDN_QKV + GDN_Z
    w = jnp.concatenate([w_in[:, :main], _pad_cols(w_in[:, main:main + GDN_HEADS], LANES),
                         _pad_cols(w_in[:, main + GDN_HEADS:], LANES)], axis=1).astype(BF16)
    convw = jnp.pad(conv_w.astype(F32), ((0, SUBLANES - GDN_CONV), (0, 0)))
    return w, convw, _row(a_log, LANES), _row(dt_bias, LANES), _row(norm_w), w_out.astype(BF16)


def _prep_mlstm(w_in, i_bias, f_bias, w_out):
    main = 2 * ML_QK + 2 * ML_V
    w = jnp.concatenate([w_in[:, :main], _pad_cols(w_in[:, main:main + ML_HEADS], LANES),
                         _pad_cols(w_in[:, main + ML_HEADS:], LANES)], axis=1).astype(BF16)
    return w, _row(i_bias, LANES), _row(f_bias, LANES), w_out.astype(BF16)


def _swap_halves(w):
    half = w.shape[-1] // 2
    return jnp.concatenate([w[..., half:], w[..., :half]], axis=-1)


def _prep_mla(w_in, q_norm, kv_norm, w_uq, w_ukv, w_out):
    base = MLA_Q_RANK + MLA_KV_RANK
    kr = w_in[:, base:]
    win = jnp.concatenate([w_in[:, :base], _pad_cols(kr, LANES), _pad_cols(_swap_halves(kr), LANES)],
                          axis=1).astype(BF16)
    wq = w_uq.reshape(MLA_Q_RANK, MLA_HEADS, MLA_NOPE + MLA_ROPE)
    nope, rope = wq[..., :MLA_NOPE], wq[..., MLA_NOPE:]
    zeros = jnp.zeros((MLA_Q_RANK, MLA_HEADS, LANES - MLA_ROPE), w_uq.dtype)
    wuq = jnp.concatenate([nope, rope, zeros], axis=-1).reshape(MLA_Q_RANK, -1).astype(BF16)
    wuqsw = jnp.concatenate([_swap_halves(rope), zeros], axis=-1).reshape(MLA_Q_RANK, -1).astype(BF16)
    wkv3 = w_ukv.reshape(MLA_KV_RANK, MLA_HEADS, MLA_NOPE + MLA_V)
    wuk_t = wkv3[..., :MLA_NOPE].transpose(1, 2, 0).astype(BF16)
    wuv = wkv3[..., MLA_NOPE:].transpose(1, 0, 2).astype(BF16)
    return dict(win=win, qnorm=_row(q_norm), kvnorm=_row(kv_norm), wuq=wuq, wuqsw=wuqsw,
                wkv=w_ukv.astype(BF16), wuk_t=wuk_t, wuv=wuv, wout=w_out.astype(BF16))


def _rope_tables(pos):
    half = MLA_ROPE // 2
    inv = ROPE_THETA ** (-jnp.arange(half, dtype=F32) / half)
    ang = pos.astype(F32)[:, None] * inv
    cos, sin = jnp.cos(ang), jnp.sin(ang)
    pad = jnp.zeros((pos.shape[0], LANES - MLA_ROPE), F32)
    return jnp.concatenate([cos, cos, pad], axis=1), jnp.concatenate([-sin, sin, pad], axis=1)


def _tiles(nseq, seqlen):
    m = nseq * seqlen
    return dict(ffn=min(512, m), proj=min(256, m), mla_proj=min(512, seqlen), attn_q=min(256, seqlen),
                gdn=min(2 * GDN_CHUNK, seqlen), gdn_group=math.gcd(GDN_SEQ_GROUP, nseq),
                mlstm=min(2 * ML_CHUNK, seqlen))


def _trunk(x, *, nseq, seqlen, n_tokens, decode, states, page_table, weights):
    gdn_S, gdn_conv, mla_lat, mla_rope, ml_C, ml_n, ml_m = states
    chunk_g = min(GDN_CHUNK, seqlen)
    chunk_m = min(ML_CHUNK, seqlen)
    tiles = _tiles(nseq, seqlen)
    tm = tiles["ffn"]
    new = ([], [], [], [], [], [], [])
    counts = [0, 0, 0]
    for layer in range(4):
        kind = layer % 3
        j = counts[kind]
        counts[kind] += 1
        ng = weights["gains"][layer]
        wup, wdn = weights["ffn"]
        x = _ffn_half(x, ng[0], ng[1], wup, wdn, layer, 0, tm)
        if kind == 0:
            w, convw, alog, dtb, normw, wout = weights["gdn"][j]
            qkv, z, ba = _norm_matmul(x, ng[2], w, (GDN_QKV, GDN_Z, 2 * LANES), tiles["proj"])
            o, s_new, conv_new = _gdn_seq(qkv, z, ba, gdn_conv, gdn_S, convw, alog, dtb, normw, layer=j,
                                          nseq=nseq, seqlen=seqlen, tc=tiles["gdn"], chunk=chunk_g,
                                          n_valid=min(tiles["gdn"], n_tokens), group=tiles["gdn_group"])
            new[0].append(s_new)
            new[1].append(conv_new[:, SUBLANES - (GDN_CONV - 1):, :])
        elif kind == 1:
            p = weights["mla"][j]
            cos, sin = weights["rope"]
            if decode:
                qlat, qrope, ckv, kr = _mla_proj(x, ng[2], cos, sin, p["win"], p["qnorm"], p["kvnorm"], p["wuq"],
                                                 p["wuqsw"], p["wuk_t"], tm=tiles["mla_proj"], decode=True)
                o = _mla_decode(page_table, qlat, qrope, ckv, kr, p["wuv"], mla_lat[j], mla_rope[j], nseq=nseq)
            else:
                qcat, kcat, vt, ckv, kr = _mla_proj(x, ng[2], cos, sin, p["win"], p["qnorm"], p["kvnorm"], p["wuq"],
                                                    p["wuqsw"], p["wkv"], tm=tiles["mla_proj"], decode=False,
                                                    key_tile=tiles["attn_q"])
                o = _mla_attn(qcat, kcat, vt, nseq=nseq, seqlen=seqlen, tq=tiles["attn_q"])
            wout = p["wout"]
            new[2].append(ckv)
            new[3].append(kr)
        else:
            w, ibias, fbias, wout = weights["mlstm"][j]
            q, k, v, og, gates = _norm_matmul(x, ng[2], w, (ML_QK, ML_QK, ML_V, ML_V, 2 * LANES), tiles["proj"])
            o, c_new, n_new, m_new = _mlstm_seq(q, k, v, og, gates, ml_C[j], ml_n[j], ml_m[j], ibias, fbias,
                                                nseq=nseq, seqlen=seqlen, tc=tiles["mlstm"], chunk=chunk_m,
                                                n_valid=min(tiles["mlstm"], n_tokens))
            new[4].append(c_new)
            new[5].append(n_new[..., 0])
            new[6].append(m_new[:, 0, :ML_HEADS])
        x = _out_proj(o, x, ng[3], wout, tm)
        x = _ffn_half(x, ng[4], ng[5], wup, wdn, layer, 1, tm)
    return x, [jnp.stack(s) for s in new]


def kernel(x_prompt, x_sample, state_gdn_S, state_gdn_conv, cache_mla_latent, cache_mla_rope, state_mlstm_C, state_mlstm_n, state_mlstm_m, page_table, norm_gains, w_ffn_up, w_ffn_down, gdn_w_in, gdn_conv_w, gdn_a_log, gdn_dt_bias, gdn_norm_w, gdn_w_out, mla_w_in, mla_q_norm, mla_kv_norm, mla_w_uq, mla_w_ukv, mla_w_out, mlstm_w_in, mlstm_i_bias, mlstm_f_bias, mlstm_w_out):
    nb, seq, _ = x_prompt.shape
    db, dseq, _ = x_sample.shape
    n_gdn, n_mla, n_ml = gdn_w_in.shape[0], mla_w_in.shape[0], mlstm_w_in.shape[0]
    past = page_table.shape[1] * PAGE_SIZE

    weights = dict(
        gains=[[_row(norm_gains[l, i]) for i in range(6)] for l in range(4)],
        ffn=_prep_ffn(w_ffn_up, w_ffn_down),
        gdn=[_prep_gdn(gdn_w_in[j], gdn_conv_w[j], gdn_a_log[j], gdn_dt_bias[j], gdn_norm_w[j], gdn_w_out[j])
             for j in range(n_gdn)],
        mla=[_prep_mla(mla_w_in[j], mla_q_norm[j], mla_kv_norm[j], mla_w_uq[j], mla_w_ukv[j], mla_w_out[j])
             for j in range(n_mla)],
        mlstm=[_prep_mlstm(mlstm_w_in[j], mlstm_i_bias[j], mlstm_f_bias[j], mlstm_w_out[j]) for j in range(n_ml)],
    )

    zeros = lambda *s: jnp.zeros(s, F32)
    states_p = (zeros(n_gdn, nb, GDN_HEADS, GDN_DK, GDN_DV), zeros(n_gdn, nb, SUBLANES, GDN_QKV), None, None,
                zeros(n_ml, nb, ML_HEADS, ML_DK, ML_DV), zeros(n_ml, nb, ML_HEADS, ML_DK, LANES),
                zeros(n_ml, nb, SUBLANES, LANES))
    y_p, st_p = _trunk(x_prompt.reshape(nb * seq, D_MODEL), nseq=nb, seqlen=seq, n_tokens=seq, decode=False,
                       states=states_p, page_table=None,
                       weights=dict(weights, rope=_rope_tables(jnp.arange(seq))))

    pad_t = DEC_PAD - dseq
    x_s = jnp.pad(x_sample, ((0, 0), (0, pad_t), (0, 0))).reshape(db * DEC_PAD, D_MODEL)
    conv0 = jnp.pad(state_gdn_conv, ((0, 0), (0, 0), (SUBLANES - (GDN_CONV - 1), 0), (0, 0)))
    n0 = jnp.pad(state_mlstm_n[..., None], ((0, 0),) * 4 + ((0, LANES - 1),))
    m0 = jnp.pad(state_mlstm_m[:, :, None, :], ((0, 0), (0, 0), (0, SUBLANES - 1), (0, LANES - ML_HEADS)))
    states_s = (state_gdn_S, conv0,
                cache_mla_latent.reshape((n_mla, -1) + cache_mla_latent.shape[2:]),
                jnp.swapaxes(cache_mla_rope.reshape((n_mla, -1) + cache_mla_rope.shape[2:]), -1, -2),
                state_mlstm_C, n0, m0)
    pos_s = jnp.tile(past + jnp.arange(DEC_PAD), db)
    y_s, st_s = _trunk(x_s, nseq=db, seqlen=DEC_PAD, n_tokens=dseq, decode=True, states=states_s,
                       page_table=page_table, weights=dict(weights, rope=_rope_tables(pos_s)))

    gdn_S_p, gdn_conv_p, lat_p, rope_p, ml_C_p, ml_n_p, ml_m_p = st_p
    gdn_S_s, gdn_conv_s, lat_s, rope_s, ml_C_s, ml_n_s, ml_m_s = st_s
    unpad = lambda a: a.reshape(a.shape[0], db, DEC_PAD, a.shape[-1])[:, :, :dseq]
    return (y_p.reshape(nb, seq, D_MODEL), unpad(y_s[None])[0],
            gdn_S_p, gdn_S_s, gdn_conv_p, gdn_conv_s,
            lat_p.reshape(n_mla, -1, PAGE_SIZE, MLA_KV_RANK), unpad(lat_s),
            rope_p.reshape(n_mla, -1, PAGE_SIZE, MLA_ROPE), unpad(rope_s),
            ml_C_p, ml_C_s, ml_n_p, ml_n_s, ml_m_p, ml_m_s)
```

```python
import functools
import math

import jax
import jax.numpy as jnp
from jax import lax
from jax.experimental import pallas as pl
from jax.experimental.pallas import tpu as pltpu

F32 = jnp.float32
BF16 = jnp.bfloat16
HIGHEST = lax.Precision.HIGHEST

D_MODEL = 1024
PAGE_SIZE = 128
EPS = 1e-6

GDN_HEADS = 8
GDN_DK = 128
GDN_DV = 128
GDN_CONV = 4
GDN_CHUNK = 64
GDN_QKV = GDN_HEADS * (2 * GDN_DK + GDN_DV)
GDN_Z = GDN_HEADS * GDN_DV

MLA_HEADS = 8
MLA_NOPE = 128
MLA_ROPE = 64
MLA_V = 128
MLA_Q_RANK = 384
MLA_KV_RANK = 256
MLA_SCALE = (MLA_NOPE + MLA_ROPE) ** -0.5
ROPE_THETA = 10000.0

ML_HEADS = 4
ML_DK = 128
ML_DV = 256
ML_CHUNK = 64
ML_QK = ML_HEADS * ML_DK
ML_V = ML_HEADS * ML_DV

D_FF = 2816
FF_CHUNK = 256
N_FF_CHUNKS = D_FF // FF_CHUNK

LANES = 128
SUBLANES = 8
DEC_PAD = SUBLANES
VMEM_LIMIT = 56 * 1024 * 1024
PAGES_PER_STEP = 16
DECODE_CHAINS = 4
ATTN_HEAD_GROUP = 4
GDN_SEQ_GROUP = 2
SOLVE_REFINEMENTS = 1


def _rms(x, g):
    return x * lax.rsqrt(jnp.mean(x * x, axis=-1, keepdims=True) + EPS) * g


def _silu(x):
    return x * jax.nn.sigmoid(x)


def _softplus(x):
    return jnp.maximum(x, 0.0) + jnp.log1p(jnp.exp(-jnp.abs(x)))


def _dot(a, b):
    return jnp.dot(a.astype(BF16), b.astype(BF16), preferred_element_type=F32)


def _dot_nt(a, b):
    return lax.dot_general(a.astype(BF16), b.astype(BF16), (((1,), (1,)), ((), ())),
                           preferred_element_type=F32)


def _dot_tn(a, b):
    return lax.dot_general(a.astype(BF16), b.astype(BF16), (((0,), (0,)), ((), ())),
                           preferred_element_type=F32)


def _dot_f32(a, b):
    return jnp.dot(a, b, precision=HIGHEST, preferred_element_type=F32)


def _split(x):
    hi = x.astype(BF16)
    return hi, (x - hi.astype(F32)).astype(BF16)


def _dot3(a, b):
    (a_hi, a_lo), (b_hi, b_lo) = a, b
    if a_hi.shape[1] % LANES == 0:
        return jnp.dot(jnp.concatenate([a_hi, a_lo, a_hi], axis=1), jnp.concatenate([b_hi, b_hi, b_lo], axis=0),
                       preferred_element_type=F32)
    m = a_hi.shape[0]
    r = jnp.dot(jnp.concatenate([a_hi, a_lo], axis=0), b_hi, preferred_element_type=F32)
    return r[:m] + r[m:] + jnp.dot(a_hi, b_lo, preferred_element_type=F32)


def _unit_lower_inverse_minus_eye(a_mats, ri, ci, chunk):
    rs = None
    s = 1
    while s < chunk:
        shift = s.bit_length() - 1
        off = ((ri >> (shift + 1)) == (ci >> (shift + 1))) & ((ri >> shift) != (ci >> shift))
        a_offs = [jnp.where(off, a, 0.0) for a in a_mats]
        if rs is None:
            rs = [-a for a in a_offs]
        else:
            bs = [a + _dot(r, a) for a, r in zip(a_offs, rs)]
            rs = [r - b - _dot(b, r) for r, b in zip(rs, bs)]
        s *= 2
    return rs


def _unit_lower_solve(a_mats, rhss, ri, ci, chunk):
    rs = _unit_lower_inverse_minus_eye(a_mats, ri, ci, chunk)
    a_sp = [_split(a) for a in a_mats]
    xs = [rhs + _dot(r, rhs) for r, rhs in zip(rs, rhss)]
    for _ in range(SOLVE_REFINEMENTS):
        resid = [rhs - x - _dot3(a, _split(x)) for rhs, x, a in zip(rhss, xs, a_sp)]
        xs = [x + e + _dot(r, e) for x, e, r in zip(xs, resid, rs)]
    return xs


def _compiler_params(n_axes):
    return pltpu.CompilerParams(dimension_semantics=("arbitrary",) * n_axes,
                                vmem_limit_bytes=VMEM_LIMIT)


SINGLE_BUFFER = pl.Buffered(1)


def _const_spec(shape):
    nd = len(shape)
    return pl.BlockSpec(shape, lambda *_: (0,) * nd, pipeline_mode=SINGLE_BUFFER)


def _ffn_kernel(*refs, mixer_out):
    if mixer_out:
        a_ref, gmix_ref, wout_ref, x_ref, gpre_ref, gpost_ref, wup_ref, wdn_ref, o_ref = refs
        x = x_ref[...] + _rms(jnp.dot(a_ref[...], wout_ref[...], preferred_element_type=F32), gmix_ref[...])
    else:
        x_ref, gpre_ref, gpost_ref, wup_ref, wdn_ref, o_ref = refs
        x = x_ref[...]
    xn = _rms(x, gpre_ref[...]).astype(BF16)
    acc = jnp.zeros(x.shape, F32)
    for c in range(N_FF_CHUNKS):
        lo = c * FF_CHUNK
        gate = jnp.dot(xn, wup_ref[:, lo:lo + FF_CHUNK], preferred_element_type=F32)
        up = jnp.dot(xn, wup_ref[:, D_FF + lo:D_FF + lo + FF_CHUNK], preferred_element_type=F32)
        act = (_silu(gate) * up).astype(BF16)
        acc = acc + jnp.dot(act, wdn_ref[c], preferred_element_type=F32)
    o_ref[...] = x + 0.5 * _rms(acc, gpost_ref[...])


def _ffn_half(x, gpre, gpost, wup, wdn, layer, half, tm, mixer_out=None):
    m = x.shape[0]
    row = pl.BlockSpec((tm, D_MODEL), lambda i: (i, 0))
    gain = _const_spec((1, D_MODEL))
    in_specs = [row, gain, gain,
                pl.BlockSpec((None, None) + wup.shape[2:], lambda i: (layer, half, 0, 0),
                             pipeline_mode=SINGLE_BUFFER),
                pl.BlockSpec((None, None) + wdn.shape[2:], lambda i: (layer, half, 0, 0, 0),
                             pipeline_mode=SINGLE_BUFFER)]
    args = (x, gpre, gpost, wup, wdn)
    if mixer_out is not None:
        a, gmix, wout = mixer_out
        in_specs = [row, gain, _const_spec(wout.shape)] + in_specs
        args = (a, gmix, wout) + args
    return pl.pallas_call(
        functools.partial(_ffn_kernel, mixer_out=mixer_out is not None),
        grid=(m // tm,),
        in_specs=in_specs,
        out_specs=row,
        out_shape=jax.ShapeDtypeStruct((m, D_MODEL), F32),
        compiler_params=_compiler_params(1),
        name="ffn_half",
    )(*args)


def _norm_matmul_kernel(x_ref, g_ref, w_ref, *o_refs, splits):
    xn = _rms(x_ref[...], g_ref[...]).astype(BF16)
    off = 0
    for o_ref, n in zip(o_refs, splits):
        o_ref[...] = jnp.dot(xn, w_ref[:, off:off + n], preferred_element_type=F32)
        off += n


def _norm_matmul(x, g, w, splits, tm):
    m = x.shape[0]
    return pl.pallas_call(
        functools.partial(_norm_matmul_kernel, splits=splits),
        grid=(m // tm,),
        in_specs=[pl.BlockSpec((tm, D_MODEL), lambda i: (i, 0)),
                  _const_spec((1, D_MODEL)), _const_spec(w.shape)],
        out_specs=[pl.BlockSpec((tm, n), lambda i: (i, 0)) for n in splits],
        out_shape=[jax.ShapeDtypeStruct((m, n), F32) for n in splits],
        compiler_params=_compiler_params(1),
        name="norm_matmul",
    )(x, g, w)


def _gdn_seq_kernel(qkv_ref, z_ref, ba_ref, conv0_ref, s0_ref, convw_ref, alog_ref, dtb_ref, normw_ref,
                    o_ref, sout_ref, convout_ref,
                    xext, q_s, k_s, v_s, s_s, *, tc, chunk, n_valid):
    t = pl.program_id(1)
    heads = GDN_HEADS
    group = qkv_ref.shape[0]
    sub = tc // chunk
    last = min(chunk, n_valid) - 1
    prev_rows = SUBLANES - (GDN_CONV - 1)

    @pl.when(t == 0)
    def _():
        xext[:, 0:SUBLANES, :] = conv0_ref[...]
        s_s[...] = s0_ref[...]

    xext[:, SUBLANES:SUBLANES + tc, :] = qkv_ref[...]
    for p in range(group):
        for part, dest in enumerate((q_s, k_s, v_s)):
            for h in range(heads):
                c0 = part * heads * GDN_DK + h * GDN_DK
                cols = slice(c0, c0 + GDN_DK)
                y = xext[p, prev_rows:prev_rows + tc, cols] * convw_ref[0:1, cols]
                for j in range(1, GDN_CONV):
                    y = y + xext[p, prev_rows + j:prev_rows + j + tc, cols] * convw_ref[j:j + 1, cols]
                y = _silu(y)
                if part < 2:
                    y = y * lax.rsqrt(jnp.sum(y * y, axis=-1, keepdims=True) + EPS)
                if part == 0:
                    y = y * GDN_DK ** -0.5
                dest[p, h] = y

    @pl.when(t == pl.num_programs(1) - 1)
    def _():
        convout_ref[...] = xext[:, n_valid:n_valid + SUBLANES, :]

    xext[:, 0:SUBLANES, :] = xext[:, tc:tc + SUBLANES, :]

    ri = lax.broadcasted_iota(jnp.int32, (tc, tc), 0)
    ci = lax.broadcasted_iota(jnp.int32, (tc, tc), 1)
    incl = ci <= ri
    if sub > 1:
        incl = incl & ((ri // chunk) == (ci // chunk))
    strict = incl & (ci < ri)
    tril = incl.astype(F32)
    chunk_rows = [slice(s * chunk, (s + 1) * chunk) for s in range(sub)]
    beta, gc, gc_t, exp_gc, exp_last, exp_rest = [], [], [], [], [], []
    for p in range(group):
        ba = ba_ref[p]
        b = jax.nn.sigmoid(ba[:, :LANES])
        if n_valid < tc:
            b = jnp.where(lax.broadcasted_iota(jnp.int32, b.shape, 0) < n_valid, b, 0.0)
        g = -jnp.exp(alog_ref[...]) * _softplus(ba[:, LANES:] + dtb_ref[...])
        c = _dot_f32(tril, g)
        c_last = [c[s * chunk + last:s * chunk + last + 1, :] for s in range(sub)]
        beta.append(b)
        gc.append(c)
        gc_t.append(c.T)
        exp_gc.append(jnp.exp(c))
        exp_last.append([jnp.exp(v) for v in c_last])
        exp_rest.append(jnp.exp(jnp.concatenate([jnp.broadcast_to(v, (chunk, LANES)) for v in c_last], axis=0) - c))
    normw = normw_ref[...]

    units = [(p, h) for p in range(group) for h in range(heads)]
    col = lambda x, h: x[:, h:h + 1]
    ks = {u: k_s[u] for u in units}
    bs = {(p, h): col(beta[p], h) for p, h in units}
    kbs = {u: ks[u] * bs[u] for u in units}
    decays = {(p, h): jnp.exp(jnp.where(incl, col(gc[p], h) - gc_t[p][h:h + 1, :], -jnp.inf)) for p, h in units}
    a_mats = [_dot_nt(kbs[u], ks[u]) * jnp.where(strict, decays[u], 0.0) for u in units]
    rhss = [jnp.concatenate([v_s[p, h] * bs[p, h], kbs[p, h] * col(exp_gc[p], h)], axis=1) for p, h in units]
    sols = dict(zip(units, _unit_lower_solve(a_mats, rhss, ri, ci, chunk)))
    qs = {u: q_s[u] for u in units}
    attns = {u: _dot_nt(qs[u], ks[u]) * decays[u] for u in units}
    qgs = {(p, h): qs[p, h] * col(exp_gc[p], h) for p, h in units}
    kdecs = {(p, h): ks[p, h] * col(exp_rest[p], h) for p, h in units}
    states = {u: s_s[u] for u in units}
    v_new = {u: [] for u in units}
    o_inter = {u: [] for u in units}
    for s, r in enumerate(chunk_rows):
        for p, h in units:
            u = (p, h)
            vn = sols[u][r, :GDN_DV] - _dot(sols[u][r, GDN_DV:], states[u])
            o_inter[u].append(_dot(qgs[u][r], states[u]))
            states[u] = states[u] * col(exp_last[p][s], h) + _dot_tn(kdecs[u][r], vn)
            v_new[u].append(vn)
    for p, h in units:
        u = (p, h)
        s_s[u] = states[u]
        o = jnp.concatenate(o_inter[u], axis=0) + _dot(attns[u], jnp.concatenate(v_new[u], axis=0))
        cols = slice(h * GDN_DV, (h + 1) * GDN_DV)
        o_ref[p, :, cols] = (_rms(o, normw) * _silu(z_ref[p, :, cols])).astype(o_ref.dtype)

    @pl.when(t == pl.num_programs(1) - 1)
    def _():
        sout_ref[...] = s_s[...]


def _gdn_seq(qkv, z, ba, conv0, s0, convw, alog, dtb, normw, *, layer, nseq, seqlen, tc, chunk, n_valid, group):
    nt = seqlen // tc
    tok = lambda b, t: (b, t, 0)
    seq3 = lambda b, t: (b, 0, 0)
    seq4 = lambda b, t: (b, 0, 0, 0)
    view = lambda a: a.reshape(nseq, seqlen, a.shape[-1])
    o, s_new, conv_new = pl.pallas_call(
        functools.partial(_gdn_seq_kernel, tc=tc, chunk=chunk, n_valid=n_valid),
        grid=(nseq // group, nt),
        in_specs=[pl.BlockSpec((group, tc, GDN_QKV), tok), pl.BlockSpec((group, tc, GDN_Z), tok),
                  pl.BlockSpec((group, tc, 2 * LANES), tok),
                  pl.BlockSpec((None, group, SUBLANES, GDN_QKV), lambda b, t: (layer, b, 0, 0)),
                  pl.BlockSpec((None, group, GDN_HEADS, GDN_DK, GDN_DV), lambda b, t: (layer, b, 0, 0, 0)),
                  _const_spec((SUBLANES, GDN_QKV)), _const_spec((1, LANES)), _const_spec((1, LANES)),
                  _const_spec((1, GDN_DV))],
        out_specs=[pl.BlockSpec((group, tc, GDN_Z), tok),
                   pl.BlockSpec((group, GDN_HEADS, GDN_DK, GDN_DV), seq4),
                   pl.BlockSpec((group, SUBLANES, GDN_QKV), seq3)],
        out_shape=[jax.ShapeDtypeStruct((nseq, seqlen, GDN_Z), BF16),
                   jax.ShapeDtypeStruct((nseq, GDN_HEADS, GDN_DK, GDN_DV), F32),
                   jax.ShapeDtypeStruct((nseq, SUBLANES, GDN_QKV), F32)],
        scratch_shapes=[pltpu.VMEM((group, tc + SUBLANES, GDN_QKV), F32),
                        pltpu.VMEM((group, GDN_HEADS, tc, GDN_DK), F32),
                        pltpu.VMEM((group, GDN_HEADS, tc, GDN_DK), F32),
                        pltpu.VMEM((group, GDN_HEADS, tc, GDN_DV), F32),
                        pltpu.VMEM((group, GDN_HEADS, GDN_DK, GDN_DV), F32)],
        compiler_params=_compiler_params(2),
        name="gdn_seq",
    )(view(qkv), view(z), view(ba), conv0, s0, convw, alog, dtb, normw)
    return o.reshape(nseq * seqlen, GDN_Z), s_new, conv_new


def _mlstm_seq_kernel(q_ref, k_ref, v_ref, og_ref, gates_ref, c0_ref, n0_ref, m0_ref, ibias_ref, fbias_ref,
                      h_ref, cout_ref, nout_ref, mout_ref,
                      cx_s, m_s, *, tc, chunk, n_valid):
    t = pl.program_id(1)
    heads = ML_HEADS
    sub = tc // chunk
    last = min(chunk, n_valid) - 1

    @pl.when(t == 0)
    def _():
        cx_s[:, :, :ML_DV] = c0_ref[...]
        cx_s[:, :, ML_DV:] = n0_ref[...]
        m_s[...] = m0_ref[...]

    gates = gates_ref[...]
    ig = gates[:, :LANES] + ibias_ref[...]
    lf = -_softplus(-(gates[:, LANES:] + fbias_ref[...]))

    ri = lax.broadcasted_iota(jnp.int32, (tc, tc), 0)
    ci = lax.broadcasted_iota(jnp.int32, (tc, tc), 1)
    incl = ci <= ri
    if sub > 1:
        incl = incl & ((ri // chunk) == (ci // chunk))
    mask = incl & (ci < n_valid) if n_valid < chunk else incl
    col_rows = lax.broadcasted_iota(jnp.int32, (chunk, 1), 0)
    ones_col = (lax.broadcasted_iota(jnp.int32, (tc, LANES), 1) == 0).astype(F32)
    lane_row = lax.broadcasted_iota(jnp.int32, (1, LANES), 1)
    chunk_rows = [slice(s * chunk, (s + 1) * chunk) for s in range(sub)]
    last_rows = [slice(s * chunk + last, s * chunk + last + 1) for s in range(sub)]
    hs = range(heads)

    bcum = _dot_f32(incl.astype(F32), lf)
    bcum_t = bcum.T
    ig_t = ig.T
    dmats = [jnp.where(mask, bcum[:, h:h + 1] - bcum_t[h:h + 1, :] + ig_t[h:h + 1, :], -jnp.inf) for h in hs]
    dmaxs = [jnp.max(d, axis=-1, keepdims=True) for d in dmats]
    m_row = m_s[0:1, :]
    m_in = [m_row[:, h:h + 1] for h in hs]
    inters = [[] for _ in hs]
    m_ts = [[] for _ in hs]
    m_news = [[] for _ in hs]
    for r in chunk_rows:
        for h in hs:
            inter = bcum[r, h:h + 1] + m_in[h]
            m_t = jnp.maximum(inter, dmaxs[h][r])
            m_in[h] = m_t[last:last + 1, :]
            inters[h].append(inter)
            m_ts[h].append(m_t)
            m_news[h].append(m_in[h])
    inters = [jnp.concatenate(v, axis=0) for v in inters]
    m_ts = [jnp.concatenate(v, axis=0) for v in m_ts]
    w_intras = [jnp.exp(d - m) for d, m in zip(dmats, m_ts)]
    w_inters = [jnp.exp(i - m) for i, m in zip(inters, m_ts)]
    qs = [q_ref[:, h * ML_DK:(h + 1) * ML_DK] * ML_DK ** -0.5 for h in hs]
    ks = [k_ref[:, h * ML_DK:(h + 1) * ML_DK] for h in hs]
    vxs = [jnp.concatenate([v_ref[:, h * ML_DV:(h + 1) * ML_DV], ones_col], axis=1) for h in hs]
    scores = [_dot_nt(q, k) * w for q, k, w in zip(qs, ks, w_intras)]
    intra = [_dot(s, vx) for s, vx in zip(scores, vxs)]
    cxs = [cx_s[h] for h in hs]
    inter_num = [[] for _ in hs]
    for s, r in enumerate(chunk_rows):
        for h in hs:
            inter_num[h].append(_dot(qs[h][r], cxs[h]))
            m_new = m_news[h][s]
            w_state = jnp.exp(jnp.where(col_rows <= last, bcum[last_rows[s], h:h + 1] - bcum[r, h:h + 1]
                                        + ig[r, h:h + 1], -jnp.inf) - m_new)
            carry_decay = jnp.exp(inters[h][last_rows[s], :] - m_new)
            cxs[h] = carry_decay * cxs[h] + _dot_tn(ks[h][r] * w_state, vxs[h][r])
    m_next = m_row
    for h in hs:
        cx_s[h] = cxs[h]
        num = w_inters[h] * jnp.concatenate(inter_num[h], axis=0) + intra[h]
        den = num[:, ML_DV:ML_DV + 1]
        hh = num[:, :ML_DV] / jnp.maximum(jnp.abs(den), jnp.exp(-m_ts[h]))
        vcols = slice(h * ML_DV, (h + 1) * ML_DV)
        h_ref[:, vcols] = (hh * jax.nn.sigmoid(og_ref[:, vcols])).astype(h_ref.dtype)
        m_next = jnp.where(lane_row == h, m_in[h], m_next)
    m_s[0:1, :] = m_next

    @pl.when(t == pl.num_programs(1) - 1)
    def _():
        cout_ref[...] = cx_s[:, :, :ML_DV]
        nout_ref[...] = cx_s[:, :, ML_DV:]
        mout_ref[...] = m_s[...]


def _mlstm_seq(q, k, v, og, gates, c0, n0, m0, ibias, fbias, *, nseq, seqlen, tc, chunk, n_valid):
    nt = seqlen // tc
    row = lambda b, t: (b * nt + t, 0)
    seq3 = lambda b, t: (b, 0, 0)
    seq4 = lambda b, t: (b, 0, 0, 0)
    m = nseq * seqlen
    return pl.pallas_call(
        functools.partial(_mlstm_seq_kernel, tc=tc, chunk=chunk, n_valid=n_valid),
        grid=(nseq, nt),
        in_specs=[pl.BlockSpec((tc, ML_QK), row), pl.BlockSpec((tc, ML_QK), row),
                  pl.BlockSpec((tc, ML_V), row), pl.BlockSpec((tc, ML_V), row),
                  pl.BlockSpec((tc, 2 * LANES), row),
                  pl.BlockSpec((None, ML_HEADS, ML_DK, ML_DV), seq4),
                  pl.BlockSpec((None, ML_HEADS, ML_DK, LANES), seq4),
                  pl.BlockSpec((None, SUBLANES, LANES), seq3),
                  _const_spec((1, LANES)), _const_spec((1, LANES))],
        out_specs=[pl.BlockSpec((tc, ML_V), row),
                   pl.BlockSpec((None, ML_HEADS, ML_DK, ML_DV), seq4),
                   pl.BlockSpec((None, ML_HEADS, ML_DK, LANES), seq4),
                   pl.BlockSpec((None, SUBLANES, LANES), seq3)],
        out_shape=[jax.ShapeDtypeStruct((m, ML_V), BF16),
                   jax.ShapeDtypeStruct((nseq, ML_HEADS, ML_DK, ML_DV), F32),
                   jax.ShapeDtypeStruct((nseq, ML_HEADS, ML_DK, LANES), F32),
                   jax.ShapeDtypeStruct((nseq, SUBLANES, LANES), F32)],
        scratch_shapes=[pltpu.VMEM((ML_HEADS, ML_DK, ML_DV + LANES), F32),
                        pltpu.VMEM((SUBLANES, LANES), F32)],
        compiler_params=_compiler_params(2),
        name="mlstm_seq",
    )(q, k, v, og, gates, c0, n0, m0, ibias, fbias)


def _mla_proj_kernel(x_ref, g_ref, cos_ref, sin_ref, win_ref, qnorm_ref, kvnorm_ref, wuq_ref, wuqsw_ref, wkv_ref,
                     *o_refs, decode):
    xn = _rms(x_ref[...], g_ref[...]).astype(BF16)
    proj = jnp.dot(xn, win_ref[...], preferred_element_type=F32)
    cq = _rms(proj[:, :MLA_Q_RANK], qnorm_ref[...]).astype(BF16)
    ckv = _rms(proj[:, MLA_Q_RANK:MLA_Q_RANK + MLA_KV_RANK], kvnorm_ref[...])
    base = MLA_Q_RANK + MLA_KV_RANK
    cos = cos_ref[...]
    sin = sin_ref[...]
    kr = proj[:, base:base + LANES] * cos + proj[:, base + LANES:base + 2 * LANES] * sin
    q_main = jnp.dot(cq, wuq_ref[...], preferred_element_type=F32)
    q_swap = jnp.dot(cq, wuqsw_ref[...], preferred_element_type=F32)
    if decode:
        qlat_ref, qrope_ref, ckv_ref, kr_ref = o_refs
    else:
        qcat_ref, kcat_ref, vt_ref, ckv_ref, kr_ref = o_refs
        kv = jnp.dot(ckv.astype(BF16), wkv_ref[...], preferred_element_type=F32)
    ckv_ref[...] = ckv
    kr_ref[...] = kr[:, :MLA_ROPE]
    for h in range(MLA_HEADS):
        lo = h * 2 * LANES
        q_nope = q_main[:, lo:lo + LANES] * MLA_SCALE
        q_rope = (q_main[:, lo + LANES:lo + 2 * LANES] * cos + q_swap[:, h * LANES:(h + 1) * LANES] * sin) * MLA_SCALE
        if decode:
            qlat_ref[:, lo:lo + 2 * LANES] = _dot(q_nope, wkv_ref[h]).astype(BF16)
            qrope_ref[:, h * LANES:(h + 1) * LANES] = q_rope.astype(BF16)
        else:
            qcat_ref[:, lo:lo + LANES] = q_nope.astype(BF16)
            qcat_ref[:, lo + LANES:lo + 2 * LANES] = q_rope.astype(BF16)
            kcat_ref[:, lo:lo + LANES] = kv[:, lo:lo + LANES].astype(BF16)
            kcat_ref[:, lo + LANES:lo + 2 * LANES] = kr.astype(BF16)
            key_tile = vt_ref.shape[-1]
            for s in range(vt_ref.shape[0]):
                v_tile = kv[s * key_tile:(s + 1) * key_tile, lo + LANES:lo + 2 * LANES]
                vt_ref[s, h * MLA_V:(h + 1) * MLA_V, :] = v_tile.T.astype(BF16)


def _mla_proj(x, g, cos, sin, win, qnorm, kvnorm, wuq, wuqsw, wkv, *, tm, decode, key_tile=None):
    m = x.shape[0]
    n_pos_blocks = cos.shape[0] // tm
    row = lambda i: (i, 0)
    pos = lambda i: (i % n_pos_blocks, 0)
    wide = MLA_HEADS * 2 * LANES
    if decode:
        widths = (wide, MLA_HEADS * LANES, MLA_KV_RANK, MLA_ROPE)
        dtypes = (BF16, BF16, F32, F32)
    else:
        widths = (wide, wide, MLA_KV_RANK, MLA_ROPE)
        dtypes = (BF16, BF16, F32, F32)
    out_specs = [pl.BlockSpec((tm, n), row) for n in widths]
    out_shape = [jax.ShapeDtypeStruct((m, n), dt) for n, dt in zip(widths, dtypes)]
    if not decode:
        out_specs.insert(2, pl.BlockSpec((tm // key_tile, MLA_HEADS * MLA_V, key_tile), lambda i: (i, 0, 0)))
        out_shape.insert(2, jax.ShapeDtypeStruct((m // key_tile, MLA_HEADS * MLA_V, key_tile), BF16))
    return pl.pallas_call(
        functools.partial(_mla_proj_kernel, decode=decode),
        grid=(m // tm,),
        in_specs=[pl.BlockSpec((tm, D_MODEL), row), _const_spec((1, D_MODEL)),
                  pl.BlockSpec((tm, LANES), pos), pl.BlockSpec((tm, LANES), pos),
                  _const_spec(win.shape), _const_spec((1, MLA_Q_RANK)), _const_spec((1, MLA_KV_RANK)),
                  _const_spec(wuq.shape), _const_spec(wuqsw.shape), _const_spec(wkv.shape)],
        out_specs=out_specs,
        out_shape=out_shape,
        compiler_params=_compiler_params(1),
        name="mla_proj_decode" if decode else "mla_proj",
    )(x, g, cos, sin, win, qnorm, kvnorm, wuq, wuqsw, wkv)


def _mla_attn_kernel(q_ref, k_ref, vt_ref, o_ref, m_s, l_s, acc_s, *, tq):
    qi = pl.program_id(1)
    heads = MLA_HEADS
    wide = 2 * LANES
    m_s[...] = jnp.full(m_s.shape, -jnp.inf, F32)
    l_s[...] = jnp.zeros(l_s.shape, F32)
    acc_s[...] = jnp.zeros(acc_s.shape, F32)
    key_i = lax.broadcasted_iota(jnp.int32, (tq, tq), 0)
    query_i = lax.broadcasted_iota(jnp.int32, (tq, tq), 1)

    def key_tile(j, diagonal):
        rows = pl.ds(pl.multiple_of(j * tq, tq), tq)
        for g0 in range(0, heads, ATTN_HEAD_GROUP):
            hs = range(g0, g0 + ATTN_HEAD_GROUP)
            ss = [lax.dot_general(k_ref[rows, h * wide:(h + 1) * wide], q_ref[:, h * wide:(h + 1) * wide],
                                  (((1,), (1,)), ((), ())), preferred_element_type=F32) for h in hs]
            if diagonal:
                ss = [jnp.where(key_i <= query_i, s, -jnp.inf) for s in ss]
            m_old = [m_s[h] for h in hs]
            m_new = [jnp.maximum(m, jnp.max(s, axis=0, keepdims=True)) for m, s in zip(m_old, ss)]
            ps = [jnp.exp(s - m) for s, m in zip(ss, m_new)]
            alphas = [jnp.exp(mo - mn) for mo, mn in zip(m_old, m_new)]
            pvs = [jnp.dot(vt_ref[j, h * MLA_V:(h + 1) * MLA_V, :], p.astype(BF16), preferred_element_type=F32)
                   for p, h in zip(ps, hs)]
            for i, h in enumerate(hs):
                l_s[h] = alphas[i] * l_s[h] + jnp.sum(ps[i], axis=0, keepdims=True)
                acc_s[h] = alphas[i] * acc_s[h] + pvs[i]
                m_s[h] = m_new[i]

    def body(j, carry):
        key_tile(j, False)
        return carry

    lax.fori_loop(0, qi, body, 0)
    key_tile(qi, True)
    for h in range(heads):
        o_ref[:, h * MLA_V:(h + 1) * MLA_V] = (acc_s[h] / l_s[h]).T.astype(o_ref.dtype)


def _mla_attn(qcat, kcat, vt, *, nseq, seqlen, tq):
    nq = seqlen // tq
    m = nseq * seqlen
    return pl.pallas_call(
        functools.partial(_mla_attn_kernel, tq=tq),
        grid=(nseq, nq),
        in_specs=[pl.BlockSpec((tq, MLA_HEADS * 2 * LANES), lambda b, i: (b * nq + i, 0)),
                  pl.BlockSpec((seqlen, MLA_HEADS * 2 * LANES), lambda b, i: (b, 0)),
                  pl.BlockSpec((nq, MLA_HEADS * MLA_V, tq), lambda b, i: (b, 0, 0))],
        out_specs=pl.BlockSpec((tq, MLA_HEADS * MLA_V), lambda b, i: (b * nq + i, 0)),
        out_shape=jax.ShapeDtypeStruct((m, MLA_HEADS * MLA_V), BF16),
        scratch_shapes=[pltpu.VMEM((MLA_HEADS, 1, tq), F32), pltpu.VMEM((MLA_HEADS, 1, tq), F32),
                        pltpu.VMEM((MLA_HEADS, MLA_V, tq), F32)],
        compiler_params=_compiler_params(2),
        name="mla_attn",
    )(qcat, kcat, vt)


def _mla_decode_kernel(pt_ref, qlat_ref, qrope_ref, ckv_ref, kr_ref, wuv_ref, *rest, n_pages):
    lat_refs = rest[:n_pages]
    rope_refs = rest[n_pages:2 * n_pages]
    o_ref = rest[2 * n_pages]
    ql_s, qr_s, m_s, l_s, acc_s = rest[2 * n_pages + 1:]
    g = pl.program_id(1)
    heads = MLA_HEADS
    tok = DEC_PAD

    @pl.when(g == 0)
    def _():
        for h in range(heads):
            ql_s[h * tok:(h + 1) * tok, :] = qlat_ref[:, h * 2 * LANES:(h + 1) * 2 * LANES]
            qr_s[h * tok:(h + 1) * tok, :] = qrope_ref[:, h * LANES:(h + 1) * LANES]
        m_s[...] = jnp.full(m_s.shape, -jnp.inf, F32)
        l_s[...] = jnp.zeros(l_s.shape, F32)
        acc_s[...] = jnp.zeros(acc_s.shape, F32)

    ql = ql_s[...]
    qr = qr_s[:, :MLA_ROPE]

    def update(chains, scores, values):
        m_old = [m_s[c] for c in chains]
        m_new = [jnp.maximum(m, jnp.max(s, axis=-1, keepdims=True)) for m, s in zip(m_old, scores)]
        ps = [jnp.exp(s - m) for s, m in zip(scores, m_new)]
        alphas = [jnp.exp(mo - mn) for mo, mn in zip(m_old, m_new)]
        for i, c in enumerate(chains):
            l_s[c] = alphas[i] * l_s[c] + jnp.sum(ps[i], axis=-1, keepdims=True)
            pb = ps[i].astype(BF16)
            acc = alphas[i] * acc_s[c]
            for cols, val in values[i]:
                acc = acc + jnp.dot(pb[:, cols], val, preferred_element_type=F32)
            acc_s[c] = acc
            m_s[c] = m_new[i]

    n_chains = m_s.shape[0]
    per = n_pages // n_chains
    lats = [r[...].astype(BF16) for r in lat_refs]
    scores = [_dot_nt(ql, lat) + _dot(qr, r[...]) for lat, r in zip(lats, rope_refs)]
    update(range(n_chains),
           [jnp.concatenate(scores[c * per:(c + 1) * per], axis=1) for c in range(n_chains)],
           [[(slice(i * PAGE_SIZE, (i + 1) * PAGE_SIZE), lats[c * per + i]) for i in range(per)]
            for c in range(n_chains)])

    @pl.when(g == pl.num_programs(1) - 1)
    def _():
        ckv = ckv_ref[...].astype(BF16)
        s = _dot_nt(ql, ckv) + _dot_nt(qr, kr_ref[...])
        qt = lax.broadcasted_iota(jnp.int32, s.shape, 0) % tok
        kt = lax.broadcasted_iota(jnp.int32, s.shape, 1)
        update([0], [jnp.where(kt <= qt, s, -jnp.inf)], [[(slice(0, tok), ckv)]])
        m_all = m_s[0]
        for c in range(1, n_chains):
            m_all = jnp.maximum(m_all, m_s[c])
        weights = [jnp.exp(m_s[c] - m_all) for c in range(n_chains)]
        l_all = sum(w * l_s[c] for c, w in enumerate(weights))
        acc_all = sum(w * acc_s[c] for c, w in enumerate(weights))
        o_lat = acc_all / l_all
        for h in range(heads):
            o_ref[:, h * MLA_V:(h + 1) * MLA_V] = _dot(o_lat[h * tok:(h + 1) * tok, :], wuv_ref[h]).astype(o_ref.dtype)


def _mla_decode(page_table, qlat, qrope, ckv, kr, wuv, lat_pages, rope_pages, *, nseq):
    n_pages_seq = page_table.shape[1]
    n = PAGES_PER_STEP
    groups = n_pages_seq // n
    seq = lambda b, g, pt: (b, 0)

    def page_map(i):
        return lambda b, g, pt: (pt[b * n_pages_seq + g * n + i], 0, 0)

    rows = MLA_HEADS * DEC_PAD
    grid_spec = pltpu.PrefetchScalarGridSpec(
        num_scalar_prefetch=1,
        grid=(nseq, groups),
        in_specs=[pl.BlockSpec((DEC_PAD, MLA_HEADS * 2 * LANES), seq),
                  pl.BlockSpec((DEC_PAD, MLA_HEADS * LANES), seq),
                  pl.BlockSpec((DEC_PAD, MLA_KV_RANK), seq),
                  pl.BlockSpec((DEC_PAD, MLA_ROPE), seq),
                  pl.BlockSpec(wuv.shape, lambda b, g, pt: (0, 0, 0))]
                 + [pl.BlockSpec((None, PAGE_SIZE, MLA_KV_RANK), page_map(i)) for i in range(n)]
                 + [pl.BlockSpec((None, MLA_ROPE, PAGE_SIZE), page_map(i)) for i in range(n)],
        out_specs=pl.BlockSpec((DEC_PAD, MLA_HEADS * MLA_V), seq),
        scratch_shapes=[pltpu.VMEM((rows, MLA_KV_RANK), BF16), pltpu.VMEM((rows, LANES), BF16),
                        pltpu.VMEM((DECODE_CHAINS, rows, 1), F32), pltpu.VMEM((DECODE_CHAINS, rows, 1), F32),
                        pltpu.VMEM((DECODE_CHAINS, rows, MLA_KV_RANK), F32)],
    )
    return pl.pallas_call(
        functools.partial(_mla_decode_kernel, n_pages=n),
        grid_spec=grid_spec,
        out_shape=jax.ShapeDtypeStruct((nseq * DEC_PAD, MLA_HEADS * MLA_V), BF16),
        compiler_params=_compiler_params(2),
        name="mla_decode",
    )(page_table.reshape(-1), qlat, qrope, ckv, kr, wuv, *([lat_pages] * n), *([rope_pages] * n))


def _row(v, width=None):
    v = v.astype(F32).reshape(1, -1)
    if width is not None and v.shape[1] < width:
        v = jnp.pad(v, ((0, 0), (0, width - v.shape[1])))
    return v


def _pad_cols(w, width):
    return jnp.pad(w, ((0, 0), (0, width - w.shape[1])))


def _prep_ffn(w_up, w_down):
    return w_up.astype(BF16), w_down.reshape(w_down.shape[:2] + (N_FF_CHUNKS, FF_CHUNK, D_MODEL)).astype(BF16)


def _prep_gdn(w_in, conv_w, a_log, dt_bias, norm_w, w_out):
    main = GDN_QKV + GDN_Z
    w = jnp.concatenate([w_in[:, :main], _pad_cols(w_in[:, main:main + GDN_HEADS], LANES),
                         _pad_cols(w_in[:, main + GDN_HEADS:], LANES)], axis=1).astype(BF16)
    convw = jnp.pad(conv_w.astype(F32), ((0, SUBLANES - GDN_CONV), (0, 0)))
    return w, convw, _row(a_log, LANES), _row(dt_bias, LANES), _row(norm_w), w_out.astype(BF16)


def _prep_mlstm(w_in, i_bias, f_bias, w_out):
    main = 2 * ML_QK + 2 * ML_V
    w = jnp.concatenate([w_in[:, :main], _pad_cols(w_in[:, main:main + ML_HEADS], LANES),
                         _pad_cols(w_in[:, main + ML_HEADS:], LANES)], axis=1).astype(BF16)
    return w, _row(i_bias, LANES), _row(f_bias, LANES), w_out.astype(BF16)


def _swap_halves(w):
    half = w.shape[-1] // 2
    return jnp.concatenate([w[..., half:], w[..., :half]], axis=-1)


def _prep_mla(w_in, q_norm, kv_norm, w_uq, w_ukv, w_out):
    base = MLA_Q_RANK + MLA_KV_RANK
    kr = w_in[:, base:]
    win = jnp.concatenate([w_in[:, :base], _pad_cols(kr, LANES), _pad_cols(_swap_halves(kr), LANES)],
                          axis=1).astype(BF16)
    wq = w_uq.reshape(MLA_Q_RANK, MLA_HEADS, MLA_NOPE + MLA_ROPE)
    nope, rope = wq[..., :MLA_NOPE], wq[..., MLA_NOPE:]
    zeros = jnp.zeros((MLA_Q_RANK, MLA_HEADS, LANES - MLA_ROPE), w_uq.dtype)
    wuq = jnp.concatenate([nope, rope, zeros], axis=-1).reshape(MLA_Q_RANK, -1).astype(BF16)
    wuqsw = jnp.concatenate([_swap_halves(rope), zeros], axis=-1).reshape(MLA_Q_RANK, -1).astype(BF16)
    wkv3 = w_ukv.reshape(MLA_KV_RANK, MLA_HEADS, MLA_NOPE + MLA_V)
    wuk_t = wkv3[..., :MLA_NOPE].transpose(1, 2, 0).astype(BF16)
    wuv = wkv3[..., MLA_NOPE:].transpose(1, 0, 2).astype(BF16)
    return dict(win=win, qnorm=_row(q_norm), kvnorm=_row(kv_norm), wuq=wuq, wuqsw=wuqsw,
                wkv=w_ukv.astype(BF16), wuk_t=wuk_t, wuv=wuv, wout=w_out.astype(BF16))


def _rope_tables(pos):
    half = MLA_ROPE // 2
    inv = ROPE_THETA ** (-jnp.arange(half, dtype=F32) / half)
    ang = pos.astype(F32)[:, None] * inv
    cos, sin = jnp.cos(ang), jnp.sin(ang)
    pad = jnp.zeros((pos.shape[0], LANES - MLA_ROPE), F32)
    return jnp.concatenate([cos, cos, pad], axis=1), jnp.concatenate([-sin, sin, pad], axis=1)


def _tiles(nseq, seqlen):
    m = nseq * seqlen
    return dict(ffn=min(512, m), proj=min(512, m), mla_proj=min(512, seqlen), attn_q=min(256, seqlen),
                gdn=min(2 * GDN_CHUNK, seqlen), gdn_group=math.gcd(GDN_SEQ_GROUP, nseq),
                mlstm=min(2 * ML_CHUNK, seqlen))


def _trunk(x, *, nseq, seqlen, n_tokens, decode, states, page_table, weights):
    gdn_S, gdn_conv, mla_lat, mla_rope, ml_C, ml_n, ml_m = states
    chunk_g = min(GDN_CHUNK, seqlen)
    chunk_m = min(ML_CHUNK, seqlen)
    tiles = _tiles(nseq, seqlen)
    tm = tiles["ffn"]
    new = ([], [], [], [], [], [], [])
    counts = [0, 0, 0]
    for layer in range(4):
        kind = layer % 3
        j = counts[kind]
        counts[kind] += 1
        ng = weights["gains"][layer]
        wup, wdn = weights["ffn"]
        x = _ffn_half(x, ng[0], ng[1], wup, wdn, layer, 0, tm)
        if kind == 0:
            w, convw, alog, dtb, normw, wout = weights["gdn"][j]
            qkv, z, ba = _norm_matmul(x, ng[2], w, (GDN_QKV, GDN_Z, 2 * LANES), tiles["proj"])
            o, s_new, conv_new = _gdn_seq(qkv, z, ba, gdn_conv, gdn_S, convw, alog, dtb, normw, layer=j,
                                          nseq=nseq, seqlen=seqlen, tc=tiles["gdn"], chunk=chunk_g,
                                          n_valid=min(tiles["gdn"], n_tokens), group=tiles["gdn_group"])
            new[0].append(s_new)
            new[1].append(conv_new[:, SUBLANES - (GDN_CONV - 1):, :])
        elif kind == 1:
            p = weights["mla"][j]
            cos, sin = weights["rope"]
            if decode:
                qlat, qrope, ckv, kr = _mla_proj(x, ng[2], cos, sin, p["win"], p["qnorm"], p["kvnorm"], p["wuq"],
                                                 p["wuqsw"], p["wuk_t"], tm=tiles["mla_proj"], decode=True)
                o = _mla_decode(page_table, qlat, qrope, ckv, kr, p["wuv"], mla_lat[j], mla_rope[j], nseq=nseq)
            else:
                qcat, kcat, vt, ckv, kr = _mla_proj(x, ng[2], cos, sin, p["win"], p["qnorm"], p["kvnorm"], p["wuq"],
                                                    p["wuqsw"], p["wkv"], tm=tiles["mla_proj"], decode=False,
                                                    key_tile=tiles["attn_q"])
                o = _mla_attn(qcat, kcat, vt, nseq=nseq, seqlen=seqlen, tq=tiles["attn_q"])
            wout = p["wout"]
            new[2].append(ckv)
            new[3].append(kr)
        else:
            w, ibias, fbias, wout = weights["mlstm"][j]
            q, k, v, og, gates = _norm_matmul(x, ng[2], w, (ML_QK, ML_QK, ML_V, ML_V, 2 * LANES), tiles["proj"])
            o, c_new, n_new, m_new = _mlstm_seq(q, k, v, og, gates, ml_C[j], ml_n[j], ml_m[j], ibias, fbias,
                                                nseq=nseq, seqlen=seqlen, tc=tiles["mlstm"], chunk=chunk_m,
                                                n_valid=min(tiles["mlstm"], n_tokens))
            new[4].append(c_new)
            new[5].append(n_new[..., 0])
            new[6].append(m_new[:, 0, :ML_HEADS])
        x = _ffn_half(x, ng[4], ng[5], wup, wdn, layer, 1, tm, mixer_out=(o, ng[3], wout))
    return x, [jnp.stack(s) for s in new]


def kernel(x_prompt, x_sample, state_gdn_S, state_gdn_conv, cache_mla_latent, cache_mla_rope, state_mlstm_C, state_mlstm_n, state_mlstm_m, page_table, norm_gains, w_ffn_up, w_ffn_down, gdn_w_in, gdn_conv_w, gdn_a_log, gdn_dt_bias, gdn_norm_w, gdn_w_out, mla_w_in, mla_q_norm, mla_kv_norm, mla_w_uq, mla_w_ukv, mla_w_out, mlstm_w_in, mlstm_i_bias, mlstm_f_bias, mlstm_w_out):
    nb, seq, _ = x_prompt.shape
    db, dseq, _ = x_sample.shape
    n_gdn, n_mla, n_ml = gdn_w_in.shape[0], mla_w_in.shape[0], mlstm_w_in.shape[0]
    past = page_table.shape[1] * PAGE_SIZE

    weights = dict(
        gains=[[_row(norm_gains[l, i]) for i in range(6)] for l in range(4)],
        ffn=_prep_ffn(w_ffn_up, w_ffn_down),
        gdn=[_prep_gdn(gdn_w_in[j], gdn_conv_w[j], gdn_a_log[j], gdn_dt_bias[j], gdn_norm_w[j], gdn_w_out[j])
             for j in range(n_gdn)],
        mla=[_prep_mla(mla_w_in[j], mla_q_norm[j], mla_kv_norm[j], mla_w_uq[j], mla_w_ukv[j], mla_w_out[j])
             for j in range(n_mla)],
        mlstm=[_prep_mlstm(mlstm_w_in[j], mlstm_i_bias[j], mlstm_f_bias[j], mlstm_w_out[j]) for j in range(n_ml)],
    )

    zeros = lambda *s: jnp.zeros(s, F32)
    states_p = (zeros(n_gdn, nb, GDN_HEADS, GDN_DK, GDN_DV), zeros(n_gdn, nb, SUBLANES, GDN_QKV), None, None,
                zeros(n_ml, nb, ML_HEADS, ML_DK, ML_DV), zeros(n_ml, nb, ML_HEADS, ML_DK, LANES),
                zeros(n_ml, nb, SUBLANES, LANES))
    y_p, st_p = _trunk(x_prompt.reshape(nb * seq, D_MODEL), nseq=nb, seqlen=seq, n_tokens=seq, decode=False,
                       states=states_p, page_table=None,
                       weights=dict(weights, rope=_rope_tables(jnp.arange(seq))))

    pad_t = DEC_PAD - dseq
    x_s = jnp.pad(x_sample, ((0, 0), (0, pad_t), (0, 0))).reshape(db * DEC_PAD, D_MODEL)
    conv0 = jnp.pad(state_gdn_conv, ((0, 0), (0, 0), (SUBLANES - (GDN_CONV - 1), 0), (0, 0)))
    n0 = jnp.pad(state_mlstm_n[..., None], ((0, 0),) * 4 + ((0, LANES - 1),))
    m0 = jnp.pad(state_mlstm_m[:, :, None, :], ((0, 0), (0, 0), (0, SUBLANES - 1), (0, LANES - ML_HEADS)))
    states_s = (state_gdn_S, conv0,
                cache_mla_latent.reshape((n_mla, -1) + cache_mla_latent.shape[2:]),
                jnp.swapaxes(cache_mla_rope.reshape((n_mla, -1) + cache_mla_rope.shape[2:]), -1, -2),
                state_mlstm_C, n0, m0)
    pos_s = jnp.tile(past + jnp.arange(DEC_PAD), db)
    y_s, st_s = _trunk(x_s, nseq=db, seqlen=DEC_PAD, n_tokens=dseq, decode=True, states=states_s,
                       page_table=page_table, weights=dict(weights, rope=_rope_tables(pos_s)))

    gdn_S_p, gdn_conv_p, lat_p, rope_p, ml_C_p, ml_n_p, ml_m_p = st_p
    gdn_S_s, gdn_conv_s, lat_s, rope_s, ml_C_s, ml_n_s, ml_m_s = st_s
    unpad = lambda a: a.reshape(a.shape[0], db, DEC_PAD, a.shape[-1])[:, :, :dseq]
    return (y_p.reshape(nb, seq, D_MODEL), unpad(y_s[None])[0],
            gdn_S_p, gdn_S_s, gdn_conv_p, gdn_conv_s,
            lat_p.reshape(n_mla, -1, PAGE_SIZE, MLA_KV_RANK), unpad(lat_s),
            rope_p.reshape(n_mla, -1, PAGE_SIZE, MLA_ROPE), unpad(rope_s),
            ml_C_p, ml_C_s, ml_n_p, ml_n_s, ml_m_p, ml_m_s)
```

```python
import functools
import math

import jax
import jax.numpy as jnp
from jax import lax
from jax.experimental import pallas as pl
from jax.experimental.pallas import tpu as pltpu

F32 = jnp.float32
BF16 = jnp.bfloat16
HIGHEST = lax.Precision.HIGHEST

D_MODEL = 1024
PAGE_SIZE = 128
EPS = 1e-6

GDN_HEADS = 8
GDN_DK = 128
GDN_DV = 128
GDN_CONV = 4
GDN_CHUNK = 64
GDN_QKV = GDN_HEADS * (2 * GDN_DK + GDN_DV)
GDN_Z = GDN_HEADS * GDN_DV

MLA_HEADS = 8
MLA_NOPE = 128
MLA_ROPE = 64
MLA_V = 128
MLA_Q_RANK = 384
MLA_KV_RANK = 256
MLA_SCALE = (MLA_NOPE + MLA_ROPE) ** -0.5
ROPE_THETA = 10000.0

ML_HEADS = 4
ML_DK = 128
ML_DV = 256
ML_CHUNK = 64
ML_QK = ML_HEADS * ML_DK
ML_V = ML_HEADS * ML_DV

D_FF = 2816
FF_CHUNK = 256
N_FF_CHUNKS = D_FF // FF_CHUNK

LANES = 128
SUBLANES = 8
DEC_PAD = SUBLANES
VMEM_LIMIT = 56 * 1024 * 1024
PAGES_PER_STEP = 16
DECODE_CHAINS = 4
ATTN_HEAD_GROUP = 8
GDN_SEQ_GROUP = 2
SOLVE_REFINEMENTS = 1


def _rms(x, g):
    return x * lax.rsqrt(jnp.mean(x * x, axis=-1, keepdims=True) + EPS) * g


def _silu(x):
    return x * jax.nn.sigmoid(x)


def _softplus(x):
    return jnp.maximum(x, 0.0) + jnp.log1p(jnp.exp(-jnp.abs(x)))


def _dot(a, b):
    return jnp.dot(a.astype(BF16), b.astype(BF16), preferred_element_type=F32)


def _dot_nt(a, b):
    return lax.dot_general(a.astype(BF16), b.astype(BF16), (((1,), (1,)), ((), ())),
                           preferred_element_type=F32)


def _dot_tn(a, b):
    return lax.dot_general(a.astype(BF16), b.astype(BF16), (((0,), (0,)), ((), ())),
                           preferred_element_type=F32)


def _dot_f32(a, b):
    return jnp.dot(a, b, precision=HIGHEST, preferred_element_type=F32)


def _split(x):
    hi = x.astype(BF16)
    return hi, (x - hi.astype(F32)).astype(BF16)


def _dot3(a, b):
    (a_hi, a_lo), (b_hi, b_lo) = a, b
    if a_hi.shape[1] % LANES == 0:
        return jnp.dot(jnp.concatenate([a_hi, a_lo, a_hi], axis=1), jnp.concatenate([b_hi, b_hi, b_lo], axis=0),
                       preferred_element_type=F32)
    m = a_hi.shape[0]
    r = jnp.dot(jnp.concatenate([a_hi, a_lo], axis=0), b_hi, preferred_element_type=F32)
    return r[:m] + r[m:] + jnp.dot(a_hi, b_lo, preferred_element_type=F32)


def _unit_lower_inverse_minus_eye(a_mats, ri, ci, chunk):
    rs = None
    s = 1
    while s < chunk:
        shift = s.bit_length() - 1
        off = ((ri >> (shift + 1)) == (ci >> (shift + 1))) & ((ri >> shift) != (ci >> shift))
        a_offs = [jnp.where(off, a, 0.0) for a in a_mats]
        if rs is None:
            rs = [-a for a in a_offs]
        else:
            bs = [a + _dot(r, a) for a, r in zip(a_offs, rs)]
            rs = [r - b - _dot(b, r) for r, b in zip(rs, bs)]
        s *= 2
    return rs


def _unit_lower_solve(a_mats, rhss, ri, ci, chunk):
    rs = _unit_lower_inverse_minus_eye(a_mats, ri, ci, chunk)
    a_sp = [_split(a) for a in a_mats]
    xs = [rhs + _dot(r, rhs) for r, rhs in zip(rs, rhss)]
    for _ in range(SOLVE_REFINEMENTS):
        resid = [rhs - x - _dot3(a, _split(x)) for rhs, x, a in zip(rhss, xs, a_sp)]
        xs = [x + e + _dot(r, e) for x, e, r in zip(xs, resid, rs)]
    return xs


def _compiler_params(n_axes):
    return pltpu.CompilerParams(dimension_semantics=("arbitrary",) * n_axes,
                                vmem_limit_bytes=VMEM_LIMIT)


SINGLE_BUFFER = pl.Buffered(1)


def _const_spec(shape):
    nd = len(shape)
    return pl.BlockSpec(shape, lambda *_: (0,) * nd, pipeline_mode=SINGLE_BUFFER)


def _ffn_kernel(*refs, mixer_out):
    if mixer_out:
        a_ref, gmix_ref, wout_ref, x_ref, gpre_ref, gpost_ref, wup_ref, wdn_ref, o_ref = refs
        x = x_ref[...] + _rms(jnp.dot(a_ref[...], wout_ref[...], preferred_element_type=F32), gmix_ref[...])
    else:
        x_ref, gpre_ref, gpost_ref, wup_ref, wdn_ref, o_ref = refs
        x = x_ref[...]
    xn = _rms(x, gpre_ref[...]).astype(BF16)
    acc = jnp.zeros(x.shape, F32)
    for c in range(N_FF_CHUNKS):
        lo = c * FF_CHUNK
        gate = jnp.dot(xn, wup_ref[:, lo:lo + FF_CHUNK], preferred_element_type=F32)
        up = jnp.dot(xn, wup_ref[:, D_FF + lo:D_FF + lo + FF_CHUNK], preferred_element_type=F32)
        act = (_silu(gate) * up).astype(BF16)
        acc = acc + jnp.dot(act, wdn_ref[c], preferred_element_type=F32)
    o_ref[...] = x + 0.5 * _rms(acc, gpost_ref[...])


def _ffn_half(x, gpre, gpost, wup, wdn, layer, half, tm, mixer_out=None):
    m = x.shape[0]
    row = pl.BlockSpec((tm, D_MODEL), lambda i: (i, 0))
    gain = _const_spec((1, D_MODEL))
    in_specs = [row, gain, gain,
                pl.BlockSpec((None, None) + wup.shape[2:], lambda i: (layer, half, 0, 0),
                             pipeline_mode=SINGLE_BUFFER),
                pl.BlockSpec((None, None) + wdn.shape[2:], lambda i: (layer, half, 0, 0, 0),
                             pipeline_mode=SINGLE_BUFFER)]
    args = (x, gpre, gpost, wup, wdn)
    if mixer_out is not None:
        a, gmix, wout = mixer_out
        in_specs = [row, gain, _const_spec(wout.shape)] + in_specs
        args = (a, gmix, wout) + args
    return pl.pallas_call(
        functools.partial(_ffn_kernel, mixer_out=mixer_out is not None),
        grid=(m // tm,),
        in_specs=in_specs,
        out_specs=row,
        out_shape=jax.ShapeDtypeStruct((m, D_MODEL), F32),
        compiler_params=_compiler_params(1),
        name="ffn_half",
    )(*args)


def _norm_matmul_kernel(x_ref, g_ref, w_ref, *o_refs, splits):
    xn = _rms(x_ref[...], g_ref[...]).astype(BF16)
    off = 0
    for o_ref, n in zip(o_refs, splits):
        o_ref[...] = jnp.dot(xn, w_ref[:, off:off + n], preferred_element_type=F32)
        off += n


def _norm_matmul(x, g, w, splits, tm):
    m = x.shape[0]
    return pl.pallas_call(
        functools.partial(_norm_matmul_kernel, splits=splits),
        grid=(m // tm,),
        in_specs=[pl.BlockSpec((tm, D_MODEL), lambda i: (i, 0)),
                  _const_spec((1, D_MODEL)), _const_spec(w.shape)],
        out_specs=[pl.BlockSpec((tm, n), lambda i: (i, 0)) for n in splits],
        out_shape=[jax.ShapeDtypeStruct((m, n), F32) for n in splits],
        compiler_params=_compiler_params(1),
        name="norm_matmul",
    )(x, g, w)


def _gdn_proj_kernel(x_ref, g_ref, w_ref, convw_ref, conv0_ref, q_ref, k_ref, v_ref, z_ref, ba_ref, convout_ref,
                     xext, *, seg, n_valid, tiles_per_seq):
    tm = x_ref.shape[0]
    nsub = tm // seg
    prev = SUBLANES - (GDN_CONV - 1)
    xn = _rms(x_ref[...], g_ref[...]).astype(BF16)
    if tiles_per_seq == 1:
        xext[:, 0:SUBLANES, :] = conv0_ref[...]
    else:
        @pl.when(pl.program_id(0) % tiles_per_seq == 0)
        def _():
            xext[:, 0:SUBLANES, :] = conv0_ref[...]
    for part, dest in enumerate((q_ref, k_ref, v_ref)):
        for h0 in range(0, GDN_HEADS, 2):
            base = part * GDN_HEADS * GDN_DK + h0 * GDN_DK
            pre2 = jnp.dot(xn, w_ref[:, base:base + 2 * GDN_DK], preferred_element_type=F32)
            for h in (h0, h0 + 1):
                c0 = part * GDN_HEADS * GDN_DK + h * GDN_DK
                cols = slice(c0, c0 + GDN_DK)
                pre = pre2[:, c0 - base:c0 - base + GDN_DK].reshape(nsub, seg, GDN_DK)
                xext[:, SUBLANES:SUBLANES + seg, cols] = pre
                y = pre * convw_ref[GDN_CONV - 1:GDN_CONV, cols]
                for j in range(GDN_CONV - 1):
                    y = y + xext[:, prev + j:prev + j + seg, cols] * convw_ref[j:j + 1, cols]
                y = _silu(y)
                if part < 2:
                    y = y * lax.rsqrt(jnp.sum(y * y, axis=-1, keepdims=True) + EPS)
                if part == 0:
                    y = y * GDN_DK ** -0.5
                dest[:, h * GDN_DK:(h + 1) * GDN_DK] = y.reshape(tm, GDN_DK)
    z_ref[...] = jnp.dot(xn, w_ref[:, GDN_QKV:GDN_QKV + GDN_Z], preferred_element_type=F32)
    ba_ref[...] = jnp.dot(xn, w_ref[:, GDN_QKV + GDN_Z:], preferred_element_type=F32)
    convout_ref[...] = xext[:, n_valid:n_valid + SUBLANES, :]
    if tiles_per_seq > 1:
        xext[:, 0:SUBLANES, :] = xext[:, seg:seg + SUBLANES, :]


def _gdn_proj(x, g, w, convw, conv0, *, layer, nseq, seqlen, n_tokens, tm):
    m = x.shape[0]
    seg = min(tm, seqlen)
    nsub = tm // seg
    tiles_per_seq = seqlen // seg
    row = lambda n: pl.BlockSpec((tm, n), lambda i: (i, 0))
    per_seq = lambda i: (i // tiles_per_seq, 0, 0)
    return pl.pallas_call(
        functools.partial(_gdn_proj_kernel, seg=seg, n_valid=min(seg, n_tokens), tiles_per_seq=tiles_per_seq),
        grid=(m // tm,),
        in_specs=[row(D_MODEL), _const_spec((1, D_MODEL)), _const_spec(w.shape), _const_spec((SUBLANES, GDN_QKV)),
                  pl.BlockSpec((None, nsub, SUBLANES, GDN_QKV), lambda i: (layer,) + per_seq(i))],
        out_specs=[row(GDN_Z), row(GDN_Z), row(GDN_Z), row(GDN_Z), row(2 * LANES),
                   pl.BlockSpec((nsub, SUBLANES, GDN_QKV), per_seq)],
        out_shape=[jax.ShapeDtypeStruct((m, GDN_Z), F32)] * 4 + [jax.ShapeDtypeStruct((m, 2 * LANES), F32),
                                                                jax.ShapeDtypeStruct((nseq, SUBLANES, GDN_QKV), F32)],
        scratch_shapes=[pltpu.VMEM((nsub, seg + SUBLANES, GDN_QKV), F32)],
        compiler_params=_compiler_params(1),
        name="gdn_proj",
    )(x, g, w, convw, conv0)


def _gdn_seq_kernel(q_ref, k_ref, v_ref, z_ref, ba_ref, s0_ref, alog_ref, dtb_ref, normw_ref,
                    o_ref, sout_ref, s_s, *, tc, chunk, n_valid):
    t = pl.program_id(1)
    heads = GDN_HEADS
    group = q_ref.shape[0]
    head_cols = lambda h: slice(h * GDN_DK, (h + 1) * GDN_DK)
    sub = tc // chunk
    last = min(chunk, n_valid) - 1

    @pl.when(t == 0)
    def _():
        s_s[...] = s0_ref[...]

    ri = lax.broadcasted_iota(jnp.int32, (tc, tc), 0)
    ci = lax.broadcasted_iota(jnp.int32, (tc, tc), 1)
    incl = ci <= ri
    if sub > 1:
        incl = incl & ((ri // chunk) == (ci // chunk))
    strict = incl & (ci < ri)
    tril = incl.astype(F32)
    chunk_rows = [slice(s * chunk, (s + 1) * chunk) for s in range(sub)]
    beta, gc, gc_t, exp_gc, exp_last, exp_rest = [], [], [], [], [], []
    for p in range(group):
        ba = ba_ref[p]
        b = jax.nn.sigmoid(ba[:, :LANES])
        if n_valid < tc:
            b = jnp.where(lax.broadcasted_iota(jnp.int32, b.shape, 0) < n_valid, b, 0.0)
        g = -jnp.exp(alog_ref[...]) * _softplus(ba[:, LANES:] + dtb_ref[...])
        c = _dot_f32(tril, g)
        c_last = [c[s * chunk + last:s * chunk + last + 1, :] for s in range(sub)]
        beta.append(b)
        gc.append(c)
        gc_t.append(c.T)
        exp_gc.append(jnp.exp(c))
        exp_last.append([jnp.exp(v) for v in c_last])
        exp_rest.append(jnp.exp(jnp.concatenate([jnp.broadcast_to(v, (chunk, LANES)) for v in c_last], axis=0) - c))
    normw = normw_ref[...]

    units = [(p, h) for p in range(group) for h in range(heads)]
    col = lambda x, h: x[:, h:h + 1]
    ks = {(p, h): k_ref[p, :, head_cols(h)] for p, h in units}
    bs = {(p, h): col(beta[p], h) for p, h in units}
    kbs = {u: ks[u] * bs[u] for u in units}
    decays = {(p, h): jnp.exp(jnp.where(incl, col(gc[p], h) - gc_t[p][h:h + 1, :], -jnp.inf)) for p, h in units}
    a_mats = [_dot_nt(kbs[u], ks[u]) * jnp.where(strict, decays[u], 0.0) for u in units]
    rhss = [jnp.concatenate([v_ref[p, :, head_cols(h)] * bs[p, h], kbs[p, h] * col(exp_gc[p], h)], axis=1) for p, h in units]
    sols = dict(zip(units, _unit_lower_solve(a_mats, rhss, ri, ci, chunk)))
    qs = {(p, h): q_ref[p, :, head_cols(h)] for p, h in units}
    attns = {u: _dot_nt(qs[u], ks[u]) * decays[u] for u in units}
    qgs = {(p, h): qs[p, h] * col(exp_gc[p], h) for p, h in units}
    kdecs = {(p, h): ks[p, h] * col(exp_rest[p], h) for p, h in units}
    states = {u: s_s[u] for u in units}
    v_new = {u: [] for u in units}
    o_inter = {u: [] for u in units}
    for s, r in enumerate(chunk_rows):
        for p, h in units:
            u = (p, h)
            vn = sols[u][r, :GDN_DV] - _dot(sols[u][r, GDN_DV:], states[u])
            o_inter[u].append(_dot(qgs[u][r], states[u]))
            states[u] = states[u] * col(exp_last[p][s], h) + _dot_tn(kdecs[u][r], vn)
            v_new[u].append(vn)
    for p, h in units:
        u = (p, h)
        s_s[u] = states[u]
        o = jnp.concatenate(o_inter[u], axis=0) + _dot(attns[u], jnp.concatenate(v_new[u], axis=0))
        cols = slice(h * GDN_DV, (h + 1) * GDN_DV)
        o_ref[p, :, cols] = (_rms(o, normw) * _silu(z_ref[p, :, cols])).astype(o_ref.dtype)

    @pl.when(t == pl.num_programs(1) - 1)
    def _():
        sout_ref[...] = s_s[...]


def _gdn_seq(q, k, v, z, ba, s0, alog, dtb, normw, *, layer, nseq, seqlen, tc, chunk, n_valid, group):
    nt = seqlen // tc
    tok = lambda b, t: (b, t, 0)
    view = lambda a: a.reshape(nseq, seqlen, a.shape[-1])
    wide = pl.BlockSpec((group, tc, GDN_Z), tok)
    o, s_new = pl.pallas_call(
        functools.partial(_gdn_seq_kernel, tc=tc, chunk=chunk, n_valid=n_valid),
        grid=(nseq // group, nt),
        in_specs=[wide, wide, wide, wide, pl.BlockSpec((group, tc, 2 * LANES), tok),
                  pl.BlockSpec((None, group, GDN_HEADS, GDN_DK, GDN_DV), lambda b, t: (layer, b, 0, 0, 0)),
                  _const_spec((1, LANES)), _const_spec((1, LANES)), _const_spec((1, GDN_DV))],
        out_specs=[wide, pl.BlockSpec((group, GDN_HEADS, GDN_DK, GDN_DV), lambda b, t: (b, 0, 0, 0))],
        out_shape=[jax.ShapeDtypeStruct((nseq, seqlen, GDN_Z), BF16),
                   jax.ShapeDtypeStruct((nseq, GDN_HEADS, GDN_DK, GDN_DV), F32)],
        scratch_shapes=[pltpu.VMEM((group, GDN_HEADS, GDN_DK, GDN_DV), F32)],
        compiler_params=_compiler_params(2),
        name="gdn_seq",
    )(view(q), view(k), view(v), view(z), view(ba), s0, alog, dtb, normw)
    return o.reshape(nseq * seqlen, GDN_Z), s_new


def _mlstm_seq_kernel(q_ref, k_ref, v_ref, og_ref, gates_ref, c0_ref, n0_ref, m0_ref, ibias_ref, fbias_ref,
                      h_ref, cout_ref, nout_ref, mout_ref,
                      cx_s, m_s, *, tc, chunk, n_valid):
    t = pl.program_id(1)
    heads = ML_HEADS
    sub = tc // chunk
    last = min(chunk, n_valid) - 1

    @pl.when(t == 0)
    def _():
        cx_s[:, :, :ML_DV] = c0_ref[...]
        cx_s[:, :, ML_DV:] = n0_ref[...]
        m_s[...] = m0_ref[...]

    gates = gates_ref[...]
    ig = gates[:, :LANES] + ibias_ref[...]
    lf = -_softplus(-(gates[:, LANES:] + fbias_ref[...]))

    ri = lax.broadcasted_iota(jnp.int32, (tc, tc), 0)
    ci = lax.broadcasted_iota(jnp.int32, (tc, tc), 1)
    incl = ci <= ri
    if sub > 1:
        incl = incl & ((ri // chunk) == (ci // chunk))
    mask = incl & (ci < n_valid) if n_valid < chunk else incl
    col_rows = lax.broadcasted_iota(jnp.int32, (chunk, 1), 0)
    ones_col = (lax.broadcasted_iota(jnp.int32, (tc, LANES), 1) == 0).astype(F32)
    lane_row = lax.broadcasted_iota(jnp.int32, (1, LANES), 1)
    chunk_rows = [slice(s * chunk, (s + 1) * chunk) for s in range(sub)]
    last_rows = [slice(s * chunk + last, s * chunk + last + 1) for s in range(sub)]
    hs = range(heads)

    bcum = _dot_f32(incl.astype(F32), lf)
    bcum_t = bcum.T
    ig_t = ig.T
    dmats = [jnp.where(mask, bcum[:, h:h + 1] - bcum_t[h:h + 1, :] + ig_t[h:h + 1, :], -jnp.inf) for h in hs]
    dmaxs = [jnp.max(d, axis=-1, keepdims=True) for d in dmats]
    m_row = m_s[0:1, :]
    m_in = [m_row[:, h:h + 1] for h in hs]
    inters = [[] for _ in hs]
    m_ts = [[] for _ in hs]
    m_news = [[] for _ in hs]
    for r in chunk_rows:
        for h in hs:
            inter = bcum[r, h:h + 1] + m_in[h]
            m_t = jnp.maximum(inter, dmaxs[h][r])
            m_in[h] = m_t[last:last + 1, :]
            inters[h].append(inter)
            m_ts[h].append(m_t)
            m_news[h].append(m_in[h])
    inters = [jnp.concatenate(v, axis=0) for v in inters]
    m_ts = [jnp.concatenate(v, axis=0) for v in m_ts]
    w_intras = [jnp.exp(d - m) for d, m in zip(dmats, m_ts)]
    w_inters = [jnp.exp(i - m) for i, m in zip(inters, m_ts)]
    qs = [q_ref[:, h * ML_DK:(h + 1) * ML_DK] * ML_DK ** -0.5 for h in hs]
    ks = [k_ref[:, h * ML_DK:(h + 1) * ML_DK] for h in hs]
    vxs = [jnp.concatenate([v_ref[:, h * ML_DV:(h + 1) * ML_DV], ones_col], axis=1) for h in hs]
    scores = [_dot_nt(q, k) * w for q, k, w in zip(qs, ks, w_intras)]
    intra = [_dot(s, vx) for s, vx in zip(scores, vxs)]
    cxs = [cx_s[h] for h in hs]
    inter_num = [[] for _ in hs]
    for s, r in enumerate(chunk_rows):
        for h in hs:
            inter_num[h].append(_dot(qs[h][r], cxs[h]))
            m_new = m_news[h][s]
            w_state = jnp.exp(jnp.where(col_rows <= last, bcum[last_rows[s], h:h + 1] - bcum[r, h:h + 1]
                                        + ig[r, h:h + 1], -jnp.inf) - m_new)
            carry_decay = jnp.exp(inters[h][last_rows[s], :] - m_new)
            cxs[h] = carry_decay * cxs[h] + _dot_tn(ks[h][r] * w_state, vxs[h][r])
    m_next = m_row
    for h in hs:
        cx_s[h] = cxs[h]
        num = w_inters[h] * jnp.concatenate(inter_num[h], axis=0) + intra[h]
        den = num[:, ML_DV:ML_DV + 1]
        hh = num[:, :ML_DV] / jnp.maximum(jnp.abs(den), jnp.exp(-m_ts[h]))
        vcols = slice(h * ML_DV, (h + 1) * ML_DV)
        h_ref[:, vcols] = (hh * jax.nn.sigmoid(og_ref[:, vcols])).astype(h_ref.dtype)
        m_next = jnp.where(lane_row == h, m_in[h], m_next)
    m_s[0:1, :] = m_next

    @pl.when(t == pl.num_programs(1) - 1)
    def _():
        cout_ref[...] = cx_s[:, :, :ML_DV]
        nout_ref[...] = cx_s[:, :, ML_DV:]
        mout_ref[...] = m_s[...]


def _mlstm_seq(q, k, v, og, gates, c0, n0, m0, ibias, fbias, *, nseq, seqlen, tc, chunk, n_valid):
    nt = seqlen // tc
    row = lambda b, t: (b * nt + t, 0)
    seq3 = lambda b, t: (b, 0, 0)
    seq4 = lambda b, t: (b, 0, 0, 0)
    m = nseq * seqlen
    return pl.pallas_call(
        functools.partial(_mlstm_seq_kernel, tc=tc, chunk=chunk, n_valid=n_valid),
        grid=(nseq, nt),
        in_specs=[pl.BlockSpec((tc, ML_QK), row), pl.BlockSpec((tc, ML_QK), row),
                  pl.BlockSpec((tc, ML_V), row), pl.BlockSpec((tc, ML_V), row),
                  pl.BlockSpec((tc, 2 * LANES), row),
                  pl.BlockSpec((None, ML_HEADS, ML_DK, ML_DV), seq4),
                  pl.BlockSpec((None, ML_HEADS, ML_DK, LANES), seq4),
                  pl.BlockSpec((None, SUBLANES, LANES), seq3),
                  _const_spec((1, LANES)), _const_spec((1, LANES))],
        out_specs=[pl.BlockSpec((tc, ML_V), row),
                   pl.BlockSpec((None, ML_HEADS, ML_DK, ML_DV), seq4),
                   pl.BlockSpec((None, ML_HEADS, ML_DK, LANES), seq4),
                   pl.BlockSpec((None, SUBLANES, LANES), seq3)],
        out_shape=[jax.ShapeDtypeStruct((m, ML_V), BF16),
                   jax.ShapeDtypeStruct((nseq, ML_HEADS, ML_DK, ML_DV), F32),
                   jax.ShapeDtypeStruct((nseq, ML_HEADS, ML_DK, LANES), F32),
                   jax.ShapeDtypeStruct((nseq, SUBLANES, LANES), F32)],
        scratch_shapes=[pltpu.VMEM((ML_HEADS, ML_DK, ML_DV + LANES), F32),
                        pltpu.VMEM((SUBLANES, LANES), F32)],
        compiler_params=_compiler_params(2),
        name="mlstm_seq",
    )(q, k, v, og, gates, c0, n0, m0, ibias, fbias)


def _mla_proj_kernel(x_ref, g_ref, cos_ref, sin_ref, win_ref, qnorm_ref, kvnorm_ref, wuq_ref, wuqsw_ref, wkv_ref,
                     *o_refs, decode):
    xn = _rms(x_ref[...], g_ref[...]).astype(BF16)
    proj = jnp.dot(xn, win_ref[...], preferred_element_type=F32)
    cq = _rms(proj[:, :MLA_Q_RANK], qnorm_ref[...]).astype(BF16)
    ckv = _rms(proj[:, MLA_Q_RANK:MLA_Q_RANK + MLA_KV_RANK], kvnorm_ref[...])
    base = MLA_Q_RANK + MLA_KV_RANK
    cos = cos_ref[...]
    sin = sin_ref[...]
    kr = proj[:, base:base + LANES] * cos + proj[:, base + LANES:base + 2 * LANES] * sin
    q_main = jnp.dot(cq, wuq_ref[...], preferred_element_type=F32)
    q_swap = jnp.dot(cq, wuqsw_ref[...], preferred_element_type=F32)
    if decode:
        qlat_ref, qrope_ref, ckv_ref, kr_ref = o_refs
    else:
        qcat_ref, kcat_ref, vt_ref, ckv_ref, kr_ref = o_refs
        kv = jnp.dot(ckv.astype(BF16), wkv_ref[...], preferred_element_type=F32)
    ckv_ref[...] = ckv
    kr_ref[...] = kr[:, :MLA_ROPE]
    for h in range(MLA_HEADS):
        lo = h * 2 * LANES
        q_nope = q_main[:, lo:lo + LANES] * MLA_SCALE
        q_rope = (q_main[:, lo + LANES:lo + 2 * LANES] * cos + q_swap[:, h * LANES:(h + 1) * LANES] * sin) * MLA_SCALE
        if decode:
            qlat_ref[:, lo:lo + 2 * LANES] = _dot(q_nope, wkv_ref[h]).astype(BF16)
            qrope_ref[:, h * LANES:(h + 1) * LANES] = q_rope.astype(BF16)
        else:
            qcat_ref[:, lo:lo + LANES] = q_nope.astype(BF16)
            qcat_ref[:, lo + LANES:lo + 2 * LANES] = q_rope.astype(BF16)
            kcat_ref[:, lo:lo + LANES] = kv[:, lo:lo + LANES].astype(BF16)
            kcat_ref[:, lo + LANES:lo + 2 * LANES] = kr.astype(BF16)
            key_tile = vt_ref.shape[-1]
            for s in range(vt_ref.shape[0]):
                v_tile = kv[s * key_tile:(s + 1) * key_tile, lo + LANES:lo + 2 * LANES]
                vt_ref[s, h * MLA_V:(h + 1) * MLA_V, :] = v_tile.T.astype(BF16)


def _mla_proj(x, g, cos, sin, win, qnorm, kvnorm, wuq, wuqsw, wkv, *, tm, decode, key_tile=None):
    m = x.shape[0]
    n_pos_blocks = cos.shape[0] // tm
    row = lambda i: (i, 0)
    pos = lambda i: (i % n_pos_blocks, 0)
    wide = MLA_HEADS * 2 * LANES
    if decode:
        widths = (wide, MLA_HEADS * LANES, MLA_KV_RANK, MLA_ROPE)
        dtypes = (BF16, BF16, F32, F32)
    else:
        widths = (wide, wide, MLA_KV_RANK, MLA_ROPE)
        dtypes = (BF16, BF16, F32, F32)
    out_specs = [pl.BlockSpec((tm, n), row) for n in widths]
    out_shape = [jax.ShapeDtypeStruct((m, n), dt) for n, dt in zip(widths, dtypes)]
    if not decode:
        out_specs.insert(2, pl.BlockSpec((tm // key_tile, MLA_HEADS * MLA_V, key_tile), lambda i: (i, 0, 0)))
        out_shape.insert(2, jax.ShapeDtypeStruct((m // key_tile, MLA_HEADS * MLA_V, key_tile), BF16))
    return pl.pallas_call(
        functools.partial(_mla_proj_kernel, decode=decode),
        grid=(m // tm,),
        in_specs=[pl.BlockSpec((tm, D_MODEL), row), _const_spec((1, D_MODEL)),
                  pl.BlockSpec((tm, LANES), pos), pl.BlockSpec((tm, LANES), pos),
                  _const_spec(win.shape), _const_spec((1, MLA_Q_RANK)), _const_spec((1, MLA_KV_RANK)),
                  _const_spec(wuq.shape), _const_spec(wuqsw.shape), _const_spec(wkv.shape)],
        out_specs=out_specs,
        out_shape=out_shape,
        compiler_params=_compiler_params(1),
        name="mla_proj_decode" if decode else "mla_proj",
    )(x, g, cos, sin, win, qnorm, kvnorm, wuq, wuqsw, wkv)


def _mla_attn_kernel(q_ref, k_ref, vt_ref, o_ref, m_s, l_s, acc_s, *, tq):
    qi = pl.program_id(1)
    heads = MLA_HEADS
    wide = 2 * LANES
    m_s[...] = jnp.full(m_s.shape, -jnp.inf, F32)
    l_s[...] = jnp.zeros(l_s.shape, F32)
    acc_s[...] = jnp.zeros(acc_s.shape, F32)
    key_i = lax.broadcasted_iota(jnp.int32, (tq, tq), 0)
    query_i = lax.broadcasted_iota(jnp.int32, (tq, tq), 1)

    def key_tile(j, diagonal):
        rows = pl.ds(pl.multiple_of(j * tq, tq), tq)
        for g0 in range(0, heads, ATTN_HEAD_GROUP):
            hs = range(g0, g0 + ATTN_HEAD_GROUP)
            ss = [lax.dot_general(k_ref[rows, h * wide:(h + 1) * wide], q_ref[:, h * wide:(h + 1) * wide],
                                  (((1,), (1,)), ((), ())), preferred_element_type=F32) for h in hs]
            if diagonal:
                ss = [jnp.where(key_i <= query_i, s, -jnp.inf) for s in ss]
            m_old = [m_s[h] for h in hs]
            m_new = [jnp.maximum(m, jnp.max(s, axis=0, keepdims=True)) for m, s in zip(m_old, ss)]
            ps = [jnp.exp(s - m) for s, m in zip(ss, m_new)]
            alphas = [jnp.exp(mo - mn) for mo, mn in zip(m_old, m_new)]
            pvs = [jnp.dot(vt_ref[j, h * MLA_V:(h + 1) * MLA_V, :], p.astype(BF16), preferred_element_type=F32)
                   for p, h in zip(ps, hs)]
            for i, h in enumerate(hs):
                l_s[h] = alphas[i] * l_s[h] + jnp.sum(ps[i], axis=0, keepdims=True)
                acc_s[h] = alphas[i] * acc_s[h] + pvs[i]
                m_s[h] = m_new[i]

    def body(j, carry):
        key_tile(j, False)
        return carry

    lax.fori_loop(0, qi, body, 0)
    key_tile(qi, True)
    for h in range(heads):
        o_ref[:, h * MLA_V:(h + 1) * MLA_V] = (acc_s[h] / l_s[h]).T.astype(o_ref.dtype)


def _mla_attn(qcat, kcat, vt, *, nseq, seqlen, tq):
    nq = seqlen // tq
    m = nseq * seqlen
    return pl.pallas_call(
        functools.partial(_mla_attn_kernel, tq=tq),
        grid=(nseq, nq),
        in_specs=[pl.BlockSpec((tq, MLA_HEADS * 2 * LANES), lambda b, i: (b * nq + i, 0)),
                  pl.BlockSpec((seqlen, MLA_HEADS * 2 * LANES), lambda b, i: (b, 0)),
                  pl.BlockSpec((nq, MLA_HEADS * MLA_V, tq), lambda b, i: (b, 0, 0))],
        out_specs=pl.BlockSpec((tq, MLA_HEADS * MLA_V), lambda b, i: (b * nq + i, 0)),
        out_shape=jax.ShapeDtypeStruct((m, MLA_HEADS * MLA_V), BF16),
        scratch_shapes=[pltpu.VMEM((MLA_HEADS, 1, tq), F32), pltpu.VMEM((MLA_HEADS, 1, tq), F32),
                        pltpu.VMEM((MLA_HEADS, MLA_V, tq), F32)],
        compiler_params=_compiler_params(2),
        name="mla_attn",
    )(qcat, kcat, vt)


def _mla_decode_kernel(pt_ref, qlat_ref, qrope_ref, ckv_ref, kr_ref, wuv_ref, *rest, n_pages):
    lat_refs = rest[:n_pages]
    rope_refs = rest[n_pages:2 * n_pages]
    o_ref = rest[2 * n_pages]
    ql_s, qr_s, m_s, l_s, acc_s = rest[2 * n_pages + 1:]
    g = pl.program_id(1)
    heads = MLA_HEADS
    tok = DEC_PAD

    @pl.when(g == 0)
    def _():
        for h in range(heads):
            ql_s[h * tok:(h + 1) * tok, :] = qlat_ref[:, h * 2 * LANES:(h + 1) * 2 * LANES]
            qr_s[h * tok:(h + 1) * tok, :] = qrope_ref[:, h * LANES:(h + 1) * LANES]
        m_s[...] = jnp.full(m_s.shape, -jnp.inf, F32)
        l_s[...] = jnp.zeros(l_s.shape, F32)
        acc_s[...] = jnp.zeros(acc_s.shape, F32)

    ql = ql_s[...]
    qr = qr_s[:, :MLA_ROPE]

    def update(chains, scores, values):
        m_old = [m_s[c] for c in chains]
        m_new = [jnp.maximum(m, jnp.max(s, axis=-1, keepdims=True)) for m, s in zip(m_old, scores)]
        ps = [jnp.exp(s - m) for s, m in zip(scores, m_new)]
        alphas = [jnp.exp(mo - mn) for mo, mn in zip(m_old, m_new)]
        for i, c in enumerate(chains):
            l_s[c] = alphas[i] * l_s[c] + jnp.sum(ps[i], axis=-1, keepdims=True)
            pb = ps[i].astype(BF16)
            acc = alphas[i] * acc_s[c]
            for cols, val in values[i]:
                acc = acc + jnp.dot(pb[:, cols], val, preferred_element_type=F32)
            acc_s[c] = acc
            m_s[c] = m_new[i]

    n_chains = m_s.shape[0]
    per = n_pages // n_chains
    lats = [r[...].astype(BF16) for r in lat_refs]
    scores = [_dot_nt(ql, lat) + _dot(qr, r[...]) for lat, r in zip(lats, rope_refs)]
    update(range(n_chains),
           [jnp.concatenate(scores[c * per:(c + 1) * per], axis=1) for c in range(n_chains)],
           [[(slice(i * PAGE_SIZE, (i + 1) * PAGE_SIZE), lats[c * per + i]) for i in range(per)]
            for c in range(n_chains)])

    @pl.when(g == pl.num_programs(1) - 1)
    def _():
        ckv = ckv_ref[...].astype(BF16)
        s = _dot_nt(ql, ckv) + _dot_nt(qr, kr_ref[...])
        qt = lax.broadcasted_iota(jnp.int32, s.shape, 0) % tok
        kt = lax.broadcasted_iota(jnp.int32, s.shape, 1)
        update([0], [jnp.where(kt <= qt, s, -jnp.inf)], [[(slice(0, tok), ckv)]])
        m_all = m_s[0]
        for c in range(1, n_chains):
            m_all = jnp.maximum(m_all, m_s[c])
        weights = [jnp.exp(m_s[c] - m_all) for c in range(n_chains)]
        l_all = sum(w * l_s[c] for c, w in enumerate(weights))
        acc_all = sum(w * acc_s[c] for c, w in enumerate(weights))
        o_lat = acc_all / l_all
        for h in range(heads):
            o_ref[:, h * MLA_V:(h + 1) * MLA_V] = _dot(o_lat[h * tok:(h + 1) * tok, :], wuv_ref[h]).astype(o_ref.dtype)


def _mla_decode(page_table, qlat, qrope, ckv, kr, wuv, lat_pages, rope_pages, *, nseq):
    n_pages_seq = page_table.shape[1]
    n = PAGES_PER_STEP
    groups = n_pages_seq // n
    seq = lambda b, g, pt: (b, 0)

    def page_map(i):
        return lambda b, g, pt: (pt[b * n_pages_seq + g * n + i], 0, 0)

    rows = MLA_HEADS * DEC_PAD
    grid_spec = pltpu.PrefetchScalarGridSpec(
        num_scalar_prefetch=1,
        grid=(nseq, groups),
        in_specs=[pl.BlockSpec((DEC_PAD, MLA_HEADS * 2 * LANES), seq),
                  pl.BlockSpec((DEC_PAD, MLA_HEADS * LANES), seq),
                  pl.BlockSpec((DEC_PAD, MLA_KV_RANK), seq),
                  pl.BlockSpec((DEC_PAD, MLA_ROPE), seq),
                  pl.BlockSpec(wuv.shape, lambda b, g, pt: (0, 0, 0))]
                 + [pl.BlockSpec((None, PAGE_SIZE, MLA_KV_RANK), page_map(i)) for i in range(n)]
                 + [pl.BlockSpec((None, MLA_ROPE, PAGE_SIZE), page_map(i)) for i in range(n)],
        out_specs=pl.BlockSpec((DEC_PAD, MLA_HEADS * MLA_V), seq),
        scratch_shapes=[pltpu.VMEM((rows, MLA_KV_RANK), BF16), pltpu.VMEM((rows, LANES), BF16),
                        pltpu.VMEM((DECODE_CHAINS, rows, 1), F32), pltpu.VMEM((DECODE_CHAINS, rows, 1), F32),
                        pltpu.VMEM((DECODE_CHAINS, rows, MLA_KV_RANK), F32)],
    )
    return pl.pallas_call(
        functools.partial(_mla_decode_kernel, n_pages=n),
        grid_spec=grid_spec,
        out_shape=jax.ShapeDtypeStruct((nseq * DEC_PAD, MLA_HEADS * MLA_V), BF16),
        compiler_params=_compiler_params(2),
        name="mla_decode",
    )(page_table.reshape(-1), qlat, qrope, ckv, kr, wuv, *([lat_pages] * n), *([rope_pages] * n))


def _row(v, width=None):
    v = v.astype(F32).reshape(1, -1)
    if width is not None and v.shape[1] < width:
        v = jnp.pad(v, ((0, 0), (0, width - v.shape[1])))
    return v


def _pad_cols(w, width):
    return jnp.pad(w, ((0, 0), (0, width - w.shape[1])))


def _prep_ffn(w_up, w_down):
    return w_up.astype(BF16), w_down.reshape(w_down.shape[:2] + (N_FF_CHUNKS, FF_CHUNK, D_MODEL)).astype(BF16)


def _prep_gdn(w_in, conv_w, a_log, dt_bias, norm_w, w_out):
    main = GDN_QKV + GDN_Z
    w = jnp.concatenate([w_in[:, :main], _pad_cols(w_in[:, main:main + GDN_HEADS], LANES),
                         _pad_cols(w_in[:, main + GDN_HEADS:], LANES)], axis=1).astype(BF16)
    convw = jnp.pad(conv_w.astype(F32), ((0, SUBLANES - GDN_CONV), (0, 0)))
    return w, convw, _row(a_log, LANES), _row(dt_bias, LANES), _row(norm_w), w_out.astype(BF16)


def _prep_mlstm(w_in, i_bias, f_bias, w_out):
    main = 2 * ML_QK + 2 * ML_V
    w = jnp.concatenate([w_in[:, :main], _pad_cols(w_in[:, main:main + ML_HEADS], LANES),
                         _pad_cols(w_in[:, main + ML_HEADS:], LANES)], axis=1).astype(BF16)
    return w, _row(i_bias, LANES), _row(f_bias, LANES), w_out.astype(BF16)


def _swap_halves(w):
    half = w.shape[-1] // 2
    return jnp.concatenate([w[..., half:], w[..., :half]], axis=-1)


def _prep_mla(w_in, q_norm, kv_norm, w_uq, w_ukv, w_out):
    base = MLA_Q_RANK + MLA_KV_RANK
    kr = w_in[:, base:]
    win = jnp.concatenate([w_in[:, :base], _pad_cols(kr, LANES), _pad_cols(_swap_halves(kr), LANES)],
                          axis=1).astype(BF16)
    wq = w_uq.reshape(MLA_Q_RANK, MLA_HEADS, MLA_NOPE + MLA_ROPE)
    nope, rope = wq[..., :MLA_NOPE], wq[..., MLA_NOPE:]
    zeros = jnp.zeros((MLA_Q_RANK, MLA_HEADS, LANES - MLA_ROPE), w_uq.dtype)
    wuq = jnp.concatenate([nope, rope, zeros], axis=-1).reshape(MLA_Q_RANK, -1).astype(BF16)
    wuqsw = jnp.concatenate([_swap_halves(rope), zeros], axis=-1).reshape(MLA_Q_RANK, -1).astype(BF16)
    wkv3 = w_ukv.reshape(MLA_KV_RANK, MLA_HEADS, MLA_NOPE + MLA_V)
    wuk_t = wkv3[..., :MLA_NOPE].transpose(1, 2, 0).astype(BF16)
    wuv = wkv3[..., MLA_NOPE:].transpose(1, 0, 2).astype(BF16)
    return dict(win=win, qnorm=_row(q_norm), kvnorm=_row(kv_norm), wuq=wuq, wuqsw=wuqsw,
                wkv=w_ukv.astype(BF16), wuk_t=wuk_t, wuv=wuv, wout=w_out.astype(BF16))


def _rope_tables(pos):
    half = MLA_ROPE // 2
    inv = ROPE_THETA ** (-jnp.arange(half, dtype=F32) / half)
    ang = pos.astype(F32)[:, None] * inv
    cos, sin = jnp.cos(ang), jnp.sin(ang)
    pad = jnp.zeros((pos.shape[0], LANES - MLA_ROPE), F32)
    return jnp.concatenate([cos, cos, pad], axis=1), jnp.concatenate([-sin, sin, pad], axis=1)


def _tiles(nseq, seqlen):
    m = nseq * seqlen
    return dict(ffn=min(512, m), proj=min(512, m), mla_proj=min(512, seqlen), attn_q=min(256, seqlen),
                gdn=min(2 * GDN_CHUNK, seqlen), gdn_group=math.gcd(GDN_SEQ_GROUP, nseq),
                mlstm=min(2 * ML_CHUNK, seqlen))


def _trunk(x, *, nseq, seqlen, n_tokens, decode, states, page_table, weights):
    gdn_S, gdn_conv, mla_lat, mla_rope, ml_C, ml_n, ml_m = states
    chunk_g = min(GDN_CHUNK, seqlen)
    chunk_m = min(ML_CHUNK, seqlen)
    tiles = _tiles(nseq, seqlen)
    tm = tiles["ffn"]
    new = ([], [], [], [], [], [], [])
    counts = [0, 0, 0]
    for layer in range(4):
        kind = layer % 3
        j = counts[kind]
        counts[kind] += 1
        ng = weights["gains"][layer]
        wup, wdn = weights["ffn"]
        x = _ffn_half(x, ng[0], ng[1], wup, wdn, layer, 0, tm)
        if kind == 0:
            w, convw, alog, dtb, normw, wout = weights["gdn"][j]
            q, k, v, z, ba, conv_new = _gdn_proj(x, ng[2], w, convw, gdn_conv, layer=j, nseq=nseq, seqlen=seqlen,
                                                 n_tokens=n_tokens, tm=tiles["proj"])
            o, s_new = _gdn_seq(q, k, v, z, ba, gdn_S, alog, dtb, normw, layer=j, nseq=nseq, seqlen=seqlen,
                                tc=tiles["gdn"], chunk=chunk_g, n_valid=min(tiles["gdn"], n_tokens),
                                group=tiles["gdn_group"])
            new[0].append(s_new)
            new[1].append(conv_new[:, SUBLANES - (GDN_CONV - 1):, :])
        elif kind == 1:
            p = weights["mla"][j]
            cos, sin = weights["rope"]
            if decode:
                qlat, qrope, ckv, kr = _mla_proj(x, ng[2], cos, sin, p["win"], p["qnorm"], p["kvnorm"], p["wuq"],
                                                 p["wuqsw"], p["wuk_t"], tm=tiles["mla_proj"], decode=True)
                o = _mla_decode(page_table, qlat, qrope, ckv, kr, p["wuv"], mla_lat[j], mla_rope[j], nseq=nseq)
            else:
                qcat, kcat, vt, ckv, kr = _mla_proj(x, ng[2], cos, sin, p["win"], p["qnorm"], p["kvnorm"], p["wuq"],
                                                    p["wuqsw"], p["wkv"], tm=tiles["mla_proj"], decode=False,
                                                    key_tile=tiles["attn_q"])
                o = _mla_attn(qcat, kcat, vt, nseq=nseq, seqlen=seqlen, tq=tiles["attn_q"])
            wout = p["wout"]
            new[2].append(ckv)
            new[3].append(kr)
        else:
            w, ibias, fbias, wout = weights["mlstm"][j]
            q, k, v, og, gates = _norm_matmul(x, ng[2], w, (ML_QK, ML_QK, ML_V, ML_V, 2 * LANES), tiles["proj"])
            o, c_new, n_new, m_new = _mlstm_seq(q, k, v, og, gates, ml_C[j], ml_n[j], ml_m[j], ibias, fbias,
                                                nseq=nseq, seqlen=seqlen, tc=tiles["mlstm"], chunk=chunk_m,
                                                n_valid=min(tiles["mlstm"], n_tokens))
            new[4].append(c_new)
            new[5].append(n_new[..., 0])
            new[6].append(m_new[:, 0, :ML_HEADS])
        x = _ffn_half(x, ng[4], ng[5], wup, wdn, layer, 1, tm, mixer_out=(o, ng[3], wout))
    return x, [jnp.stack(s) for s in new]


def kernel(x_prompt, x_sample, state_gdn_S, state_gdn_conv, cache_mla_latent, cache_mla_rope, state_mlstm_C, state_mlstm_n, state_mlstm_m, page_table, norm_gains, w_ffn_up, w_ffn_down, gdn_w_in, gdn_conv_w, gdn_a_log, gdn_dt_bias, gdn_norm_w, gdn_w_out, mla_w_in, mla_q_norm, mla_kv_norm, mla_w_uq, mla_w_ukv, mla_w_out, mlstm_w_in, mlstm_i_bias, mlstm_f_bias, mlstm_w_out):
    nb, seq, _ = x_prompt.shape
    db, dseq, _ = x_sample.shape
    n_gdn, n_mla, n_ml = gdn_w_in.shape[0], mla_w_in.shape[0], mlstm_w_in.shape[0]
    past = page_table.shape[1] * PAGE_SIZE

    weights = dict(
        gains=[[_row(norm_gains[l, i]) for i in range(6)] for l in range(4)],
        ffn=_prep_ffn(w_ffn_up, w_ffn_down),
        gdn=[_prep_gdn(gdn_w_in[j], gdn_conv_w[j], gdn_a_log[j], gdn_dt_bias[j], gdn_norm_w[j], gdn_w_out[j])
             for j in range(n_gdn)],
        mla=[_prep_mla(mla_w_in[j], mla_q_norm[j], mla_kv_norm[j], mla_w_uq[j], mla_w_ukv[j], mla_w_out[j])
             for j in range(n_mla)],
        mlstm=[_prep_mlstm(mlstm_w_in[j], mlstm_i_bias[j], mlstm_f_bias[j], mlstm_w_out[j]) for j in range(n_ml)],
    )

    zeros = lambda *s: jnp.zeros(s, F32)
    states_p = (zeros(n_gdn, nb, GDN_HEADS, GDN_DK, GDN_DV), zeros(n_gdn, nb, SUBLANES, GDN_QKV), None, None,
                zeros(n_ml, nb, ML_HEADS, ML_DK, ML_DV), zeros(n_ml, nb, ML_HEADS, ML_DK, LANES),
                zeros(n_ml, nb, SUBLANES, LANES))
    y_p, st_p = _trunk(x_prompt.reshape(nb * seq, D_MODEL), nseq=nb, seqlen=seq, n_tokens=seq, decode=False,
                       states=states_p, page_table=None,
                       weights=dict(weights, rope=_rope_tables(jnp.arange(seq))))

    pad_t = DEC_PAD - dseq
    x_s = jnp.pad(x_sample, ((0, 0), (0, pad_t), (0, 0))).reshape(db * DEC_PAD, D_MODEL)
    conv0 = jnp.pad(state_gdn_conv, ((0, 0), (0, 0), (SUBLANES - (GDN_CONV - 1), 0), (0, 0)))
    n0 = jnp.pad(state_mlstm_n[..., None], ((0, 0),) * 4 + ((0, LANES - 1),))
    m0 = jnp.pad(state_mlstm_m[:, :, None, :], ((0, 0), (0, 0), (0, SUBLANES - 1), (0, LANES - ML_HEADS)))
    states_s = (state_gdn_S, conv0,
                cache_mla_latent.reshape((n_mla, -1) + cache_mla_latent.shape[2:]),
                jnp.swapaxes(cache_mla_rope.reshape((n_mla, -1) + cache_mla_rope.shape[2:]), -1, -2),
                state_mlstm_C, n0, m0)
    pos_s = jnp.tile(past + jnp.arange(DEC_PAD), db)
    y_s, st_s = _trunk(x_s, nseq=db, seqlen=DEC_PAD, n_tokens=dseq, decode=True, states=states_s,
                       page_table=page_table, weights=dict(weights, rope=_rope_tables(pos_s)))

    gdn_S_p, gdn_conv_p, lat_p, rope_p, ml_C_p, ml_n_p, ml_m_p = st_p
    gdn_S_s, gdn_conv_s, lat_s, rope_s, ml_C_s, ml_n_s, ml_m_s = st_s
    unpad = lambda a: a.reshape(a.shape[0], db, DEC_PAD, a.shape[-1])[:, :, :dseq]
    return (y_p.reshape(nb, seq, D_MODEL), unpad(y_s[None])[0],
            gdn_S_p, gdn_S_s, gdn_conv_p, gdn_conv_s,
            lat_p.reshape(n_mla, -1, PAGE_SIZE, MLA_KV_RANK), unpad(lat_s),
            rope_p.reshape(n_mla, -1, PAGE_SIZE, MLA_ROPE), unpad(rope_s),
            ml_C_p, ml_C_s, ml_n_p, ml_n_s, ml_m_p, ml_m_s)
```

```python
import functools
import math

import jax
import jax.numpy as jnp
from jax import lax
from jax.experimental import pallas as pl
from jax.experimental.pallas import tpu as pltpu

F32 = jnp.float32
BF16 = jnp.bfloat16

D_MODEL = 1024
PAGE_SIZE = 128
EPS = 1e-6

GDN_HEADS = 8
GDN_DK = 128
GDN_DV = 128
GDN_CONV = 4
GDN_CHUNK = 64
GDN_QKV = GDN_HEADS * (2 * GDN_DK + GDN_DV)
GDN_Z = GDN_HEADS * GDN_DV

MLA_HEADS = 8
MLA_NOPE = 128
MLA_ROPE = 64
MLA_V = 128
MLA_Q_RANK = 384
MLA_KV_RANK = 256
MLA_SCALE = (MLA_NOPE + MLA_ROPE) ** -0.5
ROPE_THETA = 10000.0

ML_HEADS = 4
ML_DK = 128
ML_DV = 256
ML_CHUNK = 64
ML_QK = ML_HEADS * ML_DK
ML_V = ML_HEADS * ML_DV

D_FF = 2816
FF_CHUNK = 256
N_FF_CHUNKS = D_FF // FF_CHUNK

LANES = 128
SUBLANES = 8
DEC_PAD = SUBLANES
VMEM_LIMIT = 56 * 1024 * 1024
PAGES_PER_STEP = 16
DECODE_CHAINS = 4
ATTN_HEAD_GROUP = 8
GDN_SEQ_GROUP = 2
GDN_SEQ_GROUP_SHORT = 4
SOLVE_REFINEMENTS = 0


def _rms(x, g):
    return x * lax.rsqrt(jnp.mean(x * x, axis=-1, keepdims=True) + EPS) * g


def _silu(x):
    return x * jax.nn.sigmoid(x)


def _softplus(x):
    return jnp.maximum(x, 0.0) + jnp.log1p(jnp.exp(-jnp.abs(x)))


def _dot(a, b):
    return jnp.dot(a.astype(BF16), b.astype(BF16), preferred_element_type=F32)


def _dot_nt(a, b):
    return lax.dot_general(a.astype(BF16), b.astype(BF16), (((1,), (1,)), ((), ())),
                           preferred_element_type=F32)


def _dot_tn(a, b):
    return lax.dot_general(a.astype(BF16), b.astype(BF16), (((0,), (0,)), ((), ())),
                           preferred_element_type=F32)


def _mask_dot_f32(mask, x):
    m = mask.astype(BF16)
    hi = x.astype(BF16)
    rest = x - hi.astype(F32)
    mid = rest.astype(BF16)
    lo = (rest - mid.astype(F32)).astype(BF16)
    if m.shape[1] % LANES == 0:
        return jnp.dot(jnp.concatenate([m, m, m], axis=1), jnp.concatenate([hi, mid, lo], axis=0),
                       preferred_element_type=F32)
    return sum(jnp.dot(m, t, preferred_element_type=F32) for t in (hi, mid, lo))


def _split(x):
    hi = x.astype(BF16)
    return hi, (x - hi.astype(F32)).astype(BF16)


def _dot3(a, b):
    (a_hi, a_lo), (b_hi, b_lo) = a, b
    if a_hi.shape[1] % LANES == 0:
        return jnp.dot(jnp.concatenate([a_hi, a_lo, a_hi], axis=1), jnp.concatenate([b_hi, b_hi, b_lo], axis=0),
                       preferred_element_type=F32)
    m = a_hi.shape[0]
    r = jnp.dot(jnp.concatenate([a_hi, a_lo], axis=0), b_hi, preferred_element_type=F32)
    return r[:m] + r[m:] + jnp.dot(a_hi, b_lo, preferred_element_type=F32)


def _unit_lower_inverse_minus_eye(a_mats, ri, ci, chunk):
    rs = None
    s = 1
    while s < chunk:
        shift = s.bit_length() - 1
        off = ((ri >> (shift + 1)) == (ci >> (shift + 1))) & ((ri >> shift) != (ci >> shift))
        a_offs = [jnp.where(off, a, 0.0) for a in a_mats]
        if rs is None:
            rs = [-a for a in a_offs]
        else:
            bs = [a + _dot(r, a) for a, r in zip(a_offs, rs)]
            rs = [r - b - _dot(b, r) for r, b in zip(rs, bs)]
        s *= 2
    return rs


def _unit_lower_solve(a_mats, rhss, ri, ci, chunk):
    rs = _unit_lower_inverse_minus_eye(a_mats, ri, ci, chunk)
    a_sp = [_split(a) for a in a_mats]
    xs = [rhs + _dot(r, rhs) for r, rhs in zip(rs, rhss)]
    for _ in range(SOLVE_REFINEMENTS):
        resid = [rhs - x - _dot3(a, _split(x)) for rhs, x, a in zip(rhss, xs, a_sp)]
        xs = [x + e + _dot(r, e) for x, e, r in zip(xs, resid, rs)]
    return xs


def _compiler_params(n_axes):
    return pltpu.CompilerParams(dimension_semantics=("arbitrary",) * n_axes,
                                vmem_limit_bytes=VMEM_LIMIT)


SINGLE_BUFFER = pl.Buffered(1)


def _const_spec(shape):
    nd = len(shape)
    return pl.BlockSpec(shape, lambda *_: (0,) * nd, pipeline_mode=SINGLE_BUFFER)


def _ffn_kernel(*refs, mixer_out):
    if mixer_out:
        a_ref, gmix_ref, wout_ref, x_ref, gpre_ref, gpost_ref, wup_ref, wdn_ref, o_ref = refs
        x = x_ref[...] + _rms(jnp.dot(a_ref[...], wout_ref[...], preferred_element_type=F32), gmix_ref[...])
    else:
        x_ref, gpre_ref, gpost_ref, wup_ref, wdn_ref, o_ref = refs
        x = x_ref[...]
    xn = _rms(x, gpre_ref[...]).astype(BF16)
    acc = jnp.zeros(x.shape, F32)
    for c in range(N_FF_CHUNKS):
        lo = c * FF_CHUNK
        gate = jnp.dot(xn, wup_ref[:, lo:lo + FF_CHUNK], preferred_element_type=F32)
        up = jnp.dot(xn, wup_ref[:, D_FF + lo:D_FF + lo + FF_CHUNK], preferred_element_type=F32)
        act = (_silu(gate) * up).astype(BF16)
        acc = acc + jnp.dot(act, wdn_ref[c], preferred_element_type=F32)
    o_ref[...] = x + 0.5 * _rms(acc, gpost_ref[...])


def _ffn_half(x, gpre, gpost, wup, wdn, layer, half, tm, mixer_out=None):
    m = x.shape[0]
    row = pl.BlockSpec((tm, D_MODEL), lambda i: (i, 0))
    gain = _const_spec((1, D_MODEL))
    in_specs = [row, gain, gain,
                pl.BlockSpec((None, None) + wup.shape[2:], lambda i: (layer, half, 0, 0),
                             pipeline_mode=SINGLE_BUFFER),
                pl.BlockSpec((None, None) + wdn.shape[2:], lambda i: (layer, half, 0, 0, 0),
                             pipeline_mode=SINGLE_BUFFER)]
    args = (x, gpre, gpost, wup, wdn)
    if mixer_out is not None:
        a, gmix, wout = mixer_out
        in_specs = [row, gain, _const_spec(wout.shape)] + in_specs
        args = (a, gmix, wout) + args
    return pl.pallas_call(
        functools.partial(_ffn_kernel, mixer_out=mixer_out is not None),
        grid=(m // tm,),
        in_specs=in_specs,
        out_specs=row,
        out_shape=jax.ShapeDtypeStruct((m, D_MODEL), F32),
        compiler_params=_compiler_params(1),
        name="ffn_half",
    )(*args)


def _norm_matmul_kernel(x_ref, g_ref, w_ref, *o_refs, splits):
    xn = _rms(x_ref[...], g_ref[...]).astype(BF16)
    off = 0
    for o_ref, n in zip(o_refs, splits):
        o_ref[...] = jnp.dot(xn, w_ref[:, off:off + n], preferred_element_type=F32)
        off += n


def _norm_matmul(x, g, w, splits, tm):
    m = x.shape[0]
    return pl.pallas_call(
        functools.partial(_norm_matmul_kernel, splits=splits),
        grid=(m // tm,),
        in_specs=[pl.BlockSpec((tm, D_MODEL), lambda i: (i, 0)),
                  _const_spec((1, D_MODEL)), _const_spec(w.shape)],
        out_specs=[pl.BlockSpec((tm, n), lambda i: (i, 0)) for n in splits],
        out_shape=[jax.ShapeDtypeStruct((m, n), F32) for n in splits],
        compiler_params=_compiler_params(1),
        name="norm_matmul",
    )(x, g, w)


def _gdn_proj_kernel(x_ref, g_ref, w_ref, convw_ref, conv0_ref, q_ref, k_ref, v_ref, z_ref, ba_ref, convout_ref,
                     xext, *, seg, n_valid, tiles_per_seq):
    tm = x_ref.shape[0]
    nsub = tm // seg
    prev = SUBLANES - (GDN_CONV - 1)
    xn = _rms(x_ref[...], g_ref[...]).astype(BF16)
    if tiles_per_seq == 1:
        xext[:, 0:SUBLANES, :] = conv0_ref[...]
    else:
        @pl.when(pl.program_id(0) % tiles_per_seq == 0)
        def _():
            xext[:, 0:SUBLANES, :] = conv0_ref[...]
    for part, dest in enumerate((q_ref, k_ref, v_ref)):
        for h0 in range(0, GDN_HEADS, 2):
            base = part * GDN_HEADS * GDN_DK + h0 * GDN_DK
            pre2 = jnp.dot(xn, w_ref[:, base:base + 2 * GDN_DK], preferred_element_type=F32)
            for h in (h0, h0 + 1):
                c0 = part * GDN_HEADS * GDN_DK + h * GDN_DK
                cols = slice(c0, c0 + GDN_DK)
                pre = pre2[:, c0 - base:c0 - base + GDN_DK].reshape(nsub, seg, GDN_DK)
                xext[:, SUBLANES:SUBLANES + seg, cols] = pre
                y = pre * convw_ref[GDN_CONV - 1:GDN_CONV, cols]
                for j in range(GDN_CONV - 1):
                    y = y + xext[:, prev + j:prev + j + seg, cols] * convw_ref[j:j + 1, cols]
                y = _silu(y)
                if part < 2:
                    y = y * lax.rsqrt(jnp.sum(y * y, axis=-1, keepdims=True) + EPS)
                if part == 0:
                    y = y * GDN_DK ** -0.5
                dest[:, h * GDN_DK:(h + 1) * GDN_DK] = y.reshape(tm, GDN_DK)
    z_ref[...] = jnp.dot(xn, w_ref[:, GDN_QKV:GDN_QKV + GDN_Z], preferred_element_type=F32)
    ba_ref[...] = jnp.dot(xn, w_ref[:, GDN_QKV + GDN_Z:], preferred_element_type=F32)
    convout_ref[...] = xext[:, n_valid:n_valid + SUBLANES, :]
    if tiles_per_seq > 1:
        xext[:, 0:SUBLANES, :] = xext[:, seg:seg + SUBLANES, :]


def _gdn_proj(x, g, w, convw, conv0, *, layer, nseq, seqlen, n_tokens, tm):
    m = x.shape[0]
    seg = min(tm, seqlen)
    nsub = tm // seg
    tiles_per_seq = seqlen // seg
    row = lambda n: pl.BlockSpec((tm, n), lambda i: (i, 0))
    per_seq = lambda i: (i // tiles_per_seq, 0, 0)
    return pl.pallas_call(
        functools.partial(_gdn_proj_kernel, seg=seg, n_valid=min(seg, n_tokens), tiles_per_seq=tiles_per_seq),
        grid=(m // tm,),
        in_specs=[row(D_MODEL), _const_spec((1, D_MODEL)), _const_spec(w.shape), _const_spec((SUBLANES, GDN_QKV)),
                  pl.BlockSpec((None, nsub, SUBLANES, GDN_QKV), lambda i: (layer,) + per_seq(i))],
        out_specs=[row(GDN_Z), row(GDN_Z), row(GDN_Z), row(GDN_Z), row(2 * LANES),
                   pl.BlockSpec((nsub, SUBLANES, GDN_QKV), per_seq)],
        out_shape=[jax.ShapeDtypeStruct((m, GDN_Z), F32)] * 4 + [jax.ShapeDtypeStruct((m, 2 * LANES), F32),
                                                                jax.ShapeDtypeStruct((nseq, SUBLANES, GDN_QKV), F32)],
        scratch_shapes=[pltpu.VMEM((nsub, seg + SUBLANES, GDN_QKV), F32)],
        compiler_params=_compiler_params(1),
        name="gdn_proj",
    )(x, g, w, convw, conv0)


def _gdn_seq_kernel(q_ref, k_ref, v_ref, z_ref, ba_ref, s0_ref, alog_ref, dtb_ref, normw_ref,
                    o_ref, sout_ref, s_s, *, tc, chunk, n_valid):
    t = pl.program_id(1)
    heads = GDN_HEADS
    group = q_ref.shape[0]
    head_cols = lambda h: slice(h * GDN_DK, (h + 1) * GDN_DK)
    sub = tc // chunk
    last = min(chunk, n_valid) - 1

    @pl.when(t == 0)
    def _():
        s_s[...] = s0_ref[...]

    ri = lax.broadcasted_iota(jnp.int32, (tc, tc), 0)
    ci = lax.broadcasted_iota(jnp.int32, (tc, tc), 1)
    incl = ci <= ri
    if sub > 1:
        incl = incl & ((ri // chunk) == (ci // chunk))
    strict = incl & (ci < ri)
    chunk_rows = [slice(s * chunk, (s + 1) * chunk) for s in range(sub)]
    beta, gc, gc_t, exp_gc, exp_last, exp_rest = [], [], [], [], [], []
    for p in range(group):
        ba = ba_ref[p]
        b = jax.nn.sigmoid(ba[:, :LANES])
        if n_valid < tc:
            b = jnp.where(lax.broadcasted_iota(jnp.int32, b.shape, 0) < n_valid, b, 0.0)
        g = -jnp.exp(alog_ref[...]) * _softplus(ba[:, LANES:] + dtb_ref[...])
        c = _mask_dot_f32(incl, g)
        c_last = [c[s * chunk + last:s * chunk + last + 1, :] for s in range(sub)]
        beta.append(b)
        gc.append(c)
        gc_t.append(c.T)
        exp_gc.append(jnp.exp(c))
        exp_last.append([jnp.exp(v) for v in c_last])
        exp_rest.append(jnp.exp(jnp.concatenate([jnp.broadcast_to(v, (chunk, LANES)) for v in c_last], axis=0) - c))
    normw = normw_ref[...]

    units = [(p, h) for p in range(group) for h in range(heads)]
    col = lambda x, h: x[:, h:h + 1]
    ks = {(p, h): k_ref[p, :, head_cols(h)] for p, h in units}
    bs = {(p, h): col(beta[p], h) for p, h in units}
    kbs = {u: ks[u] * bs[u] for u in units}
    decays = {(p, h): jnp.exp(jnp.where(incl, col(gc[p], h) - gc_t[p][h:h + 1, :], -jnp.inf)) for p, h in units}
    a_mats = [_dot_nt(kbs[u], ks[u]) * jnp.where(strict, decays[u], 0.0) for u in units]
    rhss = [jnp.concatenate([v_ref[p, :, head_cols(h)] * bs[p, h], kbs[p, h] * col(exp_gc[p], h)], axis=1) for p, h in units]
    sols = dict(zip(units, _unit_lower_solve(a_mats, rhss, ri, ci, chunk)))
    qs = {(p, h): q_ref[p, :, head_cols(h)] for p, h in units}
    attns = {u: _dot_nt(qs[u], ks[u]) * decays[u] for u in units}
    qgs = {(p, h): qs[p, h] * col(exp_gc[p], h) for p, h in units}
    kdecs = {(p, h): ks[p, h] * col(exp_rest[p], h) for p, h in units}
    states = {u: s_s[u] for u in units}
    v_new = {u: [] for u in units}
    o_inter = {u: [] for u in units}
    for s, r in enumerate(chunk_rows):
        for p, h in units:
            u = (p, h)
            vn = sols[u][r, :GDN_DV] - _dot(sols[u][r, GDN_DV:], states[u])
            o_inter[u].append(_dot(qgs[u][r], states[u]))
            states[u] = states[u] * col(exp_last[p][s], h) + _dot_tn(kdecs[u][r], vn)
            v_new[u].append(vn)
    for p, h in units:
        u = (p, h)
        s_s[u] = states[u]
        o = jnp.concatenate(o_inter[u], axis=0) + _dot(attns[u], jnp.concatenate(v_new[u], axis=0))
        cols = slice(h * GDN_DV, (h + 1) * GDN_DV)
        o_ref[p, :, cols] = (_rms(o, normw) * _silu(z_ref[p, :, cols])).astype(o_ref.dtype)

    @pl.when(t == pl.num_programs(1) - 1)
    def _():
        sout_ref[...] = s_s[...]


def _gdn_seq(q, k, v, z, ba, s0, alog, dtb, normw, *, layer, nseq, seqlen, tc, chunk, n_valid, group):
    nt = seqlen // tc
    tok = lambda b, t: (b, t, 0)
    view = lambda a: a.reshape(nseq, seqlen, a.shape[-1])
    wide = pl.BlockSpec((group, tc, GDN_Z), tok)
    o, s_new = pl.pallas_call(
        functools.partial(_gdn_seq_kernel, tc=tc, chunk=chunk, n_valid=n_valid),
        grid=(nseq // group, nt),
        in_specs=[wide, wide, wide, wide, pl.BlockSpec((group, tc, 2 * LANES), tok),
                  pl.BlockSpec((None, group, GDN_HEADS, GDN_DK, GDN_DV), lambda b, t: (layer, b, 0, 0, 0)),
                  _const_spec((1, LANES)), _const_spec((1, LANES)), _const_spec((1, GDN_DV))],
        out_specs=[wide, pl.BlockSpec((group, GDN_HEADS, GDN_DK, GDN_DV), lambda b, t: (b, 0, 0, 0))],
        out_shape=[jax.ShapeDtypeStruct((nseq, seqlen, GDN_Z), BF16),
                   jax.ShapeDtypeStruct((nseq, GDN_HEADS, GDN_DK, GDN_DV), F32)],
        scratch_shapes=[pltpu.VMEM((group, GDN_HEADS, GDN_DK, GDN_DV), F32)],
        compiler_params=_compiler_params(2),
        name="gdn_seq",
    )(view(q), view(k), view(v), view(z), view(ba), s0, alog, dtb, normw)
    return o.reshape(nseq * seqlen, GDN_Z), s_new


def _mlstm_seq_kernel(q_ref, k_ref, v_ref, og_ref, gates_ref, c0_ref, n0_ref, m0_ref, ibias_ref, fbias_ref,
                      h_ref, cout_ref, nout_ref, mout_ref,
                      cx_s, m_s, *, tc, chunk, n_valid):
    t = pl.program_id(1)
    heads = ML_HEADS
    sub = tc // chunk
    last = min(chunk, n_valid) - 1

    @pl.when(t == 0)
    def _():
        cx_s[:, :, :ML_DV] = c0_ref[...]
        cx_s[:, :, ML_DV:] = n0_ref[...]
        m_s[...] = m0_ref[...]

    gates = gates_ref[...]
    ig = gates[:, :LANES] + ibias_ref[...]
    lf = -_softplus(-(gates[:, LANES:] + fbias_ref[...]))

    ri = lax.broadcasted_iota(jnp.int32, (tc, tc), 0)
    ci = lax.broadcasted_iota(jnp.int32, (tc, tc), 1)
    incl = ci <= ri
    if sub > 1:
        incl = incl & ((ri // chunk) == (ci // chunk))
    mask = incl & (ci < n_valid) if n_valid < chunk else incl
    col_rows = lax.broadcasted_iota(jnp.int32, (chunk, 1), 0)
    ones_col = (lax.broadcasted_iota(jnp.int32, (tc, LANES), 1) == 0).astype(F32)
    lane_row = lax.broadcasted_iota(jnp.int32, (1, LANES), 1)
    chunk_rows = [slice(s * chunk, (s + 1) * chunk) for s in range(sub)]
    last_rows = [slice(s * chunk + last, s * chunk + last + 1) for s in range(sub)]
    hs = range(heads)

    bcum = _mask_dot_f32(incl, lf)
    bcum_t = bcum.T
    ig_t = ig.T
    dmats = [jnp.where(mask, bcum[:, h:h + 1] - bcum_t[h:h + 1, :] + ig_t[h:h + 1, :], -jnp.inf) for h in hs]
    dmaxs = [jnp.max(d, axis=-1, keepdims=True) for d in dmats]
    m_row = m_s[0:1, :]
    m_in = [m_row[:, h:h + 1] for h in hs]
    inters = [[] for _ in hs]
    m_ts = [[] for _ in hs]
    m_news = [[] for _ in hs]
    for r in chunk_rows:
        for h in hs:
            inter = bcum[r, h:h + 1] + m_in[h]
            m_t = jnp.maximum(inter, dmaxs[h][r])
            m_in[h] = m_t[last:last + 1, :]
            inters[h].append(inter)
            m_ts[h].append(m_t)
            m_news[h].append(m_in[h])
    inters = [jnp.concatenate(v, axis=0) for v in inters]
    m_ts = [jnp.concatenate(v, axis=0) for v in m_ts]
    w_intras = [jnp.exp(d - m) for d, m in zip(dmats, m_ts)]
    w_inters = [jnp.exp(i - m) for i, m in zip(inters, m_ts)]
    qs = [q_ref[:, h * ML_DK:(h + 1) * ML_DK] * ML_DK ** -0.5 for h in hs]
    ks = [k_ref[:, h * ML_DK:(h + 1) * ML_DK] for h in hs]
    vxs = [jnp.concatenate([v_ref[:, h * ML_DV:(h + 1) * ML_DV], ones_col], axis=1) for h in hs]
    scores = [_dot_nt(q, k) * w for q, k, w in zip(qs, ks, w_intras)]
    intra = [_dot(s, vx) for s, vx in zip(scores, vxs)]
    cxs = [cx_s[h] for h in hs]
    inter_num = [[] for _ in hs]
    for s, r in enumerate(chunk_rows):
        for h in hs:
            inter_num[h].append(_dot(qs[h][r], cxs[h]))
            m_new = m_news[h][s]
            w_state = jnp.exp(jnp.where(col_rows <= last, bcum[last_rows[s], h:h + 1] - bcum[r, h:h + 1]
                                        + ig[r, h:h + 1], -jnp.inf) - m_new)
            carry_decay = jnp.exp(inters[h][last_rows[s], :] - m_new)
            cxs[h] = carry_decay * cxs[h] + _dot_tn(ks[h][r] * w_state, vxs[h][r])
    m_next = m_row
    for h in hs:
        cx_s[h] = cxs[h]
        num = w_inters[h] * jnp.concatenate(inter_num[h], axis=0) + intra[h]
        den = num[:, ML_DV:ML_DV + 1]
        hh = num[:, :ML_DV] / jnp.maximum(jnp.abs(den), jnp.exp(-m_ts[h]))
        vcols = slice(h * ML_DV, (h + 1) * ML_DV)
        h_ref[:, vcols] = (hh * jax.nn.sigmoid(og_ref[:, vcols])).astype(h_ref.dtype)
        m_next = jnp.where(lane_row == h, m_in[h], m_next)
    m_s[0:1, :] = m_next

    @pl.when(t == pl.num_programs(1) - 1)
    def _():
        cout_ref[...] = cx_s[:, :, :ML_DV]
        nout_ref[...] = cx_s[:, :, ML_DV:]
        mout_ref[...] = m_s[...]


def _mlstm_seq(q, k, v, og, gates, c0, n0, m0, ibias, fbias, *, nseq, seqlen, tc, chunk, n_valid):
    nt = seqlen // tc
    row = lambda b, t: (b * nt + t, 0)
    seq3 = lambda b, t: (b, 0, 0)
    seq4 = lambda b, t: (b, 0, 0, 0)
    m = nseq * seqlen
    return pl.pallas_call(
        functools.partial(_mlstm_seq_kernel, tc=tc, chunk=chunk, n_valid=n_valid),
        grid=(nseq, nt),
        in_specs=[pl.BlockSpec((tc, ML_QK), row), pl.BlockSpec((tc, ML_QK), row),
                  pl.BlockSpec((tc, ML_V), row), pl.BlockSpec((tc, ML_V), row),
                  pl.BlockSpec((tc, 2 * LANES), row),
                  pl.BlockSpec((None, ML_HEADS, ML_DK, ML_DV), seq4),
                  pl.BlockSpec((None, ML_HEADS, ML_DK, LANES), seq4),
                  pl.BlockSpec((None, SUBLANES, LANES), seq3),
                  _const_spec((1, LANES)), _const_spec((1, LANES))],
        out_specs=[pl.BlockSpec((tc, ML_V), row),
                   pl.BlockSpec((None, ML_HEADS, ML_DK, ML_DV), seq4),
                   pl.BlockSpec((None, ML_HEADS, ML_DK, LANES), seq4),
                   pl.BlockSpec((None, SUBLANES, LANES), seq3)],
        out_shape=[jax.ShapeDtypeStruct((m, ML_V), BF16),
                   jax.ShapeDtypeStruct((nseq, ML_HEADS, ML_DK, ML_DV), F32),
                   jax.ShapeDtypeStruct((nseq, ML_HEADS, ML_DK, LANES), F32),
                   jax.ShapeDtypeStruct((nseq, SUBLANES, LANES), F32)],
        scratch_shapes=[pltpu.VMEM((ML_HEADS, ML_DK, ML_DV + LANES), F32),
                        pltpu.VMEM((SUBLANES, LANES), F32)],
        compiler_params=_compiler_params(2),
        name="mlstm_seq",
    )(q, k, v, og, gates, c0, n0, m0, ibias, fbias)


def _mla_proj_kernel(x_ref, g_ref, cos_ref, sin_ref, win_ref, qnorm_ref, kvnorm_ref, wuq_ref, wuqsw_ref, wkv_ref,
                     *o_refs, decode):
    xn = _rms(x_ref[...], g_ref[...]).astype(BF16)
    proj = jnp.dot(xn, win_ref[...], preferred_element_type=F32)
    cq = _rms(proj[:, :MLA_Q_RANK], qnorm_ref[...]).astype(BF16)
    ckv = _rms(proj[:, MLA_Q_RANK:MLA_Q_RANK + MLA_KV_RANK], kvnorm_ref[...])
    base = MLA_Q_RANK + MLA_KV_RANK
    cos = cos_ref[...]
    sin = sin_ref[...]
    kr = proj[:, base:base + LANES] * cos + proj[:, base + LANES:base + 2 * LANES] * sin
    q_main = jnp.dot(cq, wuq_ref[...], preferred_element_type=F32)
    q_swap = jnp.dot(cq, wuqsw_ref[...], preferred_element_type=F32)
    if decode:
        qlat_ref, qrope_ref, ckv_ref, kr_ref = o_refs
    else:
        qcat_ref, kcat_ref, vt_ref, ckv_ref, kr_ref = o_refs
        kv = jnp.dot(ckv.astype(BF16), wkv_ref[...], preferred_element_type=F32)
    ckv_ref[...] = ckv
    kr_ref[...] = kr[:, :MLA_ROPE]
    for h in range(MLA_HEADS):
        lo = h * 2 * LANES
        q_nope = q_main[:, lo:lo + LANES] * MLA_SCALE
        q_rope = (q_main[:, lo + LANES:lo + 2 * LANES] * cos + q_swap[:, h * LANES:(h + 1) * LANES] * sin) * MLA_SCALE
        if decode:
            qlat_ref[:, lo:lo + 2 * LANES] = _dot(q_nope, wkv_ref[h]).astype(BF16)
            qrope_ref[:, h * LANES:(h + 1) * LANES] = q_rope.astype(BF16)
        else:
            qcat_ref[:, lo:lo + LANES] = q_nope.astype(BF16)
            qcat_ref[:, lo + LANES:lo + 2 * LANES] = q_rope.astype(BF16)
            kcat_ref[:, lo:lo + LANES] = kv[:, lo:lo + LANES].astype(BF16)
            kcat_ref[:, lo + LANES:lo + 2 * LANES] = kr.astype(BF16)
            key_tile = vt_ref.shape[-1]
            for s in range(vt_ref.shape[0]):
                v_tile = kv[s * key_tile:(s + 1) * key_tile, lo + LANES:lo + 2 * LANES]
                vt_ref[s, h * MLA_V:(h + 1) * MLA_V, :] = v_tile.T.astype(BF16)


def _mla_proj(x, g, cos, sin, win, qnorm, kvnorm, wuq, wuqsw, wkv, *, tm, decode, key_tile=None):
    m = x.shape[0]
    n_pos_blocks = cos.shape[0] // tm
    row = lambda i: (i, 0)
    pos = lambda i: (i % n_pos_blocks, 0)
    wide = MLA_HEADS * 2 * LANES
    if decode:
        widths = (wide, MLA_HEADS * LANES, MLA_KV_RANK, MLA_ROPE)
        dtypes = (BF16, BF16, F32, F32)
    else:
        widths = (wide, wide, MLA_KV_RANK, MLA_ROPE)
        dtypes = (BF16, BF16, F32, F32)
    out_specs = [pl.BlockSpec((tm, n), row) for n in widths]
    out_shape = [jax.ShapeDtypeStruct((m, n), dt) for n, dt in zip(widths, dtypes)]
    if not decode:
        out_specs.insert(2, pl.BlockSpec((tm // key_tile, MLA_HEADS * MLA_V, key_tile), lambda i: (i, 0, 0)))
        out_shape.insert(2, jax.ShapeDtypeStruct((m // key_tile, MLA_HEADS * MLA_V, key_tile), BF16))
    return pl.pallas_call(
        functools.partial(_mla_proj_kernel, decode=decode),
        grid=(m // tm,),
        in_specs=[pl.BlockSpec((tm, D_MODEL), row), _const_spec((1, D_MODEL)),
                  pl.BlockSpec((tm, LANES), pos), pl.BlockSpec((tm, LANES), pos),
                  _const_spec(win.shape), _const_spec((1, MLA_Q_RANK)), _const_spec((1, MLA_KV_RANK)),
                  _const_spec(wuq.shape), _const_spec(wuqsw.shape), _const_spec(wkv.shape)],
        out_specs=out_specs,
        out_shape=out_shape,
        compiler_params=_compiler_params(1),
        name="mla_proj_decode" if decode else "mla_proj",
    )(x, g, cos, sin, win, qnorm, kvnorm, wuq, wuqsw, wkv)


def _mla_attn_kernel(q_ref, k_ref, vt_ref, o_ref, m_s, l_s, acc_s, *, tq):
    qi = pl.program_id(1)
    heads = MLA_HEADS
    wide = 2 * LANES
    m_s[...] = jnp.full(m_s.shape, -jnp.inf, F32)
    l_s[...] = jnp.zeros(l_s.shape, F32)
    acc_s[...] = jnp.zeros(acc_s.shape, F32)
    key_i = lax.broadcasted_iota(jnp.int32, (tq, tq), 0)
    query_i = lax.broadcasted_iota(jnp.int32, (tq, tq), 1)

    def key_tile(j, diagonal):
        rows = pl.ds(pl.multiple_of(j * tq, tq), tq)
        for g0 in range(0, heads, ATTN_HEAD_GROUP):
            hs = range(g0, g0 + ATTN_HEAD_GROUP)
            ss = [lax.dot_general(k_ref[rows, h * wide:(h + 1) * wide], q_ref[:, h * wide:(h + 1) * wide],
                                  (((1,), (1,)), ((), ())), preferred_element_type=F32) for h in hs]
            if diagonal:
                ss = [jnp.where(key_i <= query_i, s, -jnp.inf) for s in ss]
            m_old = [m_s[h] for h in hs]
            m_new = [jnp.maximum(m, jnp.max(s, axis=0, keepdims=True)) for m, s in zip(m_old, ss)]
            ps = [jnp.exp(s - m) for s, m in zip(ss, m_new)]
            alphas = [jnp.exp(mo - mn) for mo, mn in zip(m_old, m_new)]
            pvs = [jnp.dot(vt_ref[j, h * MLA_V:(h + 1) * MLA_V, :], p.astype(BF16), preferred_element_type=F32)
                   for p, h in zip(ps, hs)]
            for i, h in enumerate(hs):
                l_s[h] = alphas[i] * l_s[h] + jnp.sum(ps[i], axis=0, keepdims=True)
                acc_s[h] = alphas[i] * acc_s[h] + pvs[i]
                m_s[h] = m_new[i]

    def body(j, carry):
        key_tile(j, False)
        return carry

    lax.fori_loop(0, qi, body, 0)
    key_tile(qi, True)
    for h in range(heads):
        o_ref[:, h * MLA_V:(h + 1) * MLA_V] = (acc_s[h] / l_s[h]).T.astype(o_ref.dtype)


def _mla_attn(qcat, kcat, vt, *, nseq, seqlen, tq):
    nq = seqlen // tq
    m = nseq * seqlen
    return pl.pallas_call(
        functools.partial(_mla_attn_kernel, tq=tq),
        grid=(nseq, nq),
        in_specs=[pl.BlockSpec((tq, MLA_HEADS * 2 * LANES), lambda b, i: (b * nq + i, 0)),
                  pl.BlockSpec((seqlen, MLA_HEADS * 2 * LANES), lambda b, i: (b, 0)),
                  pl.BlockSpec((nq, MLA_HEADS * MLA_V, tq), lambda b, i: (b, 0, 0))],
        out_specs=pl.BlockSpec((tq, MLA_HEADS * MLA_V), lambda b, i: (b * nq + i, 0)),
        out_shape=jax.ShapeDtypeStruct((m, MLA_HEADS * MLA_V), BF16),
        scratch_shapes=[pltpu.VMEM((MLA_HEADS, 1, tq), F32), pltpu.VMEM((MLA_HEADS, 1, tq), F32),
                        pltpu.VMEM((MLA_HEADS, MLA_V, tq), F32)],
        compiler_params=_compiler_params(2),
        name="mla_attn",
    )(qcat, kcat, vt)


def _mla_decode_kernel(pt_ref, qlat_ref, qrope_ref, ckv_ref, kr_ref, wuv_ref, *rest, n_pages):
    lat_refs = rest[:n_pages]
    rope_refs = rest[n_pages:2 * n_pages]
    o_ref = rest[2 * n_pages]
    ql_s, qr_s, m_s, l_s, acc_s = rest[2 * n_pages + 1:]
    g = pl.program_id(1)
    heads = MLA_HEADS
    tok = DEC_PAD

    @pl.when(g == 0)
    def _():
        for h in range(heads):
            ql_s[h * tok:(h + 1) * tok, :] = qlat_ref[:, h * 2 * LANES:(h + 1) * 2 * LANES]
            qr_s[h * tok:(h + 1) * tok, :] = qrope_ref[:, h * LANES:(h + 1) * LANES]
        m_s[...] = jnp.full(m_s.shape, -jnp.inf, F32)
        l_s[...] = jnp.zeros(l_s.shape, F32)
        acc_s[...] = jnp.zeros(acc_s.shape, F32)

    ql = ql_s[...]
    qr = qr_s[:, :MLA_ROPE]

    def update(chains, scores, values):
        m_old = [m_s[c] for c in chains]
        m_new = [jnp.maximum(m, jnp.max(s, axis=-1, keepdims=True)) for m, s in zip(m_old, scores)]
        ps = [jnp.exp(s - m) for s, m in zip(scores, m_new)]
        alphas = [jnp.exp(mo - mn) for mo, mn in zip(m_old, m_new)]
        for i, c in enumerate(chains):
            l_s[c] = alphas[i] * l_s[c] + jnp.sum(ps[i], axis=-1, keepdims=True)
            pb = ps[i].astype(BF16)
            acc = alphas[i] * acc_s[c]
            for cols, val in values[i]:
                acc = acc + jnp.dot(pb[:, cols], val, preferred_element_type=F32)
            acc_s[c] = acc
            m_s[c] = m_new[i]

    n_chains = m_s.shape[0]
    per = n_pages // n_chains
    lats = [r[...].astype(BF16) for r in lat_refs]
    scores = [_dot_nt(ql, lat) + _dot(qr, r[...]) for lat, r in zip(lats, rope_refs)]
    update(range(n_chains),
           [jnp.concatenate(scores[c * per:(c + 1) * per], axis=1) for c in range(n_chains)],
           [[(slice(i * PAGE_SIZE, (i + 1) * PAGE_SIZE), lats[c * per + i]) for i in range(per)]
            for c in range(n_chains)])

    @pl.when(g == pl.num_programs(1) - 1)
    def _():
        ckv = ckv_ref[...].astype(BF16)
        s = _dot_nt(ql, ckv) + _dot_nt(qr, kr_ref[...])
        qt = lax.broadcasted_iota(jnp.int32, s.shape, 0) % tok
        kt = lax.broadcasted_iota(jnp.int32, s.shape, 1)
        update([0], [jnp.where(kt <= qt, s, -jnp.inf)], [[(slice(0, tok), ckv)]])
        m_all = m_s[0]
        for c in range(1, n_chains):
            m_all = jnp.maximum(m_all, m_s[c])
        weights = [jnp.exp(m_s[c] - m_all) for c in range(n_chains)]
        l_all = sum(w * l_s[c] for c, w in enumerate(weights))
        acc_all = sum(w * acc_s[c] for c, w in enumerate(weights))
        o_lat = acc_all / l_all
        for h in range(heads):
            o_ref[:, h * MLA_V:(h + 1) * MLA_V] = _dot(o_lat[h * tok:(h + 1) * tok, :], wuv_ref[h]).astype(o_ref.dtype)


def _mla_decode(page_table, qlat, qrope, ckv, kr, wuv, lat_pages, rope_pages, *, nseq):
    n_pages_seq = page_table.shape[1]
    n = PAGES_PER_STEP
    groups = n_pages_seq // n
    seq = lambda b, g, pt: (b, 0)

    def page_map(i):
        return lambda b, g, pt: (pt[b * n_pages_seq + g * n + i], 0, 0)

    rows = MLA_HEADS * DEC_PAD
    grid_spec = pltpu.PrefetchScalarGridSpec(
        num_scalar_prefetch=1,
        grid=(nseq, groups),
        in_specs=[pl.BlockSpec((DEC_PAD, MLA_HEADS * 2 * LANES), seq),
                  pl.BlockSpec((DEC_PAD, MLA_HEADS * LANES), seq),
                  pl.BlockSpec((DEC_PAD, MLA_KV_RANK), seq),
                  pl.BlockSpec((DEC_PAD, MLA_ROPE), seq),
                  pl.BlockSpec(wuv.shape, lambda b, g, pt: (0, 0, 0))]
                 + [pl.BlockSpec((None, PAGE_SIZE, MLA_KV_RANK), page_map(i)) for i in range(n)]
                 + [pl.BlockSpec((None, MLA_ROPE, PAGE_SIZE), page_map(i)) for i in range(n)],
        out_specs=pl.BlockSpec((DEC_PAD, MLA_HEADS * MLA_V), seq),
        scratch_shapes=[pltpu.VMEM((rows, MLA_KV_RANK), BF16), pltpu.VMEM((rows, LANES), BF16),
                        pltpu.VMEM((DECODE_CHAINS, rows, 1), F32), pltpu.VMEM((DECODE_CHAINS, rows, 1), F32),
                        pltpu.VMEM((DECODE_CHAINS, rows, MLA_KV_RANK), F32)],
    )
    return pl.pallas_call(
        functools.partial(_mla_decode_kernel, n_pages=n),
        grid_spec=grid_spec,
        out_shape=jax.ShapeDtypeStruct((nseq * DEC_PAD, MLA_HEADS * MLA_V), BF16),
        compiler_params=_compiler_params(2),
        name="mla_decode",
    )(page_table.reshape(-1), qlat, qrope, ckv, kr, wuv, *([lat_pages] * n), *([rope_pages] * n))


def _row(v, width=None):
    v = v.astype(F32).reshape(1, -1)
    if width is not None and v.shape[1] < width:
        v = jnp.pad(v, ((0, 0), (0, width - v.shape[1])))
    return v


def _pad_cols(w, width):
    return jnp.pad(w, ((0, 0), (0, width - w.shape[1])))


def _prep_ffn(w_up, w_down):
    return w_up.astype(BF16), w_down.reshape(w_down.shape[:2] + (N_FF_CHUNKS, FF_CHUNK, D_MODEL)).astype(BF16)


def _prep_gdn(w_in, conv_w, a_log, dt_bias, norm_w, w_out):
    main = GDN_QKV + GDN_Z
    w = jnp.concatenate([w_in[:, :main], _pad_cols(w_in[:, main:main + GDN_HEADS], LANES),
                         _pad_cols(w_in[:, main + GDN_HEADS:], LANES)], axis=1).astype(BF16)
    convw = jnp.pad(conv_w.astype(F32), ((0, SUBLANES - GDN_CONV), (0, 0)))
    return w, convw, _row(a_log, LANES), _row(dt_bias, LANES), _row(norm_w), w_out.astype(BF16)


def _prep_mlstm(w_in, i_bias, f_bias, w_out):
    main = 2 * ML_QK + 2 * ML_V
    w = jnp.concatenate([w_in[:, :main], _pad_cols(w_in[:, main:main + ML_HEADS], LANES),
                         _pad_cols(w_in[:, main + ML_HEADS:], LANES)], axis=1).astype(BF16)
    return w, _row(i_bias, LANES), _row(f_bias, LANES), w_out.astype(BF16)


def _swap_halves(w):
    half = w.shape[-1] // 2
    return jnp.concatenate([w[..., half:], w[..., :half]], axis=-1)


def _prep_mla(w_in, q_norm, kv_norm, w_uq, w_ukv, w_out):
    base = MLA_Q_RANK + MLA_KV_RANK
    kr = w_in[:, base:]
    win = jnp.concatenate([w_in[:, :base], _pad_cols(kr, LANES), _pad_cols(_swap_halves(kr), LANES)],
                          axis=1).astype(BF16)
    wq = w_uq.reshape(MLA_Q_RANK, MLA_HEADS, MLA_NOPE + MLA_ROPE)
    nope, rope = wq[..., :MLA_NOPE], wq[..., MLA_NOPE:]
    zeros = jnp.zeros((MLA_Q_RANK, MLA_HEADS, LANES - MLA_ROPE), w_uq.dtype)
    wuq = jnp.concatenate([nope, rope, zeros], axis=-1).reshape(MLA_Q_RANK, -1).astype(BF16)
    wuqsw = jnp.concatenate([_swap_halves(rope), zeros], axis=-1).reshape(MLA_Q_RANK, -1).astype(BF16)
    wkv3 = w_ukv.reshape(MLA_KV_RANK, MLA_HEADS, MLA_NOPE + MLA_V)
    wuk_t = wkv3[..., :MLA_NOPE].transpose(1, 2, 0).astype(BF16)
    wuv = wkv3[..., MLA_NOPE:].transpose(1, 0, 2).astype(BF16)
    return dict(win=win, qnorm=_row(q_norm), kvnorm=_row(kv_norm), wuq=wuq, wuqsw=wuqsw,
                wkv=w_ukv.astype(BF16), wuk_t=wuk_t, wuv=wuv, wout=w_out.astype(BF16))


def _rope_tables(pos):
    half = MLA_ROPE // 2
    inv = ROPE_THETA ** (-jnp.arange(half, dtype=F32) / half)
    ang = pos.astype(F32)[:, None] * inv
    cos, sin = jnp.cos(ang), jnp.sin(ang)
    pad = jnp.zeros((pos.shape[0], LANES - MLA_ROPE), F32)
    return jnp.concatenate([cos, cos, pad], axis=1), jnp.concatenate([-sin, sin, pad], axis=1)


def _tiles(nseq, seqlen):
    m = nseq * seqlen
    return dict(ffn=min(512, m), proj=min(512, m), mla_proj=min(512, seqlen), attn_q=min(256, seqlen),
                gdn=min(2 * GDN_CHUNK, seqlen),
                gdn_group=math.gcd(GDN_SEQ_GROUP if seqlen >= 2 * GDN_CHUNK else GDN_SEQ_GROUP_SHORT, nseq),
                mlstm=min(2 * ML_CHUNK, seqlen))


def _trunk(x, *, nseq, seqlen, n_tokens, decode, states, page_table, weights):
    gdn_S, gdn_conv, mla_lat, mla_rope, ml_C, ml_n, ml_m = states
    chunk_g = min(GDN_CHUNK, seqlen)
    chunk_m = min(ML_CHUNK, seqlen)
    tiles = _tiles(nseq, seqlen)
    tm = tiles["ffn"]
    new = ([], [], [], [], [], [], [])
    counts = [0, 0, 0]
    for layer in range(4):
        kind = layer % 3
        j = counts[kind]
        counts[kind] += 1
        ng = weights["gains"][layer]
        wup, wdn = weights["ffn"]
        x = _ffn_half(x, ng[0], ng[1], wup, wdn, layer, 0, tm)
        if kind == 0:
            w, convw, alog, dtb, normw, wout = weights["gdn"][j]
            q, k, v, z, ba, conv_new = _gdn_proj(x, ng[2], w, convw, gdn_conv, layer=j, nseq=nseq, seqlen=seqlen,
                                                 n_tokens=n_tokens, tm=tiles["proj"])
            o, s_new = _gdn_seq(q, k, v, z, ba, gdn_S, alog, dtb, normw, layer=j, nseq=nseq, seqlen=seqlen,
                                tc=tiles["gdn"], chunk=chunk_g, n_valid=min(tiles["gdn"], n_tokens),
                                group=tiles["gdn_group"])
            new[0].append(s_new)
            new[1].append(conv_new[:, SUBLANES - (GDN_CONV - 1):, :])
        elif kind == 1:
            p = weights["mla"][j]
            cos, sin = weights["rope"]
            if decode:
                qlat, qrope, ckv, kr = _mla_proj(x, ng[2], cos, sin, p["win"], p["qnorm"], p["kvnorm"], p["wuq"],
                                                 p["wuqsw"], p["wuk_t"], tm=tiles["mla_proj"], decode=True)
                o = _mla_decode(page_table, qlat, qrope, ckv, kr, p["wuv"], mla_lat[j], mla_rope[j], nseq=nseq)
            else:
                qcat, kcat, vt, ckv, kr = _mla_proj(x, ng[2], cos, sin, p["win"], p["qnorm"], p["kvnorm"], p["wuq"],
                                                    p["wuqsw"], p["wkv"], tm=tiles["mla_proj"], decode=False,
                                                    key_tile=tiles["attn_q"])
                o = _mla_attn(qcat, kcat, vt, nseq=nseq, seqlen=seqlen, tq=tiles["attn_q"])
            wout = p["wout"]
            new[2].append(ckv)
            new[3].append(kr)
        else:
            w, ibias, fbias, wout = weights["mlstm"][j]
            q, k, v, og, gates = _norm_matmul(x, ng[2], w, (ML_QK, ML_QK, ML_V, ML_V, 2 * LANES), tiles["proj"])
            o, c_new, n_new, m_new = _mlstm_seq(q, k, v, og, gates, ml_C[j], ml_n[j], ml_m[j], ibias, fbias,
                                                nseq=nseq, seqlen=seqlen, tc=tiles["mlstm"], chunk=chunk_m,
                                                n_valid=min(tiles["mlstm"], n_tokens))
            new[4].append(c_new)
            new[5].append(n_new[..., 0])
            new[6].append(m_new[:, 0, :ML_HEADS])
        x = _ffn_half(x, ng[4], ng[5], wup, wdn, layer, 1, tm, mixer_out=(o, ng[3], wout))
    return x, [jnp.stack(s) for s in new]


def kernel(x_prompt, x_sample, state_gdn_S, state_gdn_conv, cache_mla_latent, cache_mla_rope, state_mlstm_C, state_mlstm_n, state_mlstm_m, page_table, norm_gains, w_ffn_up, w_ffn_down, gdn_w_in, gdn_conv_w, gdn_a_log, gdn_dt_bias, gdn_norm_w, gdn_w_out, mla_w_in, mla_q_norm, mla_kv_norm, mla_w_uq, mla_w_ukv, mla_w_out, mlstm_w_in, mlstm_i_bias, mlstm_f_bias, mlstm_w_out):
    nb, seq, _ = x_prompt.shape
    db, dseq, _ = x_sample.shape
    n_gdn, n_mla, n_ml = gdn_w_in.shape[0], mla_w_in.shape[0], mlstm_w_in.shape[0]
    past = page_table.shape[1] * PAGE_SIZE

    weights = dict(
        gains=[[_row(norm_gains[l, i]) for i in range(6)] for l in range(4)],
        ffn=_prep_ffn(w_ffn_up, w_ffn_down),
        gdn=[_prep_gdn(gdn_w_in[j], gdn_conv_w[j], gdn_a_log[j], gdn_dt_bias[j], gdn_norm_w[j], gdn_w_out[j])
             for j in range(n_gdn)],
        mla=[_prep_mla(mla_w_in[j], mla_q_norm[j], mla_kv_norm[j], mla_w_uq[j], mla_w_ukv[j], mla_w_out[j])
             for j in range(n_mla)],
        mlstm=[_prep_mlstm(mlstm_w_in[j], mlstm_i_bias[j], mlstm_f_bias[j], mlstm_w_out[j]) for j in range(n_ml)],
    )

    zeros = lambda *s: jnp.zeros(s, F32)
    states_p = (zeros(n_gdn, nb, GDN_HEADS, GDN_DK, GDN_DV), zeros(n_gdn, nb, SUBLANES, GDN_QKV), None, None,
                zeros(n_ml, nb, ML_HEADS, ML_DK, ML_DV), zeros(n_ml, nb, ML_HEADS, ML_DK, LANES),
                zeros(n_ml, nb, SUBLANES, LANES))
    y_p, st_p = _trunk(x_prompt.reshape(nb * seq, D_MODEL), nseq=nb, seqlen=seq, n_tokens=seq, decode=False,
                       states=states_p, page_table=None,
                       weights=dict(weights, rope=_rope_tables(jnp.arange(seq))))

    pad_t = DEC_PAD - dseq
    x_s = jnp.pad(x_sample, ((0, 0), (0, pad_t), (0, 0))).reshape(db * DEC_PAD, D_MODEL)
    conv0 = jnp.pad(state_gdn_conv, ((0, 0), (0, 0), (SUBLANES - (GDN_CONV - 1), 0), (0, 0)))
    n0 = jnp.pad(state_mlstm_n[..., None], ((0, 0),) * 4 + ((0, LANES - 1),))
    m0 = jnp.pad(state_mlstm_m[:, :, None, :], ((0, 0), (0, 0), (0, SUBLANES - 1), (0, LANES - ML_HEADS)))
    states_s = (state_gdn_S, conv0,
                cache_mla_latent.reshape((n_mla, -1) + cache_mla_latent.shape[2:]),
                jnp.swapaxes(cache_mla_rope.reshape((n_mla, -1) + cache_mla_rope.shape[2:]), -1, -2),
                state_mlstm_C, n0, m0)
    pos_s = jnp.tile(past + jnp.arange(DEC_PAD), db)
    y_s, st_s = _trunk(x_s, nseq=db, seqlen=DEC_PAD, n_tokens=dseq, decode=True, states=states_s,
                       page_table=page_table, weights=dict(weights, rope=_rope_tables(pos_s)))

    gdn_S_p, gdn_conv_p, lat_p, rope_p, ml_C_p, ml_n_p, ml_m_p = st_p
    gdn_S_s, gdn_conv_s, lat_s, rope_s, ml_C_s, ml_n_s, ml_m_s = st_s
    unpad = lambda a: a.reshape(a.shape[0], db, DEC_PAD, a.shape[-1])[:, :, :dseq]
    return (y_p.reshape(nb, seq, D_MODEL), unpad(y_s[None])[0],
            gdn_S_p, gdn_S_s, gdn_conv_p, gdn_conv_s,
            lat_p.reshape(n_mla, -1, PAGE_SIZE, MLA_KV_RANK), unpad(lat_s),
            rope_p.reshape(n_mla, -1, PAGE_SIZE, MLA_ROPE), unpad(rope_s),
            ml_C_p, ml_C_s, ml_n_p, ml_n_s, ml_m_p, ml_m_s)
```

```python
import functools
import math

import jax
import jax.numpy as jnp
from jax import lax
from jax.experimental import pallas as pl
from jax.experimental.pallas import tpu as pltpu

F32 = jnp.float32
BF16 = jnp.bfloat16

D_MODEL = 1024
PAGE_SIZE = 128
EPS = 1e-6

GDN_HEADS = 8
GDN_DK = 128
GDN_DV = 128
GDN_CONV = 4
GDN_CHUNK = 64
GDN_QKV = GDN_HEADS * (2 * GDN_DK + GDN_DV)
GDN_Z = GDN_HEADS * GDN_DV

MLA_HEADS = 8
MLA_NOPE = 128
MLA_ROPE = 64
MLA_V = 128
MLA_Q_RANK = 384
MLA_KV_RANK = 256
MLA_SCALE = (MLA_NOPE + MLA_ROPE) ** -0.5
ROPE_THETA = 10000.0

ML_HEADS = 4
ML_DK = 128
ML_DV = 256
ML_CHUNK = 64
ML_QK = ML_HEADS * ML_DK
ML_V = ML_HEADS * ML_DV

D_FF = 2816
FF_CHUNK = 256
N_FF_CHUNKS = D_FF // FF_CHUNK

LANES = 128
SUBLANES = 8
DEC_PAD = SUBLANES
VMEM_LIMIT = 56 * 1024 * 1024
PAGES_PER_STEP = 64
DECODE_CHAINS = 4
ATTN_HEAD_GROUP = 8
GDN_SEQ_GROUP = 2
GDN_SEQ_GROUP_SHORT = 4
SOLVE_REFINEMENTS = 0


def _rms(x, g):
    return x * lax.rsqrt(jnp.mean(x * x, axis=-1, keepdims=True) + EPS) * g


def _silu(x):
    return x * jax.nn.sigmoid(x)


def _softplus(x):
    return jnp.maximum(x, 0.0) + jnp.log1p(jnp.exp(-jnp.abs(x)))


def _dot(a, b):
    return jnp.dot(a.astype(BF16), b.astype(BF16), preferred_element_type=F32)


def _dot_nt(a, b):
    return lax.dot_general(a.astype(BF16), b.astype(BF16), (((1,), (1,)), ((), ())),
                           preferred_element_type=F32)


def _dot_tn(a, b):
    return lax.dot_general(a.astype(BF16), b.astype(BF16), (((0,), (0,)), ((), ())),
                           preferred_element_type=F32)


def _mask_dot_f32(mask, x):
    m = mask.astype(BF16)
    hi = x.astype(BF16)
    rest = x - hi.astype(F32)
    mid = rest.astype(BF16)
    lo = (rest - mid.astype(F32)).astype(BF16)
    if m.shape[1] % LANES == 0:
        return jnp.dot(jnp.concatenate([m, m, m], axis=1), jnp.concatenate([hi, mid, lo], axis=0),
                       preferred_element_type=F32)
    return sum(jnp.dot(m, t, preferred_element_type=F32) for t in (hi, mid, lo))


def _split(x):
    hi = x.astype(BF16)
    return hi, (x - hi.astype(F32)).astype(BF16)


def _dot3(a, b):
    (a_hi, a_lo), (b_hi, b_lo) = a, b
    if a_hi.shape[1] % LANES == 0:
        return jnp.dot(jnp.concatenate([a_hi, a_lo, a_hi], axis=1), jnp.concatenate([b_hi, b_hi, b_lo], axis=0),
                       preferred_element_type=F32)
    m = a_hi.shape[0]
    r = jnp.dot(jnp.concatenate([a_hi, a_lo], axis=0), b_hi, preferred_element_type=F32)
    return r[:m] + r[m:] + jnp.dot(a_hi, b_lo, preferred_element_type=F32)


def _unit_lower_inverse_minus_eye(a_mats, ri, ci, chunk):
    rs = None
    s = 1
    while s < chunk:
        shift = s.bit_length() - 1
        off = ((ri >> (shift + 1)) == (ci >> (shift + 1))) & ((ri >> shift) != (ci >> shift))
        a_offs = [jnp.where(off, a, 0.0) for a in a_mats]
        if rs is None:
            rs = [-a for a in a_offs]
        else:
            bs = [a + _dot(r, a) for a, r in zip(a_offs, rs)]
            rs = [r - b - _dot(b, r) for r, b in zip(rs, bs)]
        s *= 2
    return rs


def _unit_lower_solve(a_mats, rhss, ri, ci, chunk):
    rs = _unit_lower_inverse_minus_eye(a_mats, ri, ci, chunk)
    a_sp = [_split(a) for a in a_mats]
    xs = [rhs + _dot(r, rhs) for r, rhs in zip(rs, rhss)]
    for _ in range(SOLVE_REFINEMENTS):
        resid = [rhs - x - _dot3(a, _split(x)) for rhs, x, a in zip(rhss, xs, a_sp)]
        xs = [x + e + _dot(r, e) for x, e, r in zip(xs, resid, rs)]
    return xs


def _compiler_params(n_axes):
    return pltpu.CompilerParams(dimension_semantics=("arbitrary",) * n_axes,
                                vmem_limit_bytes=VMEM_LIMIT)


SINGLE_BUFFER = pl.Buffered(1)


def _const_spec(shape):
    nd = len(shape)
    return pl.BlockSpec(shape, lambda *_: (0,) * nd, pipeline_mode=SINGLE_BUFFER)


def _ffn_kernel(*refs, mixer_out):
    if mixer_out:
        a_ref, gmix_ref, wout_ref, x_ref, gpre_ref, gpost_ref, wup_ref, wdn_ref, o_ref = refs
        x = x_ref[...] + _rms(jnp.dot(a_ref[...], wout_ref[...], preferred_element_type=F32), gmix_ref[...])
    else:
        x_ref, gpre_ref, gpost_ref, wup_ref, wdn_ref, o_ref = refs
        x = x_ref[...]
    xn = _rms(x, gpre_ref[...]).astype(BF16)
    acc = jnp.zeros(x.shape, F32)
    for c in range(N_FF_CHUNKS):
        lo = c * FF_CHUNK
        gate = jnp.dot(xn, wup_ref[:, lo:lo + FF_CHUNK], preferred_element_type=F32)
        up = jnp.dot(xn, wup_ref[:, D_FF + lo:D_FF + lo + FF_CHUNK], preferred_element_type=F32)
        act = (_silu(gate) * up).astype(BF16)
        acc = acc + jnp.dot(act, wdn_ref[c], preferred_element_type=F32)
    o_ref[...] = x + 0.5 * _rms(acc, gpost_ref[...])


def _ffn_half(x, gpre, gpost, wup, wdn, layer, half, tm, mixer_out=None):
    m = x.shape[0]
    row = pl.BlockSpec((tm, D_MODEL), lambda i: (i, 0))
    gain = _const_spec((1, D_MODEL))
    in_specs = [row, gain, gain,
                pl.BlockSpec((None, None) + wup.shape[2:], lambda i: (layer, half, 0, 0),
                             pipeline_mode=SINGLE_BUFFER),
                pl.BlockSpec((None, None) + wdn.shape[2:], lambda i: (layer, half, 0, 0, 0),
                             pipeline_mode=SINGLE_BUFFER)]
    args = (x, gpre, gpost, wup, wdn)
    if mixer_out is not None:
        a, gmix, wout = mixer_out
        in_specs = [row, gain, _const_spec(wout.shape)] + in_specs
        args = (a, gmix, wout) + args
    return pl.pallas_call(
        functools.partial(_ffn_kernel, mixer_out=mixer_out is not None),
        grid=(m // tm,),
        in_specs=in_specs,
        out_specs=row,
        out_shape=jax.ShapeDtypeStruct((m, D_MODEL), F32),
        compiler_params=_compiler_params(1),
        name="ffn_half",
    )(*args)


def _norm_matmul_kernel(x_ref, g_ref, w_ref, *o_refs, splits):
    xn = _rms(x_ref[...], g_ref[...]).astype(BF16)
    off = 0
    for o_ref, n in zip(o_refs, splits):
        o_ref[...] = jnp.dot(xn, w_ref[:, off:off + n], preferred_element_type=F32)
        off += n


def _norm_matmul(x, g, w, splits, tm):
    m = x.shape[0]
    return pl.pallas_call(
        functools.partial(_norm_matmul_kernel, splits=splits),
        grid=(m // tm,),
        in_specs=[pl.BlockSpec((tm, D_MODEL), lambda i: (i, 0)),
                  _const_spec((1, D_MODEL)), _const_spec(w.shape)],
        out_specs=[pl.BlockSpec((tm, n), lambda i: (i, 0)) for n in splits],
        out_shape=[jax.ShapeDtypeStruct((m, n), F32) for n in splits],
        compiler_params=_compiler_params(1),
        name="norm_matmul",
    )(x, g, w)


def _gdn_proj_kernel(x_ref, g_ref, w_ref, convw_ref, conv0_ref, q_ref, k_ref, v_ref, z_ref, ba_ref, convout_ref,
                     xext, *, seg, n_valid, tiles_per_seq):
    tm = x_ref.shape[0]
    nsub = tm // seg
    prev = SUBLANES - (GDN_CONV - 1)
    xn = _rms(x_ref[...], g_ref[...]).astype(BF16)
    if tiles_per_seq == 1:
        xext[:, 0:SUBLANES, :] = conv0_ref[...]
    else:
        @pl.when(pl.program_id(0) % tiles_per_seq == 0)
        def _():
            xext[:, 0:SUBLANES, :] = conv0_ref[...]
    for part, dest in enumerate((q_ref, k_ref, v_ref)):
        for h0 in range(0, GDN_HEADS, 2):
            base = part * GDN_HEADS * GDN_DK + h0 * GDN_DK
            pre2 = jnp.dot(xn, w_ref[:, base:base + 2 * GDN_DK], preferred_element_type=F32)
            for h in (h0, h0 + 1):
                c0 = part * GDN_HEADS * GDN_DK + h * GDN_DK
                cols = slice(c0, c0 + GDN_DK)
                pre = pre2[:, c0 - base:c0 - base + GDN_DK].reshape(nsub, seg, GDN_DK)
                xext[:, SUBLANES:SUBLANES + seg, cols] = pre
                y = pre * convw_ref[GDN_CONV - 1:GDN_CONV, cols]
                for j in range(GDN_CONV - 1):
                    y = y + xext[:, prev + j:prev + j + seg, cols] * convw_ref[j:j + 1, cols]
                y = _silu(y)
                if part < 2:
                    y = y * lax.rsqrt(jnp.sum(y * y, axis=-1, keepdims=True) + EPS)
                if part == 0:
                    y = y * GDN_DK ** -0.5
                dest[:, h * GDN_DK:(h + 1) * GDN_DK] = y.reshape(tm, GDN_DK)
    z_ref[...] = jnp.dot(xn, w_ref[:, GDN_QKV:GDN_QKV + GDN_Z], preferred_element_type=F32)
    ba_ref[...] = jnp.dot(xn, w_ref[:, GDN_QKV + GDN_Z:], preferred_element_type=F32)
    convout_ref[...] = xext[:, n_valid:n_valid + SUBLANES, :]
    if tiles_per_seq > 1:
        xext[:, 0:SUBLANES, :] = xext[:, seg:seg + SUBLANES, :]


def _gdn_proj(x, g, w, convw, conv0, *, layer, nseq, seqlen, n_tokens, tm):
    m = x.shape[0]
    seg = min(tm, seqlen)
    nsub = tm // seg
    tiles_per_seq = seqlen // seg
    row = lambda n: pl.BlockSpec((tm, n), lambda i: (i, 0))
    per_seq = lambda i: (i // tiles_per_seq, 0, 0)
    return pl.pallas_call(
        functools.partial(_gdn_proj_kernel, seg=seg, n_valid=min(seg, n_tokens), tiles_per_seq=tiles_per_seq),
        grid=(m // tm,),
        in_specs=[row(D_MODEL), _const_spec((1, D_MODEL)), _const_spec(w.shape), _const_spec((SUBLANES, GDN_QKV)),
                  pl.BlockSpec((None, nsub, SUBLANES, GDN_QKV), lambda i: (layer,) + per_seq(i))],
        out_specs=[row(GDN_Z), row(GDN_Z), row(GDN_Z), row(GDN_Z), row(2 * LANES),
                   pl.BlockSpec((nsub, SUBLANES, GDN_QKV), per_seq)],
        out_shape=[jax.ShapeDtypeStruct((m, GDN_Z), F32)] * 4 + [jax.ShapeDtypeStruct((m, 2 * LANES), F32),
                                                                jax.ShapeDtypeStruct((nseq, SUBLANES, GDN_QKV), F32)],
        scratch_shapes=[pltpu.VMEM((nsub, seg + SUBLANES, GDN_QKV), F32)],
        compiler_params=_compiler_params(1),
        name="gdn_proj",
    )(x, g, w, convw, conv0)


def _gdn_seq_kernel(q_ref, k_ref, v_ref, z_ref, ba_ref, s0_ref, alog_ref, dtb_ref, normw_ref,
                    o_ref, sout_ref, s_s, *, tc, chunk, n_valid):
    t = pl.program_id(1)
    heads = GDN_HEADS
    group = q_ref.shape[0]
    head_cols = lambda h: slice(h * GDN_DK, (h + 1) * GDN_DK)
    sub = tc // chunk
    last = min(chunk, n_valid) - 1

    @pl.when(t == 0)
    def _():
        s_s[...] = s0_ref[...]

    ri = lax.broadcasted_iota(jnp.int32, (tc, tc), 0)
    ci = lax.broadcasted_iota(jnp.int32, (tc, tc), 1)
    incl = ci <= ri
    if sub > 1:
        incl = incl & ((ri // chunk) == (ci // chunk))
    strict = incl & (ci < ri)
    chunk_rows = [slice(s * chunk, (s + 1) * chunk) for s in range(sub)]
    beta, gc, gc_t, exp_gc, exp_last, exp_rest = [], [], [], [], [], []
    for p in range(group):
        ba = ba_ref[p]
        b = jax.nn.sigmoid(ba[:, :LANES])
        if n_valid < tc:
            b = jnp.where(lax.broadcasted_iota(jnp.int32, b.shape, 0) < n_valid, b, 0.0)
        g = -jnp.exp(alog_ref[...]) * _softplus(ba[:, LANES:] + dtb_ref[...])
        c = _mask_dot_f32(incl, g)
        c_last = [c[s * chunk + last:s * chunk + last + 1, :] for s in range(sub)]
        beta.append(b)
        gc.append(c)
        gc_t.append(c.T)
        exp_gc.append(jnp.exp(c))
        exp_last.append([jnp.exp(v) for v in c_last])
        exp_rest.append(jnp.exp(jnp.concatenate([jnp.broadcast_to(v, (chunk, LANES)) for v in c_last], axis=0) - c))
    normw = normw_ref[...]

    units = [(p, h) for p in range(group) for h in range(heads)]
    col = lambda x, h: x[:, h:h + 1]
    ks = {(p, h): k_ref[p, :, head_cols(h)] for p, h in units}
    bs = {(p, h): col(beta[p], h) for p, h in units}
    kbs = {u: ks[u] * bs[u] for u in units}
    decays = {(p, h): jnp.exp(jnp.where(incl, col(gc[p], h) - gc_t[p][h:h + 1, :], -jnp.inf)) for p, h in units}
    a_mats = [_dot_nt(kbs[u], ks[u]) * jnp.where(strict, decays[u], 0.0) for u in units]
    rhss = [jnp.concatenate([v_ref[p, :, head_cols(h)] * bs[p, h], kbs[p, h] * col(exp_gc[p], h)], axis=1) for p, h in units]
    sols = dict(zip(units, _unit_lower_solve(a_mats, rhss, ri, ci, chunk)))
    qs = {(p, h): q_ref[p, :, head_cols(h)] for p, h in units}
    attns = {u: _dot_nt(qs[u], ks[u]) * decays[u] for u in units}
    qgs = {(p, h): qs[p, h] * col(exp_gc[p], h) for p, h in units}
    kdecs = {(p, h): ks[p, h] * col(exp_rest[p], h) for p, h in units}
    states = {u: s_s[u] for u in units}
    v_new = {u: [] for u in units}
    o_inter = {u: [] for u in units}
    for s, r in enumerate(chunk_rows):
        for p, h in units:
            u = (p, h)
            vn = sols[u][r, :GDN_DV] - _dot(sols[u][r, GDN_DV:], states[u])
            o_inter[u].append(_dot(qgs[u][r], states[u]))
            states[u] = states[u] * col(exp_last[p][s], h) + _dot_tn(kdecs[u][r], vn)
            v_new[u].append(vn)
    for p, h in units:
        u = (p, h)
        s_s[u] = states[u]
        o = jnp.concatenate(o_inter[u], axis=0) + _dot(attns[u], jnp.concatenate(v_new[u], axis=0))
        cols = slice(h * GDN_DV, (h + 1) * GDN_DV)
        o_ref[p, :, cols] = (_rms(o, normw) * _silu(z_ref[p, :, cols])).astype(o_ref.dtype)

    @pl.when(t == pl.num_programs(1) - 1)
    def _():
        sout_ref[...] = s_s[...]


def _gdn_seq(q, k, v, z, ba, s0, alog, dtb, normw, *, layer, nseq, seqlen, tc, chunk, n_valid, group):
    nt = seqlen // tc
    tok = lambda b, t: (b, t, 0)
    view = lambda a: a.reshape(nseq, seqlen, a.shape[-1])
    wide = pl.BlockSpec((group, tc, GDN_Z), tok)
    o, s_new = pl.pallas_call(
        functools.partial(_gdn_seq_kernel, tc=tc, chunk=chunk, n_valid=n_valid),
        grid=(nseq // group, nt),
        in_specs=[wide, wide, wide, wide, pl.BlockSpec((group, tc, 2 * LANES), tok),
                  pl.BlockSpec((None, group, GDN_HEADS, GDN_DK, GDN_DV), lambda b, t: (layer, b, 0, 0, 0)),
                  _const_spec((1, LANES)), _const_spec((1, LANES)), _const_spec((1, GDN_DV))],
        out_specs=[wide, pl.BlockSpec((group, GDN_HEADS, GDN_DK, GDN_DV), lambda b, t: (b, 0, 0, 0))],
        out_shape=[jax.ShapeDtypeStruct((nseq, seqlen, GDN_Z), BF16),
                   jax.ShapeDtypeStruct((nseq, GDN_HEADS, GDN_DK, GDN_DV), F32)],
        scratch_shapes=[pltpu.VMEM((group, GDN_HEADS, GDN_DK, GDN_DV), F32)],
        compiler_params=_compiler_params(2),
        name="gdn_seq",
    )(view(q), view(k), view(v), view(z), view(ba), s0, alog, dtb, normw)
    return o.reshape(nseq * seqlen, GDN_Z), s_new


def _mlstm_seq_kernel(q_ref, k_ref, v_ref, og_ref, gates_ref, c0_ref, n0_ref, m0_ref, ibias_ref, fbias_ref,
                      h_ref, cout_ref, nout_ref, mout_ref,
                      cx_s, m_s, *, tc, chunk, n_valid):
    t = pl.program_id(1)
    heads = ML_HEADS
    sub = tc // chunk
    last = min(chunk, n_valid) - 1

    @pl.when(t == 0)
    def _():
        cx_s[:, :, :ML_DV] = c0_ref[...]
        cx_s[:, :, ML_DV:] = n0_ref[...]
        m_s[...] = m0_ref[...]

    gates = gates_ref[...]
    ig = gates[:, :LANES] + ibias_ref[...]
    lf = -_softplus(-(gates[:, LANES:] + fbias_ref[...]))

    ri = lax.broadcasted_iota(jnp.int32, (tc, tc), 0)
    ci = lax.broadcasted_iota(jnp.int32, (tc, tc), 1)
    incl = ci <= ri
    if sub > 1:
        incl = incl & ((ri // chunk) == (ci // chunk))
    mask = incl & (ci < n_valid) if n_valid < chunk else incl
    col_rows = lax.broadcasted_iota(jnp.int32, (chunk, 1), 0)
    ones_col = (lax.broadcasted_iota(jnp.int32, (tc, LANES), 1) == 0).astype(F32)
    lane_row = lax.broadcasted_iota(jnp.int32, (1, LANES), 1)
    chunk_rows = [slice(s * chunk, (s + 1) * chunk) for s in range(sub)]
    last_rows = [slice(s * chunk + last, s * chunk + last + 1) for s in range(sub)]
    hs = range(heads)

    bcum = _mask_dot_f32(incl, lf)
    bcum_t = bcum.T
    ig_t = ig.T
    dmats = [jnp.where(mask, bcum[:, h:h + 1] - bcum_t[h:h + 1, :] + ig_t[h:h + 1, :], -jnp.inf) for h in hs]
    dmaxs = [jnp.max(d, axis=-1, keepdims=True) for d in dmats]
    m_row = m_s[0:1, :]
    m_in = [m_row[:, h:h + 1] for h in hs]
    inters = [[] for _ in hs]
    m_ts = [[] for _ in hs]
    m_news = [[] for _ in hs]
    for r in chunk_rows:
        for h in hs:
            inter = bcum[r, h:h + 1] + m_in[h]
            m_t = jnp.maximum(inter, dmaxs[h][r])
            m_in[h] = m_t[last:last + 1, :]
            inters[h].append(inter)
            m_ts[h].append(m_t)
            m_news[h].append(m_in[h])
    inters = [jnp.concatenate(v, axis=0) for v in inters]
    m_ts = [jnp.concatenate(v, axis=0) for v in m_ts]
    w_intras = [jnp.exp(d - m) for d, m in zip(dmats, m_ts)]
    w_inters = [jnp.exp(i - m) for i, m in zip(inters, m_ts)]
    qs = [q_ref[:, h * ML_DK:(h + 1) * ML_DK] * ML_DK ** -0.5 for h in hs]
    ks = [k_ref[:, h * ML_DK:(h + 1) * ML_DK] for h in hs]
    vxs = [jnp.concatenate([v_ref[:, h * ML_DV:(h + 1) * ML_DV], ones_col], axis=1) for h in hs]
    scores = [_dot_nt(q, k) * w for q, k, w in zip(qs, ks, w_intras)]
    intra = [_dot(s, vx) for s, vx in zip(scores, vxs)]
    cxs = [cx_s[h] for h in hs]
    inter_num = [[] for _ in hs]
    for s, r in enumerate(chunk_rows):
        for h in hs:
            inter_num[h].append(_dot(qs[h][r], cxs[h]))
            m_new = m_news[h][s]
            w_state = jnp.exp(jnp.where(col_rows <= last, bcum[last_rows[s], h:h + 1] - bcum[r, h:h + 1]
                                        + ig[r, h:h + 1], -jnp.inf) - m_new)
            carry_decay = jnp.exp(inters[h][last_rows[s], :] - m_new)
            cxs[h] = carry_decay * cxs[h] + _dot_tn(ks[h][r] * w_state, vxs[h][r])
    m_next = m_row
    for h in hs:
        cx_s[h] = cxs[h]
        num = w_inters[h] * jnp.concatenate(inter_num[h], axis=0) + intra[h]
        den = num[:, ML_DV:ML_DV + 1]
        hh = num[:, :ML_DV] / jnp.maximum(jnp.abs(den), jnp.exp(-m_ts[h]))
        vcols = slice(h * ML_DV, (h + 1) * ML_DV)
        h_ref[:, vcols] = (hh * jax.nn.sigmoid(og_ref[:, vcols])).astype(h_ref.dtype)
        m_next = jnp.where(lane_row == h, m_in[h], m_next)
    m_s[0:1, :] = m_next

    @pl.when(t == pl.num_programs(1) - 1)
    def _():
        cout_ref[...] = cx_s[:, :, :ML_DV]
        nout_ref[...] = cx_s[:, :, ML_DV:]
        mout_ref[...] = m_s[...]


def _mlstm_seq(q, k, v, og, gates, c0, n0, m0, ibias, fbias, *, nseq, seqlen, tc, chunk, n_valid):
    nt = seqlen // tc
    row = lambda b, t: (b * nt + t, 0)
    seq3 = lambda b, t: (b, 0, 0)
    seq4 = lambda b, t: (b, 0, 0, 0)
    m = nseq * seqlen
    return pl.pallas_call(
        functools.partial(_mlstm_seq_kernel, tc=tc, chunk=chunk, n_valid=n_valid),
        grid=(nseq, nt),
        in_specs=[pl.BlockSpec((tc, ML_QK), row), pl.BlockSpec((tc, ML_QK), row),
                  pl.BlockSpec((tc, ML_V), row), pl.BlockSpec((tc, ML_V), row),
                  pl.BlockSpec((tc, 2 * LANES), row),
                  pl.BlockSpec((None, ML_HEADS, ML_DK, ML_DV), seq4),
                  pl.BlockSpec((None, ML_HEADS, ML_DK, LANES), seq4),
                  pl.BlockSpec((None, SUBLANES, LANES), seq3),
                  _const_spec((1, LANES)), _const_spec((1, LANES))],
        out_specs=[pl.BlockSpec((tc, ML_V), row),
                   pl.BlockSpec((None, ML_HEADS, ML_DK, ML_DV), seq4),
                   pl.BlockSpec((None, ML_HEADS, ML_DK, LANES), seq4),
                   pl.BlockSpec((None, SUBLANES, LANES), seq3)],
        out_shape=[jax.ShapeDtypeStruct((m, ML_V), BF16),
                   jax.ShapeDtypeStruct((nseq, ML_HEADS, ML_DK, ML_DV), F32),
                   jax.ShapeDtypeStruct((nseq, ML_HEADS, ML_DK, LANES), F32),
                   jax.ShapeDtypeStruct((nseq, SUBLANES, LANES), F32)],
        scratch_shapes=[pltpu.VMEM((ML_HEADS, ML_DK, ML_DV + LANES), F32),
                        pltpu.VMEM((SUBLANES, LANES), F32)],
        compiler_params=_compiler_params(2),
        name="mlstm_seq",
    )(q, k, v, og, gates, c0, n0, m0, ibias, fbias)


def _mla_proj_kernel(x_ref, g_ref, cos_ref, sin_ref, win_ref, qnorm_ref, kvnorm_ref, wuq_ref, wuqsw_ref, wkv_ref,
                     *o_refs, decode):
    xn = _rms(x_ref[...], g_ref[...]).astype(BF16)
    proj = jnp.dot(xn, win_ref[...], preferred_element_type=F32)
    cq = _rms(proj[:, :MLA_Q_RANK], qnorm_ref[...]).astype(BF16)
    ckv = _rms(proj[:, MLA_Q_RANK:MLA_Q_RANK + MLA_KV_RANK], kvnorm_ref[...])
    base = MLA_Q_RANK + MLA_KV_RANK
    cos = cos_ref[...]
    sin = sin_ref[...]
    kr = proj[:, base:base + LANES] * cos + proj[:, base + LANES:base + 2 * LANES] * sin
    q_main = jnp.dot(cq, wuq_ref[...], preferred_element_type=F32)
    q_swap = jnp.dot(cq, wuqsw_ref[...], preferred_element_type=F32)
    if decode:
        qlat_ref, qrope_ref, ckv_ref, kr_ref = o_refs
    else:
        qcat_ref, kcat_ref, vt_ref, ckv_ref, kr_ref = o_refs
        kv = jnp.dot(ckv.astype(BF16), wkv_ref[...], preferred_element_type=F32)
    ckv_ref[...] = ckv
    kr_ref[...] = kr[:, :MLA_ROPE]
    for h in range(MLA_HEADS):
        lo = h * 2 * LANES
        q_nope = q_main[:, lo:lo + LANES] * MLA_SCALE
        q_rope = (q_main[:, lo + LANES:lo + 2 * LANES] * cos + q_swap[:, h * LANES:(h + 1) * LANES] * sin) * MLA_SCALE
        if decode:
            qlat_ref[:, lo:lo + 2 * LANES] = _dot(q_nope, wkv_ref[h]).astype(BF16)
            qrope_ref[:, h * LANES:(h + 1) * LANES] = q_rope.astype(BF16)
        else:
            qcat_ref[:, lo:lo + LANES] = q_nope.astype(BF16)
            qcat_ref[:, lo + LANES:lo + 2 * LANES] = q_rope.astype(BF16)
            kcat_ref[:, lo:lo + LANES] = kv[:, lo:lo + LANES].astype(BF16)
            kcat_ref[:, lo + LANES:lo + 2 * LANES] = kr.astype(BF16)
            key_tile = vt_ref.shape[-1]
            for s in range(vt_ref.shape[0]):
                v_tile = kv[s * key_tile:(s + 1) * key_tile, lo + LANES:lo + 2 * LANES]
                vt_ref[s, h * MLA_V:(h + 1) * MLA_V, :] = v_tile.T.astype(BF16)


def _mla_proj(x, g, cos, sin, win, qnorm, kvnorm, wuq, wuqsw, wkv, *, tm, decode, key_tile=None):
    m = x.shape[0]
    n_pos_blocks = cos.shape[0] // tm
    row = lambda i: (i, 0)
    pos = lambda i: (i % n_pos_blocks, 0)
    wide = MLA_HEADS * 2 * LANES
    if decode:
        widths = (wide, MLA_HEADS * LANES, MLA_KV_RANK, MLA_ROPE)
        dtypes = (BF16, BF16, F32, F32)
    else:
        widths = (wide, wide, MLA_KV_RANK, MLA_ROPE)
        dtypes = (BF16, BF16, F32, F32)
    out_specs = [pl.BlockSpec((tm, n), row) for n in widths]
    out_shape = [jax.ShapeDtypeStruct((m, n), dt) for n, dt in zip(widths, dtypes)]
    if not decode:
        out_specs.insert(2, pl.BlockSpec((tm // key_tile, MLA_HEADS * MLA_V, key_tile), lambda i: (i, 0, 0)))
        out_shape.insert(2, jax.ShapeDtypeStruct((m // key_tile, MLA_HEADS * MLA_V, key_tile), BF16))
    return pl.pallas_call(
        functools.partial(_mla_proj_kernel, decode=decode),
        grid=(m // tm,),
        in_specs=[pl.BlockSpec((tm, D_MODEL), row), _const_spec((1, D_MODEL)),
                  pl.BlockSpec((tm, LANES), pos), pl.BlockSpec((tm, LANES), pos),
                  _const_spec(win.shape), _const_spec((1, MLA_Q_RANK)), _const_spec((1, MLA_KV_RANK)),
                  _const_spec(wuq.shape), _const_spec(wuqsw.shape), _const_spec(wkv.shape)],
        out_specs=out_specs,
        out_shape=out_shape,
        compiler_params=_compiler_params(1),
        name="mla_proj_decode" if decode else "mla_proj",
    )(x, g, cos, sin, win, qnorm, kvnorm, wuq, wuqsw, wkv)


def _mla_attn_kernel(q_ref, k_ref, vt_ref, o_ref, m_s, l_s, acc_s, *, tq):
    qi = pl.program_id(1)
    heads = MLA_HEADS
    wide = 2 * LANES
    m_s[...] = jnp.full(m_s.shape, -jnp.inf, F32)
    l_s[...] = jnp.zeros(l_s.shape, F32)
    acc_s[...] = jnp.zeros(acc_s.shape, F32)
    key_i = lax.broadcasted_iota(jnp.int32, (tq, tq), 0)
    query_i = lax.broadcasted_iota(jnp.int32, (tq, tq), 1)

    def key_tile(j, diagonal):
        rows = pl.ds(pl.multiple_of(j * tq, tq), tq)
        for g0 in range(0, heads, ATTN_HEAD_GROUP):
            hs = range(g0, g0 + ATTN_HEAD_GROUP)
            ss = [lax.dot_general(k_ref[rows, h * wide:(h + 1) * wide], q_ref[:, h * wide:(h + 1) * wide],
                                  (((1,), (1,)), ((), ())), preferred_element_type=F32) for h in hs]
            if diagonal:
                ss = [jnp.where(key_i <= query_i, s, -jnp.inf) for s in ss]
            m_old = [m_s[h] for h in hs]
            m_new = [jnp.maximum(m, jnp.max(s, axis=0, keepdims=True)) for m, s in zip(m_old, ss)]
            ps = [jnp.exp(s - m) for s, m in zip(ss, m_new)]
            alphas = [jnp.exp(mo - mn) for mo, mn in zip(m_old, m_new)]
            pvs = [jnp.dot(vt_ref[j, h * MLA_V:(h + 1) * MLA_V, :], p.astype(BF16), preferred_element_type=F32)
                   for p, h in zip(ps, hs)]
            for i, h in enumerate(hs):
                l_s[h] = alphas[i] * l_s[h] + jnp.sum(ps[i], axis=0, keepdims=True)
                acc_s[h] = alphas[i] * acc_s[h] + pvs[i]
                m_s[h] = m_new[i]

    def body(j, carry):
        key_tile(j, False)
        return carry

    lax.fori_loop(0, qi, body, 0)
    key_tile(qi, True)
    for h in range(heads):
        o_ref[:, h * MLA_V:(h + 1) * MLA_V] = (acc_s[h] / l_s[h]).T.astype(o_ref.dtype)


def _mla_attn(qcat, kcat, vt, *, nseq, seqlen, tq):
    nq = seqlen // tq
    m = nseq * seqlen
    return pl.pallas_call(
        functools.partial(_mla_attn_kernel, tq=tq),
        grid=(nseq, nq),
        in_specs=[pl.BlockSpec((tq, MLA_HEADS * 2 * LANES), lambda b, i: (b * nq + i, 0)),
                  pl.BlockSpec((seqlen, MLA_HEADS * 2 * LANES), lambda b, i: (b, 0)),
                  pl.BlockSpec((nq, MLA_HEADS * MLA_V, tq), lambda b, i: (b, 0, 0))],
        out_specs=pl.BlockSpec((tq, MLA_HEADS * MLA_V), lambda b, i: (b * nq + i, 0)),
        out_shape=jax.ShapeDtypeStruct((m, MLA_HEADS * MLA_V), BF16),
        scratch_shapes=[pltpu.VMEM((MLA_HEADS, 1, tq), F32), pltpu.VMEM((MLA_HEADS, 1, tq), F32),
                        pltpu.VMEM((MLA_HEADS, MLA_V, tq), F32)],
        compiler_params=_compiler_params(2),
        name="mla_attn",
    )(qcat, kcat, vt)


def _mla_decode_kernel(pt_ref, qlat_ref, qrope_ref, ckv_ref, kr_ref, wuv_ref, *rest, n_pages):
    lat_refs = rest[:n_pages]
    rope_refs = rest[n_pages:2 * n_pages]
    o_ref = rest[2 * n_pages]
    ql_s, qr_s, m_s, l_s, acc_s = rest[2 * n_pages + 1:]
    g = pl.program_id(1)
    heads = MLA_HEADS
    tok = DEC_PAD

    @pl.when(g == 0)
    def _():
        for h in range(heads):
            ql_s[h * tok:(h + 1) * tok, :] = qlat_ref[:, h * 2 * LANES:(h + 1) * 2 * LANES]
            qr_s[h * tok:(h + 1) * tok, :] = qrope_ref[:, h * LANES:(h + 1) * LANES]
        m_s[...] = jnp.full(m_s.shape, -jnp.inf, F32)
        l_s[...] = jnp.zeros(l_s.shape, F32)
        acc_s[...] = jnp.zeros(acc_s.shape, F32)

    ql = ql_s[...]
    qr = qr_s[:, :MLA_ROPE]

    def update(chains, scores, values):
        m_old = [m_s[c] for c in chains]
        m_new = [jnp.maximum(m, jnp.max(s, axis=-1, keepdims=True)) for m, s in zip(m_old, scores)]
        ps = [jnp.exp(s - m) for s, m in zip(scores, m_new)]
        alphas = [jnp.exp(mo - mn) for mo, mn in zip(m_old, m_new)]
        for i, c in enumerate(chains):
            l_s[c] = alphas[i] * l_s[c] + jnp.sum(ps[i], axis=-1, keepdims=True)
            pb = ps[i].astype(BF16)
            acc = alphas[i] * acc_s[c]
            for cols, val in values[i]:
                acc = acc + jnp.dot(pb[:, cols], val, preferred_element_type=F32)
            acc_s[c] = acc
            m_s[c] = m_new[i]

    n_chains = m_s.shape[0]
    per = n_pages // n_chains
    lats = [r[...].astype(BF16) for r in lat_refs]
    scores = [_dot_nt(ql, lat) + _dot(qr, r[...]) for lat, r in zip(lats, rope_refs)]
    update(range(n_chains),
           [jnp.concatenate(scores[c * per:(c + 1) * per], axis=1) for c in range(n_chains)],
           [[(slice(i * PAGE_SIZE, (i + 1) * PAGE_SIZE), lats[c * per + i]) for i in range(per)]
            for c in range(n_chains)])

    @pl.when(g == pl.num_programs(1) - 1)
    def _():
        ckv = ckv_ref[...].astype(BF16)
        s = _dot_nt(ql, ckv) + _dot_nt(qr, kr_ref[...])
        qt = lax.broadcasted_iota(jnp.int32, s.shape, 0) % tok
        kt = lax.broadcasted_iota(jnp.int32, s.shape, 1)
        update([0], [jnp.where(kt <= qt, s, -jnp.inf)], [[(slice(0, tok), ckv)]])
        m_all = m_s[0]
        for c in range(1, n_chains):
            m_all = jnp.maximum(m_all, m_s[c])
        weights = [jnp.exp(m_s[c] - m_all) for c in range(n_chains)]
        l_all = sum(w * l_s[c] for c, w in enumerate(weights))
        acc_all = sum(w * acc_s[c] for c, w in enumerate(weights))
        o_lat = acc_all / l_all
        for h in range(heads):
            o_ref[:, h * MLA_V:(h + 1) * MLA_V] = _dot(o_lat[h * tok:(h + 1) * tok, :], wuv_ref[h]).astype(o_ref.dtype)


def _mla_decode(page_table, qlat, qrope, ckv, kr, wuv, lat_pages, rope_pages, *, nseq):
    n_pages_seq = page_table.shape[1]
    n = math.gcd(PAGES_PER_STEP, n_pages_seq)
    groups = n_pages_seq // n
    seq = lambda b, g, pt: (b, 0)

    def page_map(i):
        return lambda b, g, pt: (pt[b * n_pages_seq + g * n + i], 0, 0)

    rows = MLA_HEADS * DEC_PAD
    grid_spec = pltpu.PrefetchScalarGridSpec(
        num_scalar_prefetch=1,
        grid=(nseq, groups),
        in_specs=[pl.BlockSpec((DEC_PAD, MLA_HEADS * 2 * LANES), seq),
                  pl.BlockSpec((DEC_PAD, MLA_HEADS * LANES), seq),
                  pl.BlockSpec((DEC_PAD, MLA_KV_RANK), seq),
                  pl.BlockSpec((DEC_PAD, MLA_ROPE), seq),
                  pl.BlockSpec(wuv.shape, lambda b, g, pt: (0, 0, 0))]
                 + [pl.BlockSpec((None, PAGE_SIZE, MLA_KV_RANK), page_map(i)) for i in range(n)]
                 + [pl.BlockSpec((None, MLA_ROPE, PAGE_SIZE), page_map(i)) for i in range(n)],
        out_specs=pl.BlockSpec((DEC_PAD, MLA_HEADS * MLA_V), seq),
        scratch_shapes=[pltpu.VMEM((rows, MLA_KV_RANK), BF16), pltpu.VMEM((rows, LANES), BF16),
                        pltpu.VMEM((DECODE_CHAINS, rows, 1), F32), pltpu.VMEM((DECODE_CHAINS, rows, 1), F32),
                        pltpu.VMEM((DECODE_CHAINS, rows, MLA_KV_RANK), F32)],
    )
    return pl.pallas_call(
        functools.partial(_mla_decode_kernel, n_pages=n),
        grid_spec=grid_spec,
        out_shape=jax.ShapeDtypeStruct((nseq * DEC_PAD, MLA_HEADS * MLA_V), BF16),
        compiler_params=_compiler_params(2),
        name="mla_decode",
    )(page_table.reshape(-1), qlat, qrope, ckv, kr, wuv, *([lat_pages] * n), *([rope_pages] * n))


def _row(v, width=None):
    v = v.astype(F32).reshape(1, -1)
    if width is not None and v.shape[1] < width:
        v = jnp.pad(v, ((0, 0), (0, width - v.shape[1])))
    return v


def _pad_cols(w, width):
    return jnp.pad(w, ((0, 0), (0, width - w.shape[1])))


def _prep_ffn(w_up, w_down):
    return w_up.astype(BF16), w_down.reshape(w_down.shape[:2] + (N_FF_CHUNKS, FF_CHUNK, D_MODEL)).astype(BF16)


def _prep_gdn(w_in, conv_w, a_log, dt_bias, norm_w, w_out):
    main = GDN_QKV + GDN_Z
    w = jnp.concatenate([w_in[:, :main], _pad_cols(w_in[:, main:main + GDN_HEADS], LANES),
                         _pad_cols(w_in[:, main + GDN_HEADS:], LANES)], axis=1).astype(BF16)
    convw = jnp.pad(conv_w.astype(F32), ((0, SUBLANES - GDN_CONV), (0, 0)))
    return w, convw, _row(a_log, LANES), _row(dt_bias, LANES), _row(norm_w), w_out.astype(BF16)


def _prep_mlstm(w_in, i_bias, f_bias, w_out):
    main = 2 * ML_QK + 2 * ML_V
    w = jnp.concatenate([w_in[:, :main], _pad_cols(w_in[:, main:main + ML_HEADS], LANES),
                         _pad_cols(w_in[:, main + ML_HEADS:], LANES)], axis=1).astype(BF16)
    return w, _row(i_bias, LANES), _row(f_bias, LANES), w_out.astype(BF16)


def _swap_halves(w):
    half = w.shape[-1] // 2
    return jnp.concatenate([w[..., half:], w[..., :half]], axis=-1)


def _prep_mla(w_in, q_norm, kv_norm, w_uq, w_ukv, w_out):
    base = MLA_Q_RANK + MLA_KV_RANK
    kr = w_in[:, base:]
    win = jnp.concatenate([w_in[:, :base], _pad_cols(kr, LANES), _pad_cols(_swap_halves(kr), LANES)],
                          axis=1).astype(BF16)
    wq = w_uq.reshape(MLA_Q_RANK, MLA_HEADS, MLA_NOPE + MLA_ROPE)
    nope, rope = wq[..., :MLA_NOPE], wq[..., MLA_NOPE:]
    zeros = jnp.zeros((MLA_Q_RANK, MLA_HEADS, LANES - MLA_ROPE), w_uq.dtype)
    wuq = jnp.concatenate([nope, rope, zeros], axis=-1).reshape(MLA_Q_RANK, -1).astype(BF16)
    wuqsw = jnp.concatenate([_swap_halves(rope), zeros], axis=-1).reshape(MLA_Q_RANK, -1).astype(BF16)
    wkv3 = w_ukv.reshape(MLA_KV_RANK, MLA_HEADS, MLA_NOPE + MLA_V)
    wuk_t = wkv3[..., :MLA_NOPE].transpose(1, 2, 0).astype(BF16)
    wuv = wkv3[..., MLA_NOPE:].transpose(1, 0, 2).astype(BF16)
    return dict(win=win, qnorm=_row(q_norm), kvnorm=_row(kv_norm), wuq=wuq, wuqsw=wuqsw,
                wkv=w_ukv.astype(BF16), wuk_t=wuk_t, wuv=wuv, wout=w_out.astype(BF16))


def _rope_tables(pos):
    half = MLA_ROPE // 2
    inv = ROPE_THETA ** (-jnp.arange(half, dtype=F32) / half)
    ang = pos.astype(F32)[:, None] * inv
    cos, sin = jnp.cos(ang), jnp.sin(ang)
    pad = jnp.zeros((pos.shape[0], LANES - MLA_ROPE), F32)
    return jnp.concatenate([cos, cos, pad], axis=1), jnp.concatenate([-sin, sin, pad], axis=1)


def _tiles(nseq, seqlen):
    m = nseq * seqlen
    return dict(ffn=min(512, m), proj=min(512, m), mla_proj=min(512, seqlen), attn_q=min(256, seqlen),
                gdn=min(2 * GDN_CHUNK, seqlen),
                gdn_group=math.gcd(GDN_SEQ_GROUP if seqlen >= 2 * GDN_CHUNK else GDN_SEQ_GROUP_SHORT, nseq),
                mlstm=min(2 * ML_CHUNK, seqlen))


def _trunk(x, *, nseq, seqlen, n_tokens, decode, states, page_table, weights):
    gdn_S, gdn_conv, mla_lat, mla_rope, ml_C, ml_n, ml_m = states
    chunk_g = min(GDN_CHUNK, seqlen)
    chunk_m = min(ML_CHUNK, seqlen)
    tiles = _tiles(nseq, seqlen)
    tm = tiles["ffn"]
    new = ([], [], [], [], [], [], [])
    counts = [0, 0, 0]
    for layer in range(4):
        kind = layer % 3
        j = counts[kind]
        counts[kind] += 1
        ng = weights["gains"][layer]
        wup, wdn = weights["ffn"]
        x = _ffn_half(x, ng[0], ng[1], wup, wdn, layer, 0, tm)
        if kind == 0:
            w, convw, alog, dtb, normw, wout = weights["gdn"][j]
            q, k, v, z, ba, conv_new = _gdn_proj(x, ng[2], w, convw, gdn_conv, layer=j, nseq=nseq, seqlen=seqlen,
                                                 n_tokens=n_tokens, tm=tiles["proj"])
            o, s_new = _gdn_seq(q, k, v, z, ba, gdn_S, alog, dtb, normw, layer=j, nseq=nseq, seqlen=seqlen,
                                tc=tiles["gdn"], chunk=chunk_g, n_valid=min(tiles["gdn"], n_tokens),
                                group=tiles["gdn_group"])
            new[0].append(s_new)
            new[1].append(conv_new[:, SUBLANES - (GDN_CONV - 1):, :])
        elif kind == 1:
            p = weights["mla"][j]
            cos, sin = weights["rope"]
            if decode:
                qlat, qrope, ckv, kr = _mla_proj(x, ng[2], cos, sin, p["win"], p["qnorm"], p["kvnorm"], p["wuq"],
                                                 p["wuqsw"], p["wuk_t"], tm=tiles["mla_proj"], decode=True)
                o = _mla_decode(page_table, qlat, qrope, ckv, kr, p["wuv"], mla_lat[j], mla_rope[j], nseq=nseq)
            else:
                qcat, kcat, vt, ckv, kr = _mla_proj(x, ng[2], cos, sin, p["win"], p["qnorm"], p["kvnorm"], p["wuq"],
                                                    p["wuqsw"], p["wkv"], tm=tiles["mla_proj"], decode=False,
                                                    key_tile=tiles["attn_q"])
                o = _mla_attn(qcat, kcat, vt, nseq=nseq, seqlen=seqlen, tq=tiles["attn_q"])
            wout = p["wout"]
            new[2].append(ckv)
            new[3].append(kr)
        else:
            w, ibias, fbias, wout = weights["mlstm"][j]
            q, k, v, og, gates = _norm_matmul(x, ng[2], w, (ML_QK, ML_QK, ML_V, ML_V, 2 * LANES), tiles["proj"])
            o, c_new, n_new, m_new = _mlstm_seq(q, k, v, og, gates, ml_C[j], ml_n[j], ml_m[j], ibias, fbias,
                                                nseq=nseq, seqlen=seqlen, tc=tiles["mlstm"], chunk=chunk_m,
                                                n_valid=min(tiles["mlstm"], n_tokens))
            new[4].append(c_new)
            new[5].append(n_new[..., 0])
            new[6].append(m_new[:, 0, :ML_HEADS])
        x = _ffn_half(x, ng[4], ng[5], wup, wdn, layer, 1, tm, mixer_out=(o, ng[3], wout))
    return x, [jnp.stack(s) for s in new]


def kernel(x_prompt, x_sample, state_gdn_S, state_gdn_conv, cache_mla_latent, cache_mla_rope, state_mlstm_C, state_mlstm_n, state_mlstm_m, page_table, norm_gains, w_ffn_up, w_ffn_down, gdn_w_in, gdn_conv_w, gdn_a_log, gdn_dt_bias, gdn_norm_w, gdn_w_out, mla_w_in, mla_q_norm, mla_kv_norm, mla_w_uq, mla_w_ukv, mla_w_out, mlstm_w_in, mlstm_i_bias, mlstm_f_bias, mlstm_w_out):
    nb, seq, _ = x_prompt.shape
    db, dseq, _ = x_sample.shape
    n_gdn, n_mla, n_ml = gdn_w_in.shape[0], mla_w_in.shape[0], mlstm_w_in.shape[0]
    past = page_table.shape[1] * PAGE_SIZE

    weights = dict(
        gains=[[_row(norm_gains[l, i]) for i in range(6)] for l in range(4)],
        ffn=_prep_ffn(w_ffn_up, w_ffn_down),
        gdn=[_prep_gdn(gdn_w_in[j], gdn_conv_w[j], gdn_a_log[j], gdn_dt_bias[j], gdn_norm_w[j], gdn_w_out[j])
             for j in range(n_gdn)],
        mla=[_prep_mla(mla_w_in[j], mla_q_norm[j], mla_kv_norm[j], mla_w_uq[j], mla_w_ukv[j], mla_w_out[j])
             for j in range(n_mla)],
        mlstm=[_prep_mlstm(mlstm_w_in[j], mlstm_i_bias[j], mlstm_f_bias[j], mlstm_w_out[j]) for j in range(n_ml)],
    )

    zeros = lambda *s: jnp.zeros(s, F32)
    states_p = (zeros(n_gdn, nb, GDN_HEADS, GDN_DK, GDN_DV), zeros(n_gdn, nb, SUBLANES, GDN_QKV), None, None,
                zeros(n_ml, nb, ML_HEADS, ML_DK, ML_DV), zeros(n_ml, nb, ML_HEADS, ML_DK, LANES),
                zeros(n_ml, nb, SUBLANES, LANES))
    y_p, st_p = _trunk(x_prompt.reshape(nb * seq, D_MODEL), nseq=nb, seqlen=seq, n_tokens=seq, decode=False,
                       states=states_p, page_table=None,
                       weights=dict(weights, rope=_rope_tables(jnp.arange(seq))))

    pad_t = DEC_PAD - dseq
    x_s = jnp.pad(x_sample, ((0, 0), (0, pad_t), (0, 0))).reshape(db * DEC_PAD, D_MODEL)
    conv0 = jnp.pad(state_gdn_conv, ((0, 0), (0, 0), (SUBLANES - (GDN_CONV - 1), 0), (0, 0)))
    n0 = jnp.pad(state_mlstm_n[..., None], ((0, 0),) * 4 + ((0, LANES - 1),))
    m0 = jnp.pad(state_mlstm_m[:, :, None, :], ((0, 0), (0, 0), (0, SUBLANES - 1), (0, LANES - ML_HEADS)))
    states_s = (state_gdn_S, conv0,
                cache_mla_latent.reshape((n_mla, -1) + cache_mla_latent.shape[2:]),
                jnp.swapaxes(cache_mla_rope.reshape((n_mla, -1) + cache_mla_rope.shape[2:]), -1, -2),
                state_mlstm_C, n0, m0)
    pos_s = jnp.tile(past + jnp.arange(DEC_PAD), db)
    y_s, st_s = _trunk(x_s, nseq=db, seqlen=DEC_PAD, n_tokens=dseq, decode=True, states=states_s,
                       page_table=page_table, weights=dict(weights, rope=_rope_tables(pos_s)))

    gdn_S_p, gdn_conv_p, lat_p, rope_p, ml_C_p, ml_n_p, ml_m_p = st_p
    gdn_S_s, gdn_conv_s, lat_s, rope_s, ml_C_s, ml_n_s, ml_m_s = st_s
    unpad = lambda a: a.reshape(a.shape[0], db, DEC_PAD, a.shape[-1])[:, :, :dseq]
    return (y_p.reshape(nb, seq, D_MODEL), unpad(y_s[None])[0],
            gdn_S_p, gdn_S_s, gdn_conv_p, gdn_conv_s,
            lat_p.reshape(n_mla, -1, PAGE_SIZE, MLA_KV_RANK), unpad(lat_s),
            rope_p.reshape(n_mla, -1, PAGE_SIZE, MLA_ROPE), unpad(rope_s),
            ml_C_p, ml_C_s, ml_n_p, ml_n_s, ml_m_p, ml_m_s)
```
